```python
import math
import jax, jax.numpy as jnp
from jax import lax
import numpy as np

D_MODEL = 1024
BATCH = 4
SEQ = 8192
DEPTH = 2

EPS = 1e-6
Q_BLOCK = 128
HEAD_DIM = 64
MIX_WIDTH = D_MODEL
A_V = 2 * HEAD_DIM
A_WIDTH = MIX_WIDTH // 2
A_HEADS = A_WIDTH // A_V
B_WINDOWS = (2, 4, 8, 16)
B_GROUPS = len(B_WINDOWS)
B_WIDTH = MIX_WIDTH - A_WIDTH
B_GROUP_DIM = B_WIDTH // B_GROUPS
EVEN_IN = 2 * (2 * A_HEADS * HEAD_DIM) + A_WIDTH + B_WIDTH
C_WIDTH = MIX_WIDTH // 2
CONV_W = 3
D_WIDTH = MIX_WIDTH - C_WIDTH
D_HEADS = D_WIDTH // HEAD_DIM
D_BRANCHES = ((128, 1), (512, 4), (2048, 16))
ODD_IN = 3 * C_WIDTH + 3 * D_WIDTH
D_FF = -(-8 * D_MODEL // (3 * 256)) * 256
N_EVEN = (DEPTH + 1) // 2
N_ODD = DEPTH // 2

kernel_name = "hybrid_diffattn_pool_shortconv_dilated"


def rmsnorm(x, g):
    xf = x.astype(jnp.float32)
    y = xf * lax.rsqrt(jnp.mean(xf * xf, axis=-1, keepdims=True) + EPS)
    return (y * g.astype(jnp.float32)).astype(x.dtype)


def alibi_slopes(n):
    return jnp.asarray(np.array([2.0 ** (-8.0 * (i + 1) / n) for i in range(n)], dtype=np.float32))


def diff_attention(q, k, v, lam, slopes):
    bsz, seq, _, nh, dk = q.shape
    n_blk = seq // Q_BLOCK
    kpos = jnp.arange(seq)
    scale = dk ** -0.5

    def block(i):
        q0 = i * Q_BLOCK
        qb = lax.dynamic_slice_in_dim(q, q0, Q_BLOCK, axis=1)
        s = jnp.einsum('bqmhd,bkmhd->bmhqk', qb, k, preferred_element_type=jnp.float32) * scale
        dist = (q0 + jnp.arange(Q_BLOCK))[:, None] - kpos[None, :]
        s = s - slopes[:, None, None] * dist.astype(jnp.float32)
        s = jnp.where(dist >= 0, s, -jnp.inf)
        p = jax.nn.softmax(s, axis=-1)
        a = p[:, 0] - lam * p[:, 1]
        return jnp.einsum('bhqk,bkhd->bqhd', a.astype(v.dtype), v)

    out = lax.map(block, jnp.arange(n_blk))
    return out.transpose(1, 0, 2, 3, 4).reshape(bsz, seq, nh, v.shape[-1])


def multiscale_pool(u, w_groups, scale):
    bsz, seq, _ = u.shape
    uf = u.astype(jnp.float32).reshape(bsz, seq, B_GROUPS, B_GROUP_DIM)
    cs = jnp.cumsum(uf, axis=1)
    pos = jnp.arange(1, seq + 1, dtype=jnp.float32)[None, :, None]
    outs = []
    for g, w in enumerate(B_WINDOWS):
        c = cs[:, :, g]
        prev = jnp.pad(c, ((0, 0), (w, 0), (0, 0)))[:, :seq]
        outs.append((c - prev) / jnp.minimum(pos, float(w)) - uf[:, :, g])
    pooled = jnp.stack(outs, axis=2).astype(u.dtype)
    y = jnp.einsum('bsgc,gcd->bsgd', pooled, w_groups).reshape(bsz, seq, B_WIDTH)
    return y * scale


def short_conv_gate(bg, cg, h, conv_w):
    z = cg * h
    z = lax.conv_general_dilated(z, conv_w[:, None, :], window_strides=(1,),
                                 padding=((CONV_W - 1, 0),),
                                 dimension_numbers=('NWC', 'WIO', 'NWC'),
                                 feature_group_count=C_WIDTH)
    return bg * z


def dilated_branch(q, k, v, slopes, window, dil):
    bsz, seq, nh, dh = q.shape
    span = window // dil
    n_prev = -(-span // Q_BLOCK)
    L = seq // dil
    Lp = -(-L // Q_BLOCK) * Q_BLOCK
    nb = Lp // Q_BLOCK
    kw = (n_prev + 1) * Q_BLOCK

    def to_classes(t):
        t = t.reshape(bsz, L, dil, nh, dh).transpose(0, 2, 1, 3, 4)
        return jnp.pad(t, ((0, 0), (0, 0), (0, Lp - L), (0, 0), (0, 0)))

    qc, kc, vc = to_classes(q), to_classes(k), to_classes(v)
    qb = qc.reshape(bsz, dil, nb, Q_BLOCK, nh, dh)

    def band(t):
        tp = jnp.pad(t, ((0, 0), (0, 0), (n_prev * Q_BLOCK, 0), (0, 0), (0, 0)))
        return jnp.concatenate(
            [tp[:, :, j * Q_BLOCK:j * Q_BLOCK + Lp].reshape(bsz, dil, nb, Q_BLOCK, nh, dh)
             for j in range(n_prev + 1)], axis=3)

    kb, vb = band(kc), band(vc)
    qi = jnp.arange(Q_BLOCK)[:, None]
    ki = jnp.arange(kw)[None, :]
    delta = n_prev * Q_BLOCK + qi - ki
    k_idx = (jnp.arange(nb)[:, None, None] - n_prev) * Q_BLOCK + ki[None]
    valid = (delta >= 0) & (delta <= span) & (k_idx >= 0)
    s = jnp.einsum('bcnqhd,bcnkhd->bcnhqk', qb, kb, preferred_element_type=jnp.float32) * (dh ** -0.5)
    s = s - slopes[:, None, None] * (delta * dil).astype(jnp.float32)
    s = jnp.where(valid[:, None], s, -jnp.inf)
    m = jnp.max(s, axis=-1, keepdims=True)
    e = jnp.exp(s - m)
    den = jnp.sum(e, axis=-1, keepdims=True)
    o = jnp.einsum('bcnhqk,bcnkhd->bcnqhd', (e / den).astype(v.dtype), vb)
    lse = (m + jnp.log(den))[..., 0]
    o = o.reshape(bsz, dil, Lp, nh, dh)[:, :, :L].transpose(0, 2, 1, 3, 4).reshape(bsz, seq, nh, dh)
    lse = lse.transpose(0, 1, 2, 4, 3).reshape(bsz, dil, Lp, nh)[:, :, :L]
    lse = lse.transpose(0, 2, 1, 3).reshape(bsz, seq, nh)
    return o, lse


def dilated_attention(q, k, v, slopes):
    outs, lses = [], []
    for window, dil in D_BRANCHES:
        o, lse = dilated_branch(q, k, v, slopes, window, dil)
        outs.append(o)
        lses.append(lse)
    wts = jax.nn.softmax(jnp.stack(lses, axis=0), axis=0)
    o = jnp.stack(outs, axis=0)
    return jnp.sum(wts[..., None].astype(o.dtype) * o, axis=0)


def swiglu_ffn(x, g, w_gate, w_up, w_down):
    h = rmsnorm(x, g)
    return (jax.nn.silu(h @ w_gate) * (h @ w_up)) @ w_down


def setup_inputs(seed: int = 0) -> dict:
    key = jax.random.key(seed)
    ks = jax.random.split(key, 24)
    f32 = jnp.float32

    def nrm(k, shape, scale):
        return jax.random.normal(k, shape, f32) * scale

    return {
        "x": nrm(ks[0], (BATCH, SEQ, D_MODEL), 1.0),
        "norm_mix": 1.0 + nrm(ks[1], (DEPTH, D_MODEL), 0.05),
        "norm_ffn": 1.0 + nrm(ks[2], (DEPTH, D_MODEL), 0.05),
        "ev_w_in": nrm(ks[3], (N_EVEN, D_MODEL, EVEN_IN), D_MODEL ** -0.5),
        "ev_w_out": nrm(ks[4], (N_EVEN, MIX_WIDTH, D_MODEL), MIX_WIDTH ** -0.5),
        "ev_q_gain": 1.0 + nrm(ks[5], (N_EVEN, HEAD_DIM), 0.05),
        "ev_k_gain": 1.0 + nrm(ks[6], (N_EVEN, HEAD_DIM), 0.05),
        "ev_lambda_q1": nrm(ks[7], (N_EVEN, HEAD_DIM), 0.1),
        "ev_lambda_k1": nrm(ks[8], (N_EVEN, HEAD_DIM), 0.1),
        "ev_lambda_q2": nrm(ks[9], (N_EVEN, HEAD_DIM), 0.1),
        "ev_lambda_k2": nrm(ks[10], (N_EVEN, HEAD_DIM), 0.1),
        "ev_subln_gain": 1.0 + nrm(ks[11], (N_EVEN, A_V), 0.05),
        "ev_pool_w": nrm(ks[12], (N_EVEN, B_GROUPS, B_GROUP_DIM, B_GROUP_DIM), B_GROUP_DIM ** -0.5),
        "ev_pool_scale": 1.0 + nrm(ks[13], (N_EVEN, B_WIDTH), 0.1),
        "od_w_in": nrm(ks[14], (N_ODD, D_MODEL, ODD_IN), D_MODEL ** -0.5),
        "od_w_out": nrm(ks[15], (N_ODD, MIX_WIDTH, D_MODEL), MIX_WIDTH ** -0.5),
        "od_conv_w": nrm(ks[16], (N_ODD, CONV_W, C_WIDTH), CONV_W ** -0.5),
        "od_q_gain": 1.0 + nrm(ks[17], (N_ODD, HEAD_DIM), 0.05),
        "od_k_gain": 1.0 + nrm(ks[18], (N_ODD, HEAD_DIM), 0.05),
        "ffn_w_gate": nrm(ks[19], (DEPTH, D_MODEL, D_FF), D_MODEL ** -0.5),
        "ffn_w_up": nrm(ks[20], (DEPTH, D_MODEL, D_FF), D_MODEL ** -0.5),
        "ffn_w_down": nrm(ks[21], (DEPTH, D_FF, D_MODEL), D_FF ** -0.5),
    }


def reference(x, norm_mix, norm_ffn, ev_w_in, ev_w_out, ev_q_gain, ev_k_gain,
              ev_lambda_q1, ev_lambda_k1, ev_lambda_q2, ev_lambda_k2, ev_subln_gain,
              ev_pool_w, ev_pool_scale, od_w_in, od_w_out, od_conv_w, od_q_gain, od_k_gain,
              ffn_w_gate, ffn_w_up, ffn_w_down):
    bsz, seq, _ = x.shape
    slopes_a = alibi_slopes(A_HEADS)
    slopes_d = alibi_slopes(D_HEADS)
    qk_w = A_HEADS * 2 * HEAD_DIM
    for layer in range(DEPTH):
        h = rmsnorm(x, norm_mix[layer])
        if layer % 2 == 0:
            i = layer // 2
            lambda_init = 0.8 - 0.6 * math.exp(-0.3 * layer)
            proj = h @ ev_w_in[i]
            q, k, v, u = jnp.split(proj, [qk_w, 2 * qk_w, 2 * qk_w + A_WIDTH], axis=-1)
            q = rmsnorm(q.reshape(bsz, seq, 2, A_HEADS, HEAD_DIM), ev_q_gain[i])
            k = rmsnorm(k.reshape(bsz, seq, 2, A_HEADS, HEAD_DIM), ev_k_gain[i])
            v = v.reshape(bsz, seq, A_HEADS, A_V)
            lam = (jnp.exp(jnp.sum(ev_lambda_q1[i].astype(jnp.float32) * ev_lambda_k1[i].astype(jnp.float32)))
                   - jnp.exp(jnp.sum(ev_lambda_q2[i].astype(jnp.float32) * ev_lambda_k2[i].astype(jnp.float32)))
                   + lambda_init)
            ya = diff_attention(q, k, v, lam, slopes_a)
            ya = (rmsnorm(ya, ev_subln_gain[i]) * (1.0 - lambda_init)).reshape(bsz, seq, A_WIDTH)
            yb = multiscale_pool(u, ev_pool_w[i], ev_pool_scale[i])
            x = x + jnp.concatenate([ya, yb], axis=-1) @ ev_w_out[i]
        else:
            i = layer // 2
            proj = h @ od_w_in[i]
            bg, cg, hc, q, k, v = jnp.split(
                proj, [C_WIDTH, 2 * C_WIDTH, 3 * C_WIDTH, 3 * C_WIDTH + D_WIDTH,
                       3 * C_WIDTH + 2 * D_WIDTH], axis=-1)
            yc = short_conv_gate(bg, cg, hc, od_conv_w[i])
            q = rmsnorm(q.reshape(bsz, seq, D_HEADS, HEAD_DIM), od_q_gain[i])
            k = rmsnorm(k.reshape(bsz, seq, D_HEADS, HEAD_DIM), od_k_gain[i])
            v = v.reshape(bsz, seq, D_HEADS, HEAD_DIM)
            yd = dilated_attention(q, k, v, slopes_d).reshape(bsz, seq, D_WIDTH)
            x = x + jnp.concatenate([yc, yd], axis=-1) @ od_w_out[i]
        x = x + swiglu_ffn(x, norm_ffn[layer], ffn_w_gate[layer], ffn_w_up[layer], ffn_w_down[layer])
    return x
```

```python
import functools
import math

import jax
import jax.numpy as jnp
from jax import lax
from jax.experimental import pallas as pl
from jax.experimental.pallas import tpu as pltpu

D_MODEL = 1024
HEAD_DIM = 64
EPS = 1e-6
A_HEADS = 4
A_V = 128
B_WINDOWS = (2, 4, 8, 16)
POOL_HALO = 16
CONV_HALO = 8
D_HEADS = 8
D_BRANCHES = ((128, 1), (512, 4), (2048, 16))
D_SPAN = 128
D_FF = 2816
LAMBDA_INIT_L0 = 0.8 - 0.6 * math.exp(-0.3 * 0)
NEG = -1e30

LANES = 128
VMEM_LIMIT = 56 * 1024 * 1024

TM_PROJ = 512
TQ = 512
TL = 512
SUB = 128

_NT = (((1,), (1,)), ((), ()))


def _dot(a, b):
    return jnp.dot(a, b, preferred_element_type=jnp.float32)


def _dot_nt(a, b):
    return lax.dot_general(a, b, _NT, preferred_element_type=jnp.float32)


def _rms_scale(x):
    return lax.rsqrt(jnp.mean(x * x, axis=-1, keepdims=True) + EPS)


def _group_rmsnorm(t, gsum_ref, gain_ref):
    ssq = _dot((t * t).astype(jnp.bfloat16), gsum_ref[...])
    return t * lax.rsqrt(ssq * (1.0 / HEAD_DIM) + EPS) * gain_ref[...]


def _shift_rows(ext, d):
    return pltpu.roll(ext, d, axis=0)


def _even_in_kernel(x_ref, g_ref, w_ref, gsum_ref, qg_ref, kg_ref, pw_ref, ps_ref,
                    q_ref, k_ref, v_ref, yb_ref, carry_ref, *, tm, blocks_per_seq):
    blk = pl.program_id(0) % blocks_per_seq

    @pl.when(blk == 0)
    def _():
        carry_ref[...] = jnp.zeros_like(carry_ref)

    x = x_ref[...]
    h = (x * _rms_scale(x) * g_ref[...]).astype(jnp.bfloat16)
    proj = _dot(h, w_ref[...])
    q_ref[...] = _group_rmsnorm(proj[:, 0:512], gsum_ref, qg_ref).astype(jnp.bfloat16)
    k_ref[...] = _group_rmsnorm(proj[:, 512:1024], gsum_ref, kg_ref).astype(jnp.bfloat16)
    v_ref[...] = proj[:, 1024:1536].astype(jnp.bfloat16)

    u = proj[:, 1536:2048]
    ext = jnp.concatenate([carry_ref[...], u], axis=0)
    pos1 = (lax.broadcasted_iota(jnp.int32, (tm, 1), 0) + (blk * tm + 1)).astype(jnp.float32)
    for g, w in enumerate(B_WINDOWS):
        cols = slice(g * LANES, (g + 1) * LANES)
        s = ext[:, cols]
        d = 1
        while d < w:
            s = s + _shift_rows(s, d)
            d *= 2
        pooled = s[POOL_HALO:, :] / jnp.minimum(pos1, float(w)) - u[:, cols]
        yb = _dot(pooled.astype(jnp.bfloat16), pw_ref[g]) * ps_ref[:, cols]
        yb_ref[:, cols] = yb.astype(jnp.bfloat16)
    carry_ref[...] = u[tm - POOL_HALO:, :]


def _even_in(x2d, g, w, gsum, qg, kg, pw, ps, *, seq):
    t = x2d.shape[0]
    tm = TM_PROJ
    const = lambda i: (0, 0)
    row = lambda i: (i, 0)
    out = jax.ShapeDtypeStruct((t, 512), jnp.bfloat16)
    return pl.pallas_call(
        functools.partial(_even_in_kernel, tm=tm, blocks_per_seq=seq // tm),
        grid=(t // tm,),
        in_specs=[
            pl.BlockSpec((tm, D_MODEL), row),
            pl.BlockSpec((1, D_MODEL), const),
            pl.BlockSpec((D_MODEL, 2048), const),
            pl.BlockSpec((512, 512), const),
            pl.BlockSpec((1, 512), const),
            pl.BlockSpec((1, 512), const),
            pl.BlockSpec((4, LANES, LANES), lambda i: (0, 0, 0)),
            pl.BlockSpec((1, 512), const),
        ],
        out_specs=[pl.BlockSpec((tm, 512), row)] * 4,
        out_shape=[out] * 4,
        scratch_shapes=[pltpu.VMEM((POOL_HALO, 512), jnp.float32)],
        compiler_params=pltpu.CompilerParams(
            dimension_semantics=("arbitrary",), vmem_limit_bytes=VMEM_LIMIT),
        name="even_in_proj",
    )(x2d, g, w, gsum, qg, kg, pw, ps)


def _diff_attn_kernel(q_ref, k_ref, v_ref, bfull_ref, bdiag_ref, slope_ref, lam_ref, sg_ref,
                      o_ref, *, tq):
    qi = pl.program_id(2)
    lane = lax.broadcasted_iota(jnp.int32, (1, LANES), 1)
    q = q_ref[...]
    zero = jnp.zeros_like(q)
    qs = (jnp.where(lane < HEAD_DIM, q, zero), jnp.where(lane >= HEAD_DIM, q, zero))

    def scores(j):
        kb = k_ref[pl.ds(pl.multiple_of(j * tq, tq), tq), :]
        vb = v_ref[pl.ds(pl.multiple_of(j * tq, tq), tq), :]
        return [_dot_nt(qm, kb) for qm in qs], vb

    s_list, vb = scores(qi)
    state = []
    for s in s_list:
        s = s + bdiag_ref[...]
        m = jnp.max(s, axis=-1, keepdims=True)
        p = jnp.exp(s - m)
        state += [m, jnp.sum(p, axis=-1, keepdims=True), _dot(p.astype(jnp.bfloat16), vb)]

    slope = slope_ref[:, 0:1]

    def body(j, state):
        s_list, vb = scores(j)
        c = slope * ((qi - j) * tq).astype(jnp.float32)
        new = []
        for idx, s in enumerate(s_list):
            m, l, acc = state[3 * idx:3 * idx + 3]
            s = s + bfull_ref[...]
            m_new = jnp.maximum(m, jnp.max(s, axis=-1, keepdims=True) - c)
            alpha = jnp.exp(m - m_new)
            p = jnp.exp(s - (m_new + c))
            l = alpha * l + jnp.sum(p, axis=-1, keepdims=True)
            acc = alpha * acc + _dot(p.astype(jnp.bfloat16), vb)
            new += [m_new, l, acc]
        return tuple(new)

    _, l1, acc1, _, l2, acc2 = lax.fori_loop(0, qi, body, tuple(state))

    lv = lam_ref[...]
    lam = (jnp.exp(jnp.sum(lv[0:1] * lv[1:2], axis=-1, keepdims=True))
           - jnp.exp(jnp.sum(lv[2:3] * lv[3:4], axis=-1, keepdims=True)) + LAMBDA_INIT_L0)
    o = acc1 / l1 - lam * (acc2 / l2)
    o = o * _rms_scale(o) * sg_ref[...] * (1.0 - LAMBDA_INIT_L0)
    o_ref[...] = o.astype(jnp.bfloat16)


def _diff_attn(q, k, v, bfull, bdiag, slopes, lam_vecs, subln_gain, *, bsz, seq):
    tq = TQ
    nq = seq // tq
    return pl.pallas_call(
        functools.partial(_diff_attn_kernel, tq=tq),
        grid=(bsz, A_HEADS, nq),
        in_specs=[
            pl.BlockSpec((tq, LANES), lambda b, h, i: (b * nq + i, h)),
            pl.BlockSpec((seq, LANES), lambda b, h, i: (b, h)),
            pl.BlockSpec((seq, LANES), lambda b, h, i: (b, h)),
            pl.BlockSpec((None, tq, tq), lambda b, h, i: (h, 0, 0)),
            pl.BlockSpec((None, tq, tq), lambda b, h, i: (h, 0, 0)),
            pl.BlockSpec((None, 1, LANES), lambda b, h, i: (h, 0, 0)),
            pl.BlockSpec((4, HEAD_DIM), lambda b, h, i: (0, 0)),
            pl.BlockSpec((1, A_V), lambda b, h, i: (0, 0)),
        ],
        out_specs=pl.BlockSpec((tq, LANES), lambda b, h, i: (b * nq + i, h)),
        out_shape=jax.ShapeDtypeStruct((bsz * seq, A_HEADS * A_V), jnp.bfloat16),
        compiler_params=pltpu.CompilerParams(
            dimension_semantics=("arbitrary",) * 3, vmem_limit_bytes=VMEM_LIMIT),
        name="diff_attention",
    )(q, k, v, bfull, bdiag, slopes, lam_vecs, subln_gain)


def _ffn_tail(x1, g_ref, wg_ref, wu_ref, wd_ref):
    h = (x1 * _rms_scale(x1) * g_ref[...]).astype(jnp.bfloat16)
    gate = _dot(h, wg_ref[...])
    up = _dot(h, wu_ref[...])
    act = (gate * jax.nn.sigmoid(gate) * up).astype(jnp.bfloat16)
    return x1 + _dot(act, wd_ref[...])


def _even_post_kernel(x_ref, ya_ref, yb_ref, wo_ref, g_ref, wg_ref, wu_ref, wd_ref, o_ref):
    x1 = x_ref[...] + _dot(ya_ref[...], wo_ref[0:512, :]) + _dot(yb_ref[...], wo_ref[512:1024, :])
    o_ref[...] = _ffn_tail(x1, g_ref, wg_ref, wu_ref, wd_ref)


def _odd_post_kernel(x_ref, yc_ref, o1_ref, o2_ref, o3_ref, l1_ref, l2_ref, l3_ref,
                     wo_ref, g_ref, wg_ref, wu_ref, wd_ref, o_ref):
    lses = [l1_ref[...], l2_ref[...], l3_ref[...]]
    mx = jnp.maximum(jnp.maximum(lses[0], lses[1]), lses[2])
    es = [jnp.exp(l - mx) for l in lses]
    inv = 1.0 / (es[0] + es[1] + es[2])
    ws = [e * inv for e in es]
    lane = lax.broadcasted_iota(jnp.int32, (1, LANES), 1)
    outs = [o1_ref, o2_ref, o3_ref]
    pairs = []
    for p in range(D_HEADS // 2):
        cols = slice(p * LANES, (p + 1) * LANES)
        yd = None
        for w, o in zip(ws, outs):
            wf = jnp.where(lane < HEAD_DIM, w[:, 2 * p:2 * p + 1], w[:, 2 * p + 1:2 * p + 2])
            term = wf * o[:, cols].astype(jnp.float32)
            yd = term if yd is None else yd + term
        pairs.append(yd.astype(jnp.bfloat16))
    yd = jnp.concatenate(pairs, axis=1)
    x1 = x_ref[...] + _dot(yc_ref[...], wo_ref[0:512, :]) + _dot(yd, wo_ref[512:1024, :])
    o_ref[...] = _ffn_tail(x1, g_ref, wg_ref, wu_ref, wd_ref)


def _post(kernel_fn, x2d, row_inputs, wo, g, wg, wu, wd, name):
    t = x2d.shape[0]
    tm = TM_PROJ
    row = lambda i: (i, 0)
    const = lambda i: (0, 0)
    resident = functools.partial(pl.BlockSpec, index_map=const, pipeline_mode=pl.Buffered(1))
    in_specs = [pl.BlockSpec((tm, D_MODEL), row)]
    in_specs += [pl.BlockSpec((tm, a.shape[1]), row) for a in row_inputs]
    in_specs += [
        resident((D_MODEL, D_MODEL)),
        pl.BlockSpec((1, D_MODEL), const),
        resident((D_MODEL, D_FF)),
        resident((D_MODEL, D_FF)),
        resident((D_FF, D_MODEL)),
    ]
    return pl.pallas_call(
        kernel_fn,
        grid=(t // tm,),
        in_specs=in_specs,
        out_specs=pl.BlockSpec((tm, D_MODEL), row),
        out_shape=jax.ShapeDtypeStruct((t, D_MODEL), jnp.float32),
        compiler_params=pltpu.CompilerParams(
            dimension_semantics=("arbitrary",), vmem_limit_bytes=VMEM_LIMIT),
        name=name,
    )(x2d, *row_inputs, wo, g, wg, wu, wd)


def _odd_in_kernel(x_ref, g_ref, w_ref, gsum_ref, qg_ref, kg_ref, cw_ref,
                   yc_ref, q_ref, k_ref, v_ref, carry_ref, *, tm, blocks_per_seq):
    blk = pl.program_id(0) % blocks_per_seq

    @pl.when(blk == 0)
    def _():
        carry_ref[...] = jnp.zeros_like(carry_ref)

    x = x_ref[...]
    h = (x * _rms_scale(x) * g_ref[...]).astype(jnp.bfloat16)
    proj = _dot(h, w_ref[...])
    bg = proj[:, 0:512]
    z = proj[:, 512:1024] * proj[:, 1024:1536]
    ext = jnp.concatenate([carry_ref[...], z], axis=0)
    z1 = _shift_rows(ext, 1)[CONV_HALO:, :]
    z2 = _shift_rows(ext, 2)[CONV_HALO:, :]
    cw = cw_ref[...]
    yc = bg * (cw[0:1] * z2 + cw[1:2] * z1 + cw[2:3] * z)
    yc_ref[...] = yc.astype(jnp.bfloat16)
    carry_ref[...] = z[tm - CONV_HALO:, :]

    q_ref[...] = _group_rmsnorm(proj[:, 1536:2048], gsum_ref, qg_ref).astype(jnp.bfloat16)
    k_ref[...] = _group_rmsnorm(proj[:, 2048:2560], gsum_ref, kg_ref).astype(jnp.bfloat16)
    v_ref[...] = proj[:, 2560:3072].astype(jnp.bfloat16)


def _odd_in(x2d, g, w, gsum, qg, kg, cw, *, seq):
    t = x2d.shape[0]
    tm = TM_PROJ
    const = lambda i: (0, 0)
    row = lambda i: (i, 0)
    out = jax.ShapeDtypeStruct((t, 512), jnp.bfloat16)
    return pl.pallas_call(
        functools.partial(_odd_in_kernel, tm=tm, blocks_per_seq=seq // tm),
        grid=(t // tm,),
        in_specs=[
            pl.BlockSpec((tm, D_MODEL), row),
            pl.BlockSpec((1, D_MODEL), const),
            pl.BlockSpec((D_MODEL, 3072), const),
            pl.BlockSpec((512, 512), const),
            pl.BlockSpec((1, 512), const),
            pl.BlockSpec((1, 512), const),
            pl.BlockSpec((3, 512), const),
        ],
        out_specs=[pl.BlockSpec((tm, 512), row)] * 4,
        out_shape=[out] * 4,
        scratch_shapes=[pltpu.VMEM((CONV_HALO, 512), jnp.float32)],
        compiler_params=pltpu.CompilerParams(
            dimension_semantics=("arbitrary",), vmem_limit_bytes=VMEM_LIMIT),
        name="odd_in_proj",
    )(x2d, g, w, gsum, qg, kg, cw)


def _dilated_kernel(q_ref, kh_ref, kc_ref, vh_ref, vc_ref, bias_ref, o_ref, lse_ref,
                    kbuf, vbuf, *, tl):
    j = pl.program_id(2)
    kbuf[0:D_SPAN, :] = kh_ref[...]
    kbuf[D_SPAN:, :] = kc_ref[...]
    vbuf[0:D_SPAN, :] = vh_ref[...]
    vbuf[D_SPAN:, :] = vc_ref[...]
    lane = lax.broadcasted_iota(jnp.int32, (1, LANES), 1)
    kk = lax.broadcasted_iota(jnp.int32, (1, SUB + D_SPAN), 1)
    halo_pen = jnp.where(kk < D_SPAN, jnp.where(j == 0, NEG, 0.0), 0.0)
    for qs in range(tl // SUB):
        rows = slice(qs * SUB, (qs + 1) * SUB)
        krows = slice(qs * SUB, (qs + 1) * SUB + D_SPAN)
        lse_tile = jnp.zeros((SUB, LANES), jnp.float32)
        for p in range(D_HEADS // 2):
            cols = slice(p * LANES, (p + 1) * LANES)
            q = q_ref[rows, cols]
            kp = kbuf[krows, cols]
            vp = vbuf[krows, cols]
            zero = jnp.zeros_like(q)
            halves = []
            for e in range(2):
                hd = 2 * p + e
                qm = jnp.where((lane < HEAD_DIM) if e == 0 else (lane >= HEAD_DIM), q, zero)
                s = _dot_nt(qm, kp) + bias_ref[hd]
                if qs == 0:
                    s = s + halo_pen
                m = jnp.max(s, axis=-1, keepdims=True)
                pe = jnp.exp(s - m)
                l = jnp.sum(pe, axis=-1, keepdims=True)
                halves.append(_dot(pe.astype(jnp.bfloat16), vp) / l)
                lse_tile = jnp.where(lane == hd, m + jnp.log(l), lse_tile)
            o_ref[rows, cols] = jnp.where(lane < HEAD_DIM, halves[0], halves[1]).astype(jnp.bfloat16)
        lse_ref[rows, :] = lse_tile


def _dilated_branch(q, k, v, bias, *, bsz, seq, dil):
    tl = TL
    length = seq // dil
    nb = length // tl
    halo_per_blk = tl // D_SPAN
    width = D_HEADS * HEAD_DIM
    view = lambda a: a.reshape(bsz, length, dil * width)
    cur = lambda b, r, j: (b, j, r)
    halo = lambda b, r, j: (b, jnp.maximum(j * halo_per_blk - 1, 0), r)
    o, lse = pl.pallas_call(
        functools.partial(_dilated_kernel, tl=tl),
        grid=(bsz, dil, nb),
        in_specs=[
            pl.BlockSpec((None, tl, width), cur),
            pl.BlockSpec((None, D_SPAN, width), halo),
            pl.BlockSpec((None, tl, width), cur),
            pl.BlockSpec((None, D_SPAN, width), halo),
            pl.BlockSpec((None, tl, width), cur),
            pl.BlockSpec((D_HEADS, SUB, SUB + D_SPAN), lambda b, r, j: (0, 0, 0)),
        ],
        out_specs=[
            pl.BlockSpec((None, tl, width), cur),
            pl.BlockSpec((None, tl, LANES), cur),
        ],
        out_shape=[
            jax.ShapeDtypeStruct((bsz, length, dil * width), jnp.bfloat16),
            jax.ShapeDtypeStruct((bsz, length, dil * LANES), jnp.float32),
        ],
        scratch_shapes=[pltpu.VMEM((tl + D_SPAN, width), jnp.bfloat16)] * 2,
        compiler_params=pltpu.CompilerParams(
            dimension_semantics=("arbitrary",) * 3, vmem_limit_bytes=VMEM_LIMIT),
        name=f"dilated_attention_d{dil}",
    )(view(q), view(k), view(k), view(v), view(v), bias)
    return o.reshape(bsz * seq, width), lse.reshape(bsz * seq, LANES)


def _alibi_slopes(n):
    return [2.0 ** (-8.0 * (i + 1) / n) for i in range(n)]


def _diff_bias_tables():
    ii = lax.broadcasted_iota(jnp.int32, (TQ, TQ), 0)
    jj = lax.broadcasted_iota(jnp.int32, (TQ, TQ), 1)
    rel = (jj - ii).astype(jnp.float32)
    slopes = jnp.asarray(_alibi_slopes(A_HEADS), jnp.float32)
    full = slopes[:, None, None] * rel[None]
    diag = jnp.where((jj <= ii)[None], full, NEG)
    return full, diag, jnp.broadcast_to(slopes[:, None, None], (A_HEADS, 1, LANES))


def _dilated_bias_table(dil):
    ii = lax.broadcasted_iota(jnp.int32, (SUB, SUB + D_SPAN), 0)
    kk = lax.broadcasted_iota(jnp.int32, (SUB, SUB + D_SPAN), 1)
    delta = D_SPAN + ii - kk
    slopes = jnp.asarray(_alibi_slopes(D_HEADS), jnp.float32)
    bias = -slopes[:, None, None] * (delta * dil).astype(jnp.float32)[None]
    return jnp.where(((delta >= 0) & (delta <= D_SPAN))[None], bias, NEG)


def kernel(x, norm_mix, norm_ffn, ev_w_in, ev_w_out, ev_q_gain, ev_k_gain, ev_lambda_q1, ev_lambda_k1, ev_lambda_q2, ev_lambda_k2, ev_subln_gain, ev_pool_w, ev_pool_scale, od_w_in, od_w_out, od_conv_w, od_q_gain, od_k_gain, ffn_w_gate, ffn_w_up, ffn_w_down):
    bsz, seq, _ = x.shape
    bf = jnp.bfloat16
    f32 = jnp.float32
    scale = HEAD_DIM ** -0.5
    x2d = x.reshape(bsz * seq, D_MODEL)
    grp = lax.broadcasted_iota(jnp.int32, (512, 512), 0) // HEAD_DIM
    gsum = (grp == grp.T).astype(bf)
    tile8 = lambda gain, s: (jnp.tile(gain.astype(f32), 8) * s).reshape(1, 512)

    w = ev_w_in[0]
    regroup = lambda c: c.reshape(D_MODEL, 2, A_HEADS, HEAD_DIM).transpose(0, 2, 1, 3).reshape(D_MODEL, 512)
    w_even = jnp.concatenate([regroup(w[:, 0:512]), regroup(w[:, 512:1024]), w[:, 1024:]], axis=1).astype(bf)
    q, k, v, yb = _even_in(
        x2d, norm_mix[0].reshape(1, D_MODEL).astype(f32), w_even, gsum,
        tile8(ev_q_gain[0], scale), tile8(ev_k_gain[0], 1.0),
        ev_pool_w[0].astype(bf), ev_pool_scale[0].reshape(1, 512).astype(f32), seq=seq)
    bfull, bdiag, slopes = _diff_bias_tables()
    lam_vecs = jnp.stack([ev_lambda_q1[0], ev_lambda_k1[0], ev_lambda_q2[0], ev_lambda_k2[0]]).astype(f32)
    ya = _diff_attn(q, k, v, bfull, bdiag, slopes, lam_vecs,
                    ev_subln_gain[0].reshape(1, A_V).astype(f32), bsz=bsz, seq=seq)
    x2d = _post(_even_post_kernel, x2d, [ya, yb], ev_w_out[0].astype(bf),
                norm_ffn[0].reshape(1, D_MODEL).astype(f32), ffn_w_gate[0].astype(bf),
                ffn_w_up[0].astype(bf), ffn_w_down[0].astype(bf), "even_post")

    yc, q, k, v = _odd_in(
        x2d, norm_mix[1].reshape(1, D_MODEL).astype(f32), od_w_in[0].astype(bf), gsum,
        tile8(od_q_gain[0], scale), tile8(od_k_gain[0], 1.0), od_conv_w[0].astype(f32), seq=seq)
    branch = []
    for _, dil in D_BRANCHES:
        branch.append(_dilated_branch(q, k, v, _dilated_bias_table(dil), bsz=bsz, seq=seq, dil=dil))
    row_inputs = [yc] + [o for o, _ in branch] + [l for _, l in branch]
    x2d = _post(_odd_post_kernel, x2d, row_inputs, od_w_out[0].astype(bf),
                norm_ffn[1].reshape(1, D_MODEL).astype(f32), ffn_w_gate[1].astype(bf),
                ffn_w_up[1].astype(bf), ffn_w_down[1].astype(bf), "odd_post")
    return x2d.reshape(bsz, seq, D_MODEL)
```

```python
import functools
import math

import numpy as np
import jax
import jax.numpy as jnp
from jax import lax
from jax.experimental import pallas as pl
from jax.experimental.pallas import tpu as pltpu

D_MODEL = 1024
HEAD_DIM = 64
EPS = 1e-6
A_HEADS = 4
A_V = 128
B_WINDOWS = (2, 4, 8, 16)
POOL_HALO = 16
CONV_HALO = 8
D_HEADS = 8
D_BRANCHES = ((128, 1), (512, 4), (2048, 16))
D_SPAN = 128
D_FF = 2816
LAMBDA_INIT_L0 = 0.8 - 0.6 * math.exp(-0.3 * 0)
NEG = -1e30
LOG2E = math.log2(math.e)
LOG2E_HI = float(np.float32(LOG2E).astype(jnp.bfloat16))
LOG2E_LO = float(np.float32(LOG2E - LOG2E_HI).astype(jnp.bfloat16))
MAX_UNSHIFTED_SCORE = 60.0

LANES = 128
VMEM_LIMIT = 56 * 1024 * 1024

TM_PROJ = 512
TQ = 512
TL = 512
SUB = 128

_NT = (((1,), (1,)), ((), ()))


def _dot(a, b):
    return jnp.dot(a, b, preferred_element_type=jnp.float32)


def _dot_nt(a, b):
    return lax.dot_general(a, b, _NT, preferred_element_type=jnp.float32)


def _rms_scale(x):
    return lax.rsqrt(jnp.mean(x * x, axis=-1, keepdims=True) + EPS)


def _group_rmsnorm(t, gsum_ref, gain_ref):
    ssq = _dot((t * t).astype(jnp.bfloat16), gsum_ref[...])
    return t * lax.rsqrt(ssq * (1.0 / HEAD_DIM) + EPS) * gain_ref[...]


def _alibi_slopes(n):
    return [2.0 ** (-8.0 * (i + 1) / n) for i in range(n)]


def _alibi_lanes(pos, lane, first):
    rel = lane - first
    hi = ((pos >> 8) << 8).astype(jnp.float32)
    lo = (pos & 255).astype(jnp.float32)
    pos_sel = jnp.where((rel & 1) == 0, hi, lo)
    l_sel = jnp.where((rel & 3) < 2, LOG2E_HI, LOG2E_LO)
    first4 = (rel >= 0) & (rel < 4)
    last4 = (rel >= 4) & (rel < 8)
    return (jnp.where(last4, pos_sel, 0.0), jnp.where(first4, l_sel, 0.0),
            jnp.where(first4, pos_sel, 0.0), jnp.where(last4, l_sel, 0.0))


def _shift_rows(ext, d):
    return pltpu.roll(ext, d, axis=0)


def _even_in_kernel(x_ref, g_ref, w_ref, gsum_ref, qg_ref, kg_ref, pw_ref, ps_ref,
                    q1_ref, q2_ref, k1_ref, k2_ref, v_ref, yb_ref, carry_ref,
                    *, tm, blocks_per_seq):
    blk = pl.program_id(0) % blocks_per_seq

    @pl.when(blk == 0)
    def _():
        carry_ref[...] = jnp.zeros_like(carry_ref)

    x = x_ref[...]
    h = (x * _rms_scale(x) * g_ref[...]).astype(jnp.bfloat16)
    proj = _dot(h, w_ref[...])
    qn = _group_rmsnorm(proj[:, 0:512], gsum_ref, qg_ref)
    kn = _group_rmsnorm(proj[:, 512:1024], gsum_ref, kg_ref)
    v_ref[...] = proj[:, 1024:1536].astype(jnp.bfloat16)

    pos = lax.broadcasted_iota(jnp.int32, (tm, 1), 0) + blk * tm
    lane = lax.broadcasted_iota(jnp.int32, (1, LANES), 1)
    low = lane < HEAD_DIM
    for mp, (q_out, k_out) in enumerate(((q1_ref, k1_ref), (q2_ref, k2_ref))):
        q_pos, q_const, k_pos, k_const = _alibi_lanes(pos, lane, HEAD_DIM if mp == 0 else 0)
        data = low if mp == 0 else jnp.logical_not(low)
        for hd, slope in enumerate(_alibi_slopes(A_HEADS)):
            cols = slice(hd * LANES, (hd + 1) * LANES)
            q_out[:, cols] = jnp.where(data, qn[:, cols], q_const - slope * q_pos).astype(jnp.bfloat16)
            k_out[:, cols] = jnp.where(data, kn[:, cols], k_const + slope * k_pos).astype(jnp.bfloat16)

    u = proj[:, 1536:2048]
    ext = jnp.concatenate([carry_ref[...], u], axis=0)
    pos1 = (lax.broadcasted_iota(jnp.int32, (tm, 1), 0) + (blk * tm + 1)).astype(jnp.float32)
    for g, w in enumerate(B_WINDOWS):
        cols = slice(g * LANES, (g + 1) * LANES)
        s = ext[:, cols]
        d = 1
        while d < w:
            s = s + _shift_rows(s, d)
            d *= 2
        pooled = s[POOL_HALO:, :] / jnp.minimum(pos1, float(w)) - u[:, cols]
        yb = _dot(pooled.astype(jnp.bfloat16), pw_ref[g]) * ps_ref[:, cols]
        yb_ref[:, cols] = yb.astype(jnp.bfloat16)
    carry_ref[...] = u[tm - POOL_HALO:, :]


def _even_in(x2d, g, w, gsum, qg, kg, pw, ps, *, seq):
    t = x2d.shape[0]
    tm = TM_PROJ
    const = lambda i: (0, 0)
    row = lambda i: (i, 0)
    out = jax.ShapeDtypeStruct((t, 512), jnp.bfloat16)
    return pl.pallas_call(
        functools.partial(_even_in_kernel, tm=tm, blocks_per_seq=seq // tm),
        grid=(t // tm,),
        in_specs=[
            pl.BlockSpec((tm, D_MODEL), row),
            pl.BlockSpec((1, D_MODEL), const),
            pl.BlockSpec((D_MODEL, 2048), const),
            pl.BlockSpec((512, 512), const),
            pl.BlockSpec((1, 512), const),
            pl.BlockSpec((1, 512), const),
            pl.BlockSpec((4, LANES, LANES), lambda i: (0, 0, 0)),
            pl.BlockSpec((1, 512), const),
        ],
        out_specs=[pl.BlockSpec((tm, 512), row)] * 6,
        out_shape=[out] * 6,
        scratch_shapes=[pltpu.VMEM((POOL_HALO, 512), jnp.float32)],
        compiler_params=pltpu.CompilerParams(
            dimension_semantics=("arbitrary",), vmem_limit_bytes=VMEM_LIMIT),
        name="even_in_proj",
    )(x2d, g, w, gsum, qg, kg, pw, ps)


def _diff_attn_kernel(q1_ref, q2_ref, k1_ref, k2_ref, v_ref, lam_ref, sg_ref, o_ref,
                      *, tq, online):
    qi = pl.program_id(2)
    qs = (q1_ref[...], q2_ref[...])
    k_refs = (k1_ref, k2_ref)

    def tile(j, state):
        rows = pl.ds(pl.multiple_of(j * tq, tq), tq)
        vb = v_ref[rows, :]
        out = []
        for mp, (q, k_ref) in enumerate(zip(qs, k_refs)):
            s = _dot_nt(q, k_ref[rows, :])
            if state is None:
                ii = lax.broadcasted_iota(jnp.int32, (tq, tq), 0)
                jj = lax.broadcasted_iota(jnp.int32, (tq, tq), 1)
                s = jnp.where(jj > ii, NEG, s)
            if online:
                m_new = jnp.max(s, axis=-1, keepdims=True)
                if state is not None:
                    m, l, acc = state[3 * mp:3 * mp + 3]
                    m_new = jnp.maximum(m, m_new)
                    alpha = jnp.exp2(m - m_new)
                p = jnp.exp2(s - m_new)
                l_new = jnp.sum(p, axis=-1, keepdims=True)
                acc_new = _dot(p.astype(jnp.bfloat16), vb)
                if state is not None:
                    l_new = alpha * l + l_new
                    acc_new = alpha * acc + acc_new
                out += [m_new, l_new, acc_new]
            else:
                p = jnp.exp2(s)
                l_new = p[:, 0:LANES]
                for c in range(1, tq // LANES):
                    l_new = l_new + p[:, c * LANES:(c + 1) * LANES]
                acc_new = _dot(p.astype(jnp.bfloat16), vb)
                if state is not None:
                    l_new = state[2 * mp] + l_new
                    acc_new = state[2 * mp + 1] + acc_new
                out += [l_new, acc_new]
        return tuple(out)

    state = lax.fori_loop(0, qi, tile, tile(qi, None))
    if online:
        _, l1, acc1, _, l2, acc2 = state
    else:
        l1, acc1, l2, acc2 = state
        l1 = jnp.sum(l1, axis=-1, keepdims=True)
        l2 = jnp.sum(l2, axis=-1, keepdims=True)

    lv = lam_ref[...]
    lam = (jnp.exp(jnp.sum(lv[0:1] * lv[1:2], axis=-1, keepdims=True))
           - jnp.exp(jnp.sum(lv[2:3] * lv[3:4], axis=-1, keepdims=True)) + LAMBDA_INIT_L0)
    o = acc1 / l1 - lam * (acc2 / l2)
    o = o * _rms_scale(o) * sg_ref[...] * (1.0 - LAMBDA_INIT_L0)
    o_ref[...] = o.astype(jnp.bfloat16)


def _score_bound(q_gain, k_gain):
    return (HEAD_DIM ** 0.5) * jnp.max(jnp.abs(q_gain)) * jnp.max(jnp.abs(k_gain))


def _diff_attn(q1, q2, k1, k2, v, lam_vecs, subln_gain, *, bsz, seq, online):
    tq = TQ
    nq = seq // tq
    q_spec = pl.BlockSpec((tq, LANES), lambda b, h, i: (b * nq + i, h))
    kv_spec = pl.BlockSpec((seq, LANES), lambda b, h, i: (b, h))
    return pl.pallas_call(
        functools.partial(_diff_attn_kernel, tq=tq, online=online),
        grid=(bsz, A_HEADS, nq),
        in_specs=[
            q_spec, q_spec, kv_spec, kv_spec, kv_spec,
            pl.BlockSpec((4, HEAD_DIM), lambda b, h, i: (0, 0)),
            pl.BlockSpec((1, A_V), lambda b, h, i: (0, 0)),
        ],
        out_specs=q_spec,
        out_shape=jax.ShapeDtypeStruct((bsz * seq, A_HEADS * A_V), jnp.bfloat16),
        compiler_params=pltpu.CompilerParams(
            dimension_semantics=("arbitrary",) * 3, vmem_limit_bytes=VMEM_LIMIT),
        name="diff_attention_online" if online else "diff_attention",
    )(q1, q2, k1, k2, v, lam_vecs, subln_gain)


def _ffn_tail(x1, g_ref, wg_ref, wu_ref, wd_ref):
    h = (x1 * _rms_scale(x1) * g_ref[...]).astype(jnp.bfloat16)
    gate = _dot(h, wg_ref[...])
    up = _dot(h, wu_ref[...])
    act = (gate * jax.nn.sigmoid(gate) * up).astype(jnp.bfloat16)
    return x1 + _dot(act, wd_ref[...])


def _even_post_kernel(x_ref, ya_ref, yb_ref, wo_ref, g_ref, wg_ref, wu_ref, wd_ref, o_ref):
    x1 = x_ref[...] + _dot(ya_ref[...], wo_ref[0:512, :]) + _dot(yb_ref[...], wo_ref[512:1024, :])
    o_ref[...] = _ffn_tail(x1, g_ref, wg_ref, wu_ref, wd_ref)


def _odd_post_kernel(x_ref, yc_ref, o1_ref, o2_ref, o3_ref, l1_ref, l2_ref, l3_ref,
                     wo_ref, g_ref, wg_ref, wu_ref, wd_ref, o_ref):
    lses = [l1_ref[...], l2_ref[...], l3_ref[...]]
    mx = jnp.maximum(jnp.maximum(lses[0], lses[1]), lses[2])
    es = [jnp.exp(l - mx) for l in lses]
    inv = 1.0 / (es[0] + es[1] + es[2])
    ws = [e * inv for e in es]
    lane = lax.broadcasted_iota(jnp.int32, (1, LANES), 1)
    outs = [o1_ref, o2_ref, o3_ref]
    pairs = []
    for p in range(D_HEADS // 2):
        cols = slice(p * LANES, (p + 1) * LANES)
        yd = None
        for w, o in zip(ws, outs):
            wf = jnp.where(lane < HEAD_DIM, w[:, 2 * p:2 * p + 1], w[:, 2 * p + 1:2 * p + 2])
            term = wf * o[:, cols].astype(jnp.float32)
            yd = term if yd is None else yd + term
        pairs.append(yd.astype(jnp.bfloat16))
    yd = jnp.concatenate(pairs, axis=1)
    x1 = x_ref[...] + _dot(yc_ref[...], wo_ref[0:512, :]) + _dot(yd, wo_ref[512:1024, :])
    o_ref[...] = _ffn_tail(x1, g_ref, wg_ref, wu_ref, wd_ref)


def _post(kernel_fn, x2d, row_inputs, wo, g, wg, wu, wd, name):
    t = x2d.shape[0]
    tm = TM_PROJ
    row = lambda i: (i, 0)
    const = lambda i: (0, 0)
    resident = functools.partial(pl.BlockSpec, index_map=const, pipeline_mode=pl.Buffered(1))
    in_specs = [pl.BlockSpec((tm, D_MODEL), row)]
    in_specs += [pl.BlockSpec((tm, a.shape[1]), row) for a in row_inputs]
    in_specs += [
        resident((D_MODEL, D_MODEL)),
        pl.BlockSpec((1, D_MODEL), const),
        resident((D_MODEL, D_FF)),
        resident((D_MODEL, D_FF)),
        resident((D_FF, D_MODEL)),
    ]
    return pl.pallas_call(
        kernel_fn,
        grid=(t // tm,),
        in_specs=in_specs,
        out_specs=pl.BlockSpec((tm, D_MODEL), row),
        out_shape=jax.ShapeDtypeStruct((t, D_MODEL), jnp.float32),
        compiler_params=pltpu.CompilerParams(
            dimension_semantics=("arbitrary",), vmem_limit_bytes=VMEM_LIMIT),
        name=name,
    )(x2d, *row_inputs, wo, g, wg, wu, wd)


def _odd_in_kernel(x_ref, g_ref, w_ref, gsum_ref, qg_ref, kg_ref, cw_ref,
                   yc_ref, q_ref, k_ref, v_ref, carry_ref, *, tm, blocks_per_seq):
    blk = pl.program_id(0) % blocks_per_seq

    @pl.when(blk == 0)
    def _():
        carry_ref[...] = jnp.zeros_like(carry_ref)

    x = x_ref[...]
    h = (x * _rms_scale(x) * g_ref[...]).astype(jnp.bfloat16)
    proj = _dot(h, w_ref[...])
    bg = proj[:, 0:512]
    z = proj[:, 512:1024] * proj[:, 1024:1536]
    ext = jnp.concatenate([carry_ref[...], z], axis=0)
    z1 = _shift_rows(ext, 1)[CONV_HALO:, :]
    z2 = _shift_rows(ext, 2)[CONV_HALO:, :]
    cw = cw_ref[...]
    yc = bg * (cw[0:1] * z2 + cw[1:2] * z1 + cw[2:3] * z)
    yc_ref[...] = yc.astype(jnp.bfloat16)
    carry_ref[...] = z[tm - CONV_HALO:, :]

    q_ref[...] = _group_rmsnorm(proj[:, 1536:2048], gsum_ref, qg_ref).astype(jnp.bfloat16)
    k_ref[...] = _group_rmsnorm(proj[:, 2048:2560], gsum_ref, kg_ref).astype(jnp.bfloat16)
    v_ref[...] = proj[:, 2560:3072].astype(jnp.bfloat16)


def _odd_in(x2d, g, w, gsum, qg, kg, cw, *, seq):
    t = x2d.shape[0]
    tm = TM_PROJ
    const = lambda i: (0, 0)
    row = lambda i: (i, 0)
    out = jax.ShapeDtypeStruct((t, 512), jnp.bfloat16)
    return pl.pallas_call(
        functools.partial(_odd_in_kernel, tm=tm, blocks_per_seq=seq // tm),
        grid=(t // tm,),
        in_specs=[
            pl.BlockSpec((tm, D_MODEL), row),
            pl.BlockSpec((1, D_MODEL), const),
            pl.BlockSpec((D_MODEL, 3072), const),
            pl.BlockSpec((512, 512), const),
            pl.BlockSpec((1, 512), const),
            pl.BlockSpec((1, 512), const),
            pl.BlockSpec((3, 512), const),
        ],
        out_specs=[pl.BlockSpec((tm, 512), row)] * 4,
        out_shape=[out] * 4,
        scratch_shapes=[pltpu.VMEM((CONV_HALO, 512), jnp.float32)],
        compiler_params=pltpu.CompilerParams(
            dimension_semantics=("arbitrary",), vmem_limit_bytes=VMEM_LIMIT),
        name="odd_in_proj",
    )(x2d, g, w, gsum, qg, kg, cw)


def _dilated_kernel(q_ref, kh_ref, kc_ref, vh_ref, vc_ref, bias_ref, o_ref, lse_ref,
                    kbuf, vbuf, *, tl):
    j = pl.program_id(2)
    kbuf[0:D_SPAN, :] = kh_ref[...]
    kbuf[D_SPAN:, :] = kc_ref[...]
    vbuf[0:D_SPAN, :] = vh_ref[...]
    vbuf[D_SPAN:, :] = vc_ref[...]
    lane = lax.broadcasted_iota(jnp.int32, (1, LANES), 1)
    kk = lax.broadcasted_iota(jnp.int32, (1, SUB + D_SPAN), 1)
    halo_pen = jnp.where(kk < D_SPAN, jnp.where(j == 0, NEG, 0.0), 0.0)
    for qs in range(tl // SUB):
        rows = slice(qs * SUB, (qs + 1) * SUB)
        krows = slice(qs * SUB, (qs + 1) * SUB + D_SPAN)
        lse_tile = jnp.zeros((SUB, LANES), jnp.float32)
        for p in range(D_HEADS // 2):
            cols = slice(p * LANES, (p + 1) * LANES)
            q = q_ref[rows, cols]
            kp = kbuf[krows, cols]
            vp = vbuf[krows, cols]
            zero = jnp.zeros_like(q)
            halves = []
            for e in range(2):
                hd = 2 * p + e
                qm = jnp.where((lane < HEAD_DIM) if e == 0 else (lane >= HEAD_DIM), q, zero)
                s = _dot_nt(qm, kp) + bias_ref[hd]
                if qs == 0:
                    s = s + halo_pen
                m = jnp.max(s, axis=-1, keepdims=True)
                pe = jnp.exp(s - m)
                l = jnp.sum(pe, axis=-1, keepdims=True)
                halves.append(_dot(pe.astype(jnp.bfloat16), vp) / l)
                lse_tile = jnp.where(lane == hd, m + jnp.log(l), lse_tile)
            o_ref[rows, cols] = jnp.where(lane < HEAD_DIM, halves[0], halves[1]).astype(jnp.bfloat16)
        lse_ref[rows, :] = lse_tile


def _dilated_branch(q, k, v, bias, *, bsz, seq, dil):
    tl = TL
    length = seq // dil
    nb = length // tl
    halo_per_blk = tl // D_SPAN
    width = D_HEADS * HEAD_DIM
    view = lambda a: a.reshape(bsz, length, dil * width)
    cur = lambda b, r, j: (b, j, r)
    halo = lambda b, r, j: (b, jnp.maximum(j * halo_per_blk - 1, 0), r)
    o, lse = pl.pallas_call(
        functools.partial(_dilated_kernel, tl=tl),
        grid=(bsz, dil, nb),
        in_specs=[
            pl.BlockSpec((None, tl, width), cur),
            pl.BlockSpec((None, D_SPAN, width), halo),
            pl.BlockSpec((None, tl, width), cur),
            pl.BlockSpec((None, D_SPAN, width), halo),
            pl.BlockSpec((None, tl, width), cur),
            pl.BlockSpec((D_HEADS, SUB, SUB + D_SPAN), lambda b, r, j: (0, 0, 0)),
        ],
        out_specs=[
            pl.BlockSpec((None, tl, width), cur),
            pl.BlockSpec((None, tl, LANES), cur),
        ],
        out_shape=[
            jax.ShapeDtypeStruct((bsz, length, dil * width), jnp.bfloat16),
            jax.ShapeDtypeStruct((bsz, length, dil * LANES), jnp.float32),
        ],
        scratch_shapes=[pltpu.VMEM((tl + D_SPAN, width), jnp.bfloat16)] * 2,
        compiler_params=pltpu.CompilerParams(
            dimension_semantics=("arbitrary",) * 3, vmem_limit_bytes=VMEM_LIMIT),
        name=f"dilated_attention_d{dil}",
    )(view(q), view(k), view(k), view(v), view(v), bias)
    return o.reshape(bsz * seq, width), lse.reshape(bsz * seq, LANES)


def _dilated_bias_table(dil):
    ii = lax.broadcasted_iota(jnp.int32, (SUB, SUB + D_SPAN), 0)
    kk = lax.broadcasted_iota(jnp.int32, (SUB, SUB + D_SPAN), 1)
    delta = D_SPAN + ii - kk
    slopes = jnp.asarray(_alibi_slopes(D_HEADS), jnp.float32)
    bias = -slopes[:, None, None] * (delta * dil).astype(jnp.float32)[None]
    return jnp.where(((delta >= 0) & (delta <= D_SPAN))[None], bias, NEG)


def kernel(x, norm_mix, norm_ffn, ev_w_in, ev_w_out, ev_q_gain, ev_k_gain, ev_lambda_q1, ev_lambda_k1, ev_lambda_q2, ev_lambda_k2, ev_subln_gain, ev_pool_w, ev_pool_scale, od_w_in, od_w_out, od_conv_w, od_q_gain, od_k_gain, ffn_w_gate, ffn_w_up, ffn_w_down):
    bsz, seq, _ = x.shape
    bf = jnp.bfloat16
    f32 = jnp.float32
    scale = HEAD_DIM ** -0.5
    x2d = x.reshape(bsz * seq, D_MODEL)
    grp = lax.broadcasted_iota(jnp.int32, (512, 512), 0) // HEAD_DIM
    gsum = (grp == grp.T).astype(bf)
    tile8 = lambda gain, s: (jnp.tile(gain.astype(f32), 8) * s).reshape(1, 512)

    w = ev_w_in[0]
    regroup = lambda c: c.reshape(D_MODEL, 2, A_HEADS, HEAD_DIM).transpose(0, 2, 1, 3).reshape(D_MODEL, 512)
    w_even = jnp.concatenate([regroup(w[:, 0:512]), regroup(w[:, 512:1024]), w[:, 1024:]], axis=1).astype(bf)
    q1, q2, k1, k2, v, yb = _even_in(
        x2d, norm_mix[0].reshape(1, D_MODEL).astype(f32), w_even, gsum,
        tile8(ev_q_gain[0], scale * LOG2E), tile8(ev_k_gain[0], 1.0),
        ev_pool_w[0].astype(bf), ev_pool_scale[0].reshape(1, 512).astype(f32), seq=seq)
    lam_vecs = jnp.stack([ev_lambda_q1[0], ev_lambda_k1[0], ev_lambda_q2[0], ev_lambda_k2[0]]).astype(f32)
    attn = functools.partial(_diff_attn, q1, q2, k1, k2, v, lam_vecs,
                             ev_subln_gain[0].reshape(1, A_V).astype(f32), bsz=bsz, seq=seq)
    ya = lax.cond(_score_bound(ev_q_gain[0], ev_k_gain[0]) <= MAX_UNSHIFTED_SCORE,
                  lambda: attn(online=False), lambda: attn(online=True))
    x2d = _post(_even_post_kernel, x2d, [ya, yb], ev_w_out[0].astype(bf),
                norm_ffn[0].reshape(1, D_MODEL).astype(f32), ffn_w_gate[0].astype(bf),
                ffn_w_up[0].astype(bf), ffn_w_down[0].astype(bf), "even_post")

    yc, q, k, v = _odd_in(
        x2d, norm_mix[1].reshape(1, D_MODEL).astype(f32), od_w_in[0].astype(bf), gsum,
        tile8(od_q_gain[0], scale), tile8(od_k_gain[0], 1.0), od_conv_w[0].astype(f32), seq=seq)
    branch = []
    for _, dil in D_BRANCHES:
        branch.append(_dilated_branch(q, k, v, _dilated_bias_table(dil), bsz=bsz, seq=seq, dil=dil))
    row_inputs = [yc] + [o for o, _ in branch] + [l for _, l in branch]
    x2d = _post(_odd_post_kernel, x2d, row_inputs, od_w_out[0].astype(bf),
                norm_ffn[1].reshape(1, D_MODEL).astype(f32), ffn_w_gate[1].astype(bf),
                ffn_w_up[1].astype(bf), ffn_w_down[1].astype(bf), "odd_post")
    return x2d.reshape(bsz, seq, D_MODEL)
```

```python
import functools
import math

import numpy as np
import jax
import jax.numpy as jnp
from jax import lax
from jax.experimental import pallas as pl
from jax.experimental.pallas import tpu as pltpu

D_MODEL = 1024
HEAD_DIM = 64
EPS = 1e-6
A_HEADS = 4
A_V = 128
B_WINDOWS = (2, 4, 8, 16)
POOL_HALO = 16
CONV_HALO = 8
D_HEADS = 8
D_WIDTH = D_HEADS * HEAD_DIM
D_DILATIONS = (1, 4, 16)
D_SPAN = 128
D_FF = 2816
LAMBDA_INIT_L0 = 0.8 - 0.6 * math.exp(-0.3 * 0)
NEG = -1e30
LOG2E = math.log2(math.e)
LOG2E_HI = float(np.float32(LOG2E).astype(jnp.bfloat16))
LOG2E_LO = float(np.float32(LOG2E - LOG2E_HI).astype(jnp.bfloat16))
MAX_UNSHIFTED_SCORE = 60.0

LANES = 128
VMEM_LIMIT = 56 * 1024 * 1024

TM_PROJ = 512
TQ = 512
KV_UNROLL = 4
TL = 512
SUB = 128
DBLK = 2048
GRP = 256
NG = DBLK // GRP

_NT = (((1,), (1,)), ((), ()))


def _dot(a, b):
    return jnp.dot(a, b, preferred_element_type=jnp.float32)


def _dot_nt(a, b):
    return lax.dot_general(a, b, _NT, preferred_element_type=jnp.float32)


def _rms_scale(x):
    return lax.rsqrt(jnp.mean(x * x, axis=-1, keepdims=True) + EPS)


def _group_rmsnorm(t, gsum_ref, gain_ref):
    ssq = _dot((t * t).astype(jnp.bfloat16), gsum_ref[...])
    return t * lax.rsqrt(ssq * (1.0 / HEAD_DIM) + EPS) * gain_ref[...]


def _alibi_slopes(n):
    return [2.0 ** (-8.0 * (i + 1) / n) for i in range(n)]


def _alibi_lanes(pos, lane, first):
    rel = lane - first
    hi = ((pos >> 8) << 8).astype(jnp.float32)
    lo = (pos & 255).astype(jnp.float32)
    pos_sel = jnp.where((rel & 1) == 0, hi, lo)
    l_sel = jnp.where((rel & 3) < 2, LOG2E_HI, LOG2E_LO)
    first4 = (rel >= 0) & (rel < 4)
    last4 = (rel >= 4) & (rel < 8)
    return (jnp.where(last4, pos_sel, 0.0), jnp.where(first4, l_sel, 0.0),
            jnp.where(first4, pos_sel, 0.0), jnp.where(last4, l_sel, 0.0))


def _shift_rows(ext, d):
    return pltpu.roll(ext, d, axis=0)


def _with_ones(v):
    return jnp.concatenate([v, jnp.ones_like(v)], axis=1)


def _score_bound(q_gain, k_gain):
    return (HEAD_DIM ** 0.5) * jnp.max(jnp.abs(q_gain)) * jnp.max(jnp.abs(k_gain))


def _even_in_kernel(x_ref, g_ref, w_ref, gsum_ref, qg_ref, kg_ref, pw_ref, ps_ref,
                    q1_ref, q2_ref, k1_ref, k2_ref, v_ref, yb_ref, carry_ref,
                    *, tm, blocks_per_seq):
    blk = pl.program_id(0) % blocks_per_seq

    @pl.when(blk == 0)
    def _():
        carry_ref[...] = jnp.zeros_like(carry_ref)

    x = x_ref[...]
    h = (x * _rms_scale(x) * g_ref[...]).astype(jnp.bfloat16)
    proj = _dot(h, w_ref[...])
    qn = _group_rmsnorm(proj[:, 0:512], gsum_ref, qg_ref)
    kn = _group_rmsnorm(proj[:, 512:1024], gsum_ref, kg_ref)
    v_ref[...] = proj[:, 1024:1536].astype(jnp.bfloat16)

    pos = lax.broadcasted_iota(jnp.int32, (tm, 1), 0) + blk * tm
    lane = lax.broadcasted_iota(jnp.int32, (1, LANES), 1)
    low = lane < HEAD_DIM
    for mp, (q_out, k_out) in enumerate(((q1_ref, k1_ref), (q2_ref, k2_ref))):
        q_pos, q_const, k_pos, k_const = _alibi_lanes(pos, lane, HEAD_DIM if mp == 0 else 0)
        data = low if mp == 0 else jnp.logical_not(low)
        for hd, slope in enumerate(_alibi_slopes(A_HEADS)):
            cols = slice(hd * LANES, (hd + 1) * LANES)
            q_out[:, cols] = jnp.where(data, qn[:, cols], q_const - slope * q_pos).astype(jnp.bfloat16)
            k_out[:, cols] = jnp.where(data, kn[:, cols], k_const + slope * k_pos).astype(jnp.bfloat16)

    u = proj[:, 1536:2048]
    ext = jnp.concatenate([carry_ref[...], u], axis=0)
    pos1 = (lax.broadcasted_iota(jnp.int32, (tm, 1), 0) + (blk * tm + 1)).astype(jnp.float32)
    for g, w in enumerate(B_WINDOWS):
        cols = slice(g * LANES, (g + 1) * LANES)
        s = ext[:, cols]
        d = 1
        while d < w:
            s = s + _shift_rows(s, d)
            d *= 2
        pooled = s[POOL_HALO:, :] / jnp.minimum(pos1, float(w)) - u[:, cols]
        yb = _dot(pooled.astype(jnp.bfloat16), pw_ref[g]) * ps_ref[:, cols]
        yb_ref[:, cols] = yb.astype(jnp.bfloat16)
    carry_ref[...] = u[tm - POOL_HALO:, :]


def _even_in(x2d, g, w, gsum, qg, kg, pw, ps, *, seq):
    t = x2d.shape[0]
    tm = TM_PROJ
    const = lambda i: (0, 0)
    row = lambda i: (i, 0)
    out = jax.ShapeDtypeStruct((t, 512), jnp.bfloat16)
    return pl.pallas_call(
        functools.partial(_even_in_kernel, tm=tm, blocks_per_seq=seq // tm),
        grid=(t // tm,),
        in_specs=[
            pl.BlockSpec((tm, D_MODEL), row),
            pl.BlockSpec((1, D_MODEL), const),
            pl.BlockSpec((D_MODEL, 2048), const),
            pl.BlockSpec((512, 512), const),
            pl.BlockSpec((1, 512), const),
            pl.BlockSpec((1, 512), const),
            pl.BlockSpec((4, LANES, LANES), lambda i: (0, 0, 0)),
            pl.BlockSpec((1, 512), const),
        ],
        out_specs=[pl.BlockSpec((tm, 512), row)] * 6,
        out_shape=[out] * 6,
        scratch_shapes=[pltpu.VMEM((POOL_HALO, 512), jnp.float32)],
        compiler_params=pltpu.CompilerParams(
            dimension_semantics=("arbitrary",), vmem_limit_bytes=VMEM_LIMIT),
        name="even_in_proj",
    )(x2d, g, w, gsum, qg, kg, pw, ps)


def _diff_attn_kernel(q1_ref, q2_ref, k1_ref, k2_ref, v_ref, lam_ref, sg_ref, o_ref,
                      *, tq, online):
    qi = pl.program_id(2)
    qs = (q1_ref[...], q2_ref[...])
    k_refs = (k1_ref, k2_ref)

    def key_rows(j):
        return pl.ds(pl.multiple_of(j * tq, tq), tq)

    def scores(j, diagonal):
        out = []
        for q, k_ref in zip(qs, k_refs):
            s = _dot_nt(q, k_ref[key_rows(j), :])
            if diagonal:
                ii = lax.broadcasted_iota(jnp.int32, (tq, tq), 0)
                jj = lax.broadcasted_iota(jnp.int32, (tq, tq), 1)
                s = jnp.where(jj > ii, NEG, s)
            out.append(s)
        return out

    def pv(j, ps):
        v1 = _with_ones(v_ref[key_rows(j), :])
        return [_dot(p, v1) for p in ps]

    if online:
        def tile(j, state):
            out = []
            for mp, s in enumerate(scores(j, state is None)):
                m_new = jnp.max(s, axis=-1, keepdims=True)
                if state is not None:
                    m_new = jnp.maximum(state[2 * mp], m_new)
                out += [m_new, jnp.exp2(s - m_new).astype(jnp.bfloat16)]
            accs = pv(j, out[1::2])
            if state is not None:
                accs = [jnp.exp2(state[2 * mp] - out[2 * mp]) * state[2 * mp + 1] + accs[mp]
                        for mp in range(2)]
            return (out[0], accs[0], out[2], accs[1])

        _, acc1, _, acc2 = lax.fori_loop(0, qi, tile, tile(qi, None))
    else:
        def tile(j, state):
            accs = pv(j, [jnp.exp2(s).astype(jnp.bfloat16) for s in scores(j, state is None)])
            return tuple(accs) if state is None else (state[0] + accs[0], state[1] + accs[1])

        def unrolled(i, state):
            for t in range(KV_UNROLL):
                state = tile(KV_UNROLL * i + t, state)
            return state

        n_main = qi // KV_UNROLL
        state = lax.fori_loop(0, n_main, unrolled, tile(qi, None))
        acc1, acc2 = lax.fori_loop(KV_UNROLL * n_main, qi, tile, state)

    lv = lam_ref[...]
    lam = (jnp.exp(jnp.sum(lv[0:1] * lv[1:2], axis=-1, keepdims=True))
           - jnp.exp(jnp.sum(lv[2:3] * lv[3:4], axis=-1, keepdims=True)) + LAMBDA_INIT_L0)
    o = acc1[:, :A_V] / acc1[:, A_V:] - lam * (acc2[:, :A_V] / acc2[:, A_V:])
    o = o * _rms_scale(o) * sg_ref[...] * (1.0 - LAMBDA_INIT_L0)
    o_ref[...] = o.astype(jnp.bfloat16)


def _diff_attn(q1, q2, k1, k2, v, lam_vecs, subln_gain, *, bsz, seq, online):
    tq = TQ
    nq = seq // tq
    q_spec = pl.BlockSpec((tq, LANES), lambda b, h, i: (b * nq + i, h))
    kv_spec = pl.BlockSpec((seq, LANES), lambda b, h, i: (b, h))
    return pl.pallas_call(
        functools.partial(_diff_attn_kernel, tq=tq, online=online),
        grid=(bsz, A_HEADS, nq),
        in_specs=[
            q_spec, q_spec, kv_spec, kv_spec, kv_spec,
            pl.BlockSpec((4, HEAD_DIM), lambda b, h, i: (0, 0)),
            pl.BlockSpec((1, A_V), lambda b, h, i: (0, 0)),
        ],
        out_specs=q_spec,
        out_shape=jax.ShapeDtypeStruct((bsz * seq, A_HEADS * A_V), jnp.bfloat16),
        compiler_params=pltpu.CompilerParams(
            dimension_semantics=("arbitrary",) * 3, vmem_limit_bytes=VMEM_LIMIT),
        name="diff_attention_online" if online else "diff_attention",
    )(q1, q2, k1, k2, v, lam_vecs, subln_gain)


def _post_kernel(x_ref, ya_ref, yb_ref, wo_ref, g_ref, wg_ref, wu_ref, wd_ref, o_ref):
    half = ya_ref.shape[1]
    x1 = x_ref[...] + _dot(ya_ref[...], wo_ref[0:half, :]) + _dot(yb_ref[...], wo_ref[half:, :])
    h = (x1 * _rms_scale(x1) * g_ref[...]).astype(jnp.bfloat16)
    gate = _dot(h, wg_ref[...])
    up = _dot(h, wu_ref[...])
    act = (gate * jax.nn.sigmoid(gate) * up).astype(jnp.bfloat16)
    o_ref[...] = x1 + _dot(act, wd_ref[...])


def _post(x2d, ya, yb, wo, g, wg, wu, wd, name):
    t = x2d.shape[0]
    tm = TM_PROJ
    row = lambda i: (i, 0)
    const = lambda i: (0, 0)
    resident = functools.partial(pl.BlockSpec, index_map=const, pipeline_mode=pl.Buffered(1))
    return pl.pallas_call(
        _post_kernel,
        grid=(t // tm,),
        in_specs=[
            pl.BlockSpec((tm, D_MODEL), row),
            pl.BlockSpec((tm, ya.shape[1]), row),
            pl.BlockSpec((tm, yb.shape[1]), row),
            resident((D_MODEL, D_MODEL)),
            pl.BlockSpec((1, D_MODEL), const),
            resident((D_MODEL, D_FF)),
            resident((D_MODEL, D_FF)),
            resident((D_FF, D_MODEL)),
        ],
        out_specs=pl.BlockSpec((tm, D_MODEL), row),
        out_shape=jax.ShapeDtypeStruct((t, D_MODEL), jnp.float32),
        compiler_params=pltpu.CompilerParams(
            dimension_semantics=("arbitrary",), vmem_limit_bytes=VMEM_LIMIT),
        name=name,
    )(x2d, ya, yb, wo, g, wg, wu, wd)


def _odd_in_kernel(x_ref, g_ref, w_ref, gsum_ref, qg_ref, kg_ref, cw_ref,
                   yc_ref, q_ref, k_ref, v_ref, carry_ref, *, tm, blocks_per_seq):
    blk = pl.program_id(0) % blocks_per_seq

    @pl.when(blk == 0)
    def _():
        carry_ref[...] = jnp.zeros_like(carry_ref)

    x = x_ref[...]
    h = (x * _rms_scale(x) * g_ref[...]).astype(jnp.bfloat16)
    proj = _dot(h, w_ref[...])
    bg = proj[:, 0:512]
    z = proj[:, 512:1024] * proj[:, 1024:1536]
    ext = jnp.concatenate([carry_ref[...], z], axis=0)
    z1 = _shift_rows(ext, 1)[CONV_HALO:, :]
    z2 = _shift_rows(ext, 2)[CONV_HALO:, :]
    cw = cw_ref[...]
    yc = bg * (cw[0:1] * z2 + cw[1:2] * z1 + cw[2:3] * z)
    yc_ref[...] = yc.astype(jnp.bfloat16)
    carry_ref[...] = z[tm - CONV_HALO:, :]

    q_ref[...] = _group_rmsnorm(proj[:, 1536:2048], gsum_ref, qg_ref).astype(jnp.bfloat16)
    k_ref[...] = _group_rmsnorm(proj[:, 2048:2560], gsum_ref, kg_ref).astype(jnp.bfloat16)
    v_ref[...] = proj[:, 2560:3072].astype(jnp.bfloat16)


def _odd_in(x2d, g, w, gsum, qg, kg, cw, *, seq):
    t = x2d.shape[0]
    tm = TM_PROJ
    const = lambda i: (0, 0)
    row = lambda i: (i, 0)
    out = jax.ShapeDtypeStruct((t, 512), jnp.bfloat16)
    return pl.pallas_call(
        functools.partial(_odd_in_kernel, tm=tm, blocks_per_seq=seq // tm),
        grid=(t // tm,),
        in_specs=[
            pl.BlockSpec((tm, D_MODEL), row),
            pl.BlockSpec((1, D_MODEL), const),
            pl.BlockSpec((D_MODEL, 3072), const),
            pl.BlockSpec((512, 512), const),
            pl.BlockSpec((1, 512), const),
            pl.BlockSpec((1, 512), const),
            pl.BlockSpec((3, 512), const),
        ],
        out_specs=[pl.BlockSpec((tm, 512), row)] * 4,
        out_shape=[out] * 4,
        scratch_shapes=[pltpu.VMEM((CONV_HALO, 512), jnp.float32)],
        compiler_params=pltpu.CompilerParams(
            dimension_semantics=("arbitrary",), vmem_limit_bytes=VMEM_LIMIT),
        name="odd_in_proj",
    )(x2d, g, w, gsum, qg, kg, cw)


def _pair_tile(q, kp, vp, bias):
    lane = lax.broadcasted_iota(jnp.int32, (1, LANES), 1)
    low = lane < HEAD_DIM
    zero = jnp.zeros_like(q)
    q2 = jnp.concatenate([jnp.where(low, q, zero), jnp.where(low, zero, q)], axis=0)
    p = jnp.exp2(_dot_nt(q2, kp) + bias).astype(jnp.bfloat16)
    r = _dot(p, _with_ones(vp))
    num = jnp.where(low, r[:SUB, :LANES], r[SUB:, :LANES])
    den = jnp.where(low, r[:SUB, LANES:], r[SUB:, LANES:])
    return num, den


def _dilated_fused_kernel(q_ref, k_ref, v_ref, perm_ref, tab_ref, o_ref,
                          kn, vn, k4, v4, k16, v16, qp, res4, res16):
    blk = pl.program_id(1)
    first = blk == 0
    cur = (blk & 1) * NG
    prev = NG - cur
    bf = jnp.bfloat16

    @pl.when(first)
    def _():
        kn[0:D_SPAN, :] = jnp.zeros((D_SPAN, D_WIDTH), bf)
        vn[0:D_SPAN, :] = jnp.zeros((D_SPAN, D_WIDTH), bf)
        k4[0:2] = jnp.zeros((2, GRP, D_WIDTH), bf)
        v4[0:2] = jnp.zeros((2, GRP, D_WIDTH), bf)
        k16[pl.ds(prev, NG)] = jnp.zeros((NG, GRP, D_WIDTH), bf)
        v16[pl.ds(prev, NG)] = jnp.zeros((NG, GRP, D_WIDTH), bf)

    @pl.when(jnp.logical_not(first))
    def _():
        kn[0:D_SPAN, :] = kn[DBLK:DBLK + D_SPAN, :]
        vn[0:D_SPAN, :] = vn[DBLK:DBLK + D_SPAN, :]
        k4[0:2] = k4[NG:NG + 2]
        v4[0:2] = v4[NG:NG + 2]

    kn[D_SPAN:, :] = k_ref[...]
    vn[D_SPAN:, :] = v_ref[...]
    for g in range(NG):
        rows = slice(g * GRP, (g + 1) * GRP)
        for src, d4, d16 in ((k_ref, k4, k16), (v_ref, v4, v16)):
            x = src[rows, :]
            d4[2 + g] = _dot(perm_ref[0], x).astype(bf)
            d16[cur + g] = _dot(perm_ref[2], x).astype(bf)

    kk = lax.broadcasted_iota(jnp.int32, (1, 2 * SUB), 1)
    no_halo = jnp.where(kk < D_SPAN, jnp.where(first, NEG, 0.0), 0.0)

    def pair_cols(p):
        return slice(p * LANES, (p + 1) * LANES)

    for g in range(NG):
        qp[g] = _dot(perm_ref[2], q_ref[g * GRP:(g + 1) * GRP, :]).astype(bf)

    def class16(c, carry):
        r = pl.ds(pl.multiple_of(c * 16, 16), 16)
        for p in range(D_HEADS // 2):
            cols = pair_cols(p)
            q = jnp.concatenate([qp[g, r, cols] for g in range(NG)], axis=0)
            kp = jnp.concatenate([k16[prev + g, r, cols] for g in range(NG)]
                                 + [k16[cur + g, r, cols] for g in range(NG)], axis=0)
            vp = jnp.concatenate([v16[prev + g, r, cols] for g in range(NG)]
                                 + [v16[cur + g, r, cols] for g in range(NG)], axis=0)
            num, den = _pair_tile(q, kp, vp, tab_ref[2, p] + no_halo)
            for g in range(NG):
                res16[0, g, r, cols] = num[16 * g:16 * g + 16].astype(bf)
                res16[1, g, r, cols] = den[16 * g:16 * g + 16].astype(bf)
        return carry

    lax.fori_loop(0, 16, class16, 0)

    for g in range(NG):
        qp[g] = _dot(perm_ref[0], q_ref[g * GRP:(g + 1) * GRP, :]).astype(bf)

    def class4(c, carry):
        r = pl.ds(pl.multiple_of(c * 64, 64), 64)
        for qs in range(NG // 2):
            for p in range(D_HEADS // 2):
                cols = pair_cols(p)
                q = jnp.concatenate([qp[2 * qs + i, r, cols] for i in range(2)], axis=0)
                kp = jnp.concatenate([k4[2 * qs + i, r, cols] for i in range(4)], axis=0)
                vp = jnp.concatenate([v4[2 * qs + i, r, cols] for i in range(4)], axis=0)
                bias = tab_ref[1, p] + no_halo if qs == 0 else tab_ref[1, p]
                num, den = _pair_tile(q, kp, vp, bias)
                for i in range(2):
                    res4[0, 2 * qs + i, r, cols] = num[64 * i:64 * i + 64].astype(bf)
                    res4[1, 2 * qs + i, r, cols] = den[64 * i:64 * i + 64].astype(bf)
        return carry

    lax.fori_loop(0, 4, class4, 0)

    def group(g, carry):
        num = _dot(perm_ref[1], res4[0, g]) + _dot(perm_ref[3], res16[0, g])
        den = _dot(perm_ref[1], res4[1, g]) + _dot(perm_ref[3], res16[1, g])
        pen = jnp.where(g == 0, no_halo, 0.0)
        for t in range(GRP // SUB):
            q_rows = pl.ds(pl.multiple_of(g * GRP + t * SUB, SUB), SUB)
            k_rows = pl.ds(pl.multiple_of(g * GRP + t * SUB, SUB), 2 * SUB)
            sub = slice(t * SUB, (t + 1) * SUB)
            for p in range(D_HEADS // 2):
                cols = pair_cols(p)
                bias = tab_ref[0, p] + pen if t == 0 else tab_ref[0, p]
                n1, d1 = _pair_tile(q_ref[q_rows, cols], kn[k_rows, cols], vn[k_rows, cols], bias)
                o_ref[q_rows, cols] = ((n1 + num[sub, cols]) / (d1 + den[sub, cols])).astype(bf)
        return carry

    lax.fori_loop(0, NG, group, 0)


def _class_permutations():
    mats = []
    out = np.arange(GRP)
    for d in (4, 16):
        per = GRP // d
        p = np.zeros((GRP, GRP), np.float32)
        p[out, d * (out % per) + out // per] = 1.0
        mats += [p, p.T]
    return jnp.asarray(np.stack(mats), jnp.bfloat16)


def _dilated_bias_table(dil):
    ii = lax.broadcasted_iota(jnp.int32, (SUB, SUB + D_SPAN), 0)
    kk = lax.broadcasted_iota(jnp.int32, (SUB, SUB + D_SPAN), 1)
    delta = D_SPAN + ii - kk
    slopes = jnp.asarray(_alibi_slopes(D_HEADS), jnp.float32)
    bias = (-LOG2E) * slopes[:, None, None] * (delta * dil).astype(jnp.float32)[None]
    return jnp.where(((delta >= 0) & (delta <= D_SPAN))[None], bias, NEG)


def _dilated_fused(q, k, v, *, bsz, seq):
    nblk = seq // DBLK
    tabs = jnp.stack([_dilated_bias_table(d).reshape(D_HEADS // 2, 2 * SUB, SUB + D_SPAN)
                      for d in D_DILATIONS])
    blk_spec = pl.BlockSpec((DBLK, D_WIDTH), lambda b, i: (b * nblk + i, 0))
    group_buf = lambda n: pltpu.VMEM((n, GRP, D_WIDTH), jnp.bfloat16)
    return pl.pallas_call(
        _dilated_fused_kernel,
        grid=(bsz, nblk),
        in_specs=[
            blk_spec, blk_spec, blk_spec,
            pl.BlockSpec((4, GRP, GRP), lambda b, i: (0, 0, 0), pipeline_mode=pl.Buffered(1)),
            pl.BlockSpec(tabs.shape, lambda b, i: (0, 0, 0, 0), pipeline_mode=pl.Buffered(1)),
        ],
        out_specs=blk_spec,
        out_shape=jax.ShapeDtypeStruct((bsz * seq, D_WIDTH), jnp.bfloat16),
        scratch_shapes=[
            pltpu.VMEM((D_SPAN + DBLK, D_WIDTH), jnp.bfloat16),
            pltpu.VMEM((D_SPAN + DBLK, D_WIDTH), jnp.bfloat16),
            group_buf(2 + NG), group_buf(2 + NG),
            group_buf(2 * NG), group_buf(2 * NG),
            group_buf(NG),
            pltpu.VMEM((2, NG, GRP, D_WIDTH), jnp.bfloat16),
            pltpu.VMEM((2, NG, GRP, D_WIDTH), jnp.bfloat16),
        ],
        compiler_params=pltpu.CompilerParams(
            dimension_semantics=("arbitrary",) * 2, vmem_limit_bytes=VMEM_LIMIT),
        name="dilated_attention_fused",
    )(q, k, v, _class_permutations(), tabs)


def _dilated_kernel(q_ref, kh_ref, kc_ref, vh_ref, vc_ref, bias_ref, o_ref, lse_ref,
                    kbuf, vbuf, *, tl):
    j = pl.program_id(2)
    kbuf[0:D_SPAN, :] = kh_ref[...]
    kbuf[D_SPAN:, :] = kc_ref[...]
    vbuf[0:D_SPAN, :] = vh_ref[...]
    vbuf[D_SPAN:, :] = vc_ref[...]
    lane = lax.broadcasted_iota(jnp.int32, (1, LANES), 1)
    kk = lax.broadcasted_iota(jnp.int32, (1, SUB + D_SPAN), 1)
    halo_pen = jnp.where(kk < D_SPAN, jnp.where(j == 0, NEG, 0.0), 0.0)
    for qs in range(tl // SUB):
        rows = slice(qs * SUB, (qs + 1) * SUB)
        krows = slice(qs * SUB, (qs + 1) * SUB + D_SPAN)
        lse_tile = jnp.zeros((SUB, LANES), jnp.float32)
        for p in range(D_HEADS // 2):
            cols = slice(p * LANES, (p + 1) * LANES)
            q = q_ref[rows, cols]
            kp = kbuf[krows, cols]
            vp = vbuf[krows, cols]
            zero = jnp.zeros_like(q)
            halves = []
            for e in range(2):
                hd = 2 * p + e
                qm = jnp.where((lane < HEAD_DIM) if e == 0 else (lane >= HEAD_DIM), q, zero)
                s = _dot_nt(qm, kp) + bias_ref[hd]
                if qs == 0:
                    s = s + halo_pen
                m = jnp.max(s, axis=-1, keepdims=True)
                pe = jnp.exp2(s - m)
                l = jnp.sum(pe, axis=-1, keepdims=True)
                halves.append(_dot(pe.astype(jnp.bfloat16), vp) / l)
                lse_tile = jnp.where(lane == hd, m + jnp.log2(l), lse_tile)
            o_ref[rows, cols] = jnp.where(lane < HEAD_DIM, halves[0], halves[1]).astype(jnp.bfloat16)
        lse_ref[rows, :] = lse_tile


def _dilated_branch(q, k, v, *, bsz, seq, dil):
    tl = TL
    length = seq // dil
    nb = length // tl
    halo_per_blk = tl // D_SPAN
    view = lambda a: a.reshape(bsz, length, dil * D_WIDTH)
    cur = lambda b, r, j: (b, j, r)
    halo = lambda b, r, j: (b, jnp.maximum(j * halo_per_blk - 1, 0), r)
    o, lse = pl.pallas_call(
        functools.partial(_dilated_kernel, tl=tl),
        grid=(bsz, dil, nb),
        in_specs=[
            pl.BlockSpec((None, tl, D_WIDTH), cur),
            pl.BlockSpec((None, D_SPAN, D_WIDTH), halo),
            pl.BlockSpec((None, tl, D_WIDTH), cur),
            pl.BlockSpec((None, D_SPAN, D_WIDTH), halo),
            pl.BlockSpec((None, tl, D_WIDTH), cur),
            pl.BlockSpec((D_HEADS, SUB, SUB + D_SPAN), lambda b, r, j: (0, 0, 0)),
        ],
        out_specs=[
            pl.BlockSpec((None, tl, D_WIDTH), cur),
            pl.BlockSpec((None, tl, LANES), cur),
        ],
        out_shape=[
            jax.ShapeDtypeStruct((bsz, length, dil * D_WIDTH), jnp.bfloat16),
            jax.ShapeDtypeStruct((bsz, length, dil * LANES), jnp.float32),
        ],
        scratch_shapes=[pltpu.VMEM((tl + D_SPAN, D_WIDTH), jnp.bfloat16)] * 2,
        compiler_params=pltpu.CompilerParams(
            dimension_semantics=("arbitrary",) * 3, vmem_limit_bytes=VMEM_LIMIT),
        name=f"dilated_attention_d{dil}",
    )(view(q), view(k), view(k), view(v), view(v), _dilated_bias_table(dil))
    return o.reshape(bsz * seq, D_WIDTH), lse.reshape(bsz * seq, LANES)


def _branch_combine_kernel(o1_ref, o2_ref, o3_ref, l1_ref, l2_ref, l3_ref, y_ref):
    lses = [l1_ref[...], l2_ref[...], l3_ref[...]]
    mx = jnp.maximum(jnp.maximum(lses[0], lses[1]), lses[2])
    es = [jnp.exp2(l - mx) for l in lses]
    inv = 1.0 / (es[0] + es[1] + es[2])
    ws = [e * inv for e in es]
    lane = lax.broadcasted_iota(jnp.int32, (1, LANES), 1)
    for p in range(D_HEADS // 2):
        cols = slice(p * LANES, (p + 1) * LANES)
        yd = None
        for w, o in zip(ws, (o1_ref, o2_ref, o3_ref)):
            wf = jnp.where(lane < HEAD_DIM, w[:, 2 * p:2 * p + 1], w[:, 2 * p + 1:2 * p + 2])
            term = wf * o[:, cols].astype(jnp.float32)
            yd = term if yd is None else yd + term
        y_ref[:, cols] = yd.astype(jnp.bfloat16)


def _dilated_by_branch(q, k, v, *, bsz, seq):
    outs, lses = zip(*[_dilated_branch(q, k, v, bsz=bsz, seq=seq, dil=d) for d in D_DILATIONS])
    tm = TM_PROJ
    row = lambda i: (i, 0)
    return pl.pallas_call(
        _branch_combine_kernel,
        grid=(bsz * seq // tm,),
        in_specs=[pl.BlockSpec((tm, D_WIDTH), row)] * 3 + [pl.BlockSpec((tm, LANES), row)] * 3,
        out_specs=pl.BlockSpec((tm, D_WIDTH), row),
        out_shape=jax.ShapeDtypeStruct((bsz * seq, D_WIDTH), jnp.bfloat16),
        compiler_params=pltpu.CompilerParams(
            dimension_semantics=("arbitrary",), vmem_limit_bytes=VMEM_LIMIT),
        name="dilated_branch_combine",
    )(*outs, *lses)


def kernel(x, norm_mix, norm_ffn, ev_w_in, ev_w_out, ev_q_gain, ev_k_gain, ev_lambda_q1, ev_lambda_k1, ev_lambda_q2, ev_lambda_k2, ev_subln_gain, ev_pool_w, ev_pool_scale, od_w_in, od_w_out, od_conv_w, od_q_gain, od_k_gain, ffn_w_gate, ffn_w_up, ffn_w_down):
    bsz, seq, _ = x.shape
    bf = jnp.bfloat16
    f32 = jnp.float32
    q_scale = HEAD_DIM ** -0.5 * LOG2E
    x2d = x.reshape(bsz * seq, D_MODEL)
    grp = lax.broadcasted_iota(jnp.int32, (512, 512), 0) // HEAD_DIM
    gsum = (grp == grp.T).astype(bf)
    tile8 = lambda gain, s: (jnp.tile(gain.astype(f32), 8) * s).reshape(1, 512)

    w = ev_w_in[0]
    regroup = lambda c: c.reshape(D_MODEL, 2, A_HEADS, HEAD_DIM).transpose(0, 2, 1, 3).reshape(D_MODEL, 512)
    w_even = jnp.concatenate([regroup(w[:, 0:512]), regroup(w[:, 512:1024]), w[:, 1024:]], axis=1).astype(bf)
    q1, q2, k1, k2, v, yb = _even_in(
        x2d, norm_mix[0].reshape(1, D_MODEL).astype(f32), w_even, gsum,
        tile8(ev_q_gain[0], q_scale), tile8(ev_k_gain[0], 1.0),
        ev_pool_w[0].astype(bf), ev_pool_scale[0].reshape(1, 512).astype(f32), seq=seq)
    lam_vecs = jnp.stack([ev_lambda_q1[0], ev_lambda_k1[0], ev_lambda_q2[0], ev_lambda_k2[0]]).astype(f32)
    attn = functools.partial(_diff_attn, q1, q2, k1, k2, v, lam_vecs,
                             ev_subln_gain[0].reshape(1, A_V).astype(f32), bsz=bsz, seq=seq)
    ya = lax.cond(_score_bound(ev_q_gain[0], ev_k_gain[0]) <= MAX_UNSHIFTED_SCORE,
                  lambda: attn(online=False), lambda: attn(online=True))
    x2d = _post(x2d, ya, yb, ev_w_out[0].astype(bf),
                norm_ffn[0].reshape(1, D_MODEL).astype(f32), ffn_w_gate[0].astype(bf),
                ffn_w_up[0].astype(bf), ffn_w_down[0].astype(bf), "even_post")

    yc, q, k, v = _odd_in(
        x2d, norm_mix[1].reshape(1, D_MODEL).astype(f32), od_w_in[0].astype(bf), gsum,
        tile8(od_q_gain[0], q_scale), tile8(od_k_gain[0], 1.0), od_conv_w[0].astype(f32), seq=seq)
    yd = lax.cond(_score_bound(od_q_gain[0], od_k_gain[0]) <= MAX_UNSHIFTED_SCORE,
                  lambda: _dilated_fused(q, k, v, bsz=bsz, seq=seq),
                  lambda: _dilated_by_branch(q, k, v, bsz=bsz, seq=seq))
    x2d = _post(x2d, yc, yd, od_w_out[0].astype(bf),
                norm_ffn[1].reshape(1, D_MODEL).astype(f32), ffn_w_gate[1].astype(bf),
                ffn_w_up[1].astype(bf), ffn_w_down[1].astype(bf), "odd_post")
    return x2d.reshape(bsz, seq, D_MODEL)
```

```python
import functools
import math

import numpy as np
import jax
import jax.numpy as jnp
from jax import lax
from jax.experimental import pallas as pl
from jax.experimental.pallas import tpu as pltpu

D_MODEL = 1024
HEAD_DIM = 64
EPS = 1e-6
A_HEADS = 4
A_V = 128
B_WINDOWS = (2, 4, 8, 16)
POOL_HALO = 16
CONV_HALO = 8
D_HEADS = 8
D_WIDTH = D_HEADS * HEAD_DIM
D_DILATIONS = (1, 4, 16)
D_SPAN = 128
D_FF = 2816
LAMBDA_INIT_L0 = 0.8 - 0.6 * math.exp(-0.3 * 0)
NEG = -1e30
LOG2E = math.log2(math.e)
LOG2E_HI = float(np.float32(LOG2E).astype(jnp.bfloat16))
LOG2E_LO = float(np.float32(LOG2E - LOG2E_HI).astype(jnp.bfloat16))
MAX_UNSHIFTED_SCORE = 60.0

LANES = 128
MXU_DEPTH = 256
VMEM_LIMIT = 56 * 1024 * 1024

TM_PROJ = 512
TM_IN = 1024
SM_IN = 512
TQ = 1024
TL = 512
SUB = 128
DBLK = 2048
GRP = 256
NG = DBLK // GRP

_NT = (((1,), (1,)), ((), ()))


def _dot(a, b):
    return jnp.dot(a, b, preferred_element_type=jnp.float32)


def _dot_nt(a, b):
    return lax.dot_general(a, b, _NT, preferred_element_type=jnp.float32)


def _rms_scale(x):
    return lax.rsqrt(jnp.mean(x * x, axis=-1, keepdims=True) + EPS)


def _group_rmsnorm(t, gsum_ref, gain_ref):
    sq = (t * t).astype(jnp.bfloat16)
    half = gsum_ref.shape[0]
    ssq = jnp.concatenate([_dot(sq[:, :half], gsum_ref[...]), _dot(sq[:, half:], gsum_ref[...])],
                          axis=1)
    return t * lax.rsqrt(ssq * (1.0 / HEAD_DIM) + EPS) * gain_ref[...]


def _alibi_slopes(n):
    return [2.0 ** (-8.0 * (i + 1) / n) for i in range(n)]


def _alibi_lanes(pos, lane, first):
    rel = lane - first
    hi = ((pos >> 8) << 8).astype(jnp.float32)
    lo = (pos & 255).astype(jnp.float32)
    pos_sel = jnp.where((rel & 1) == 0, hi, lo)
    l_sel = jnp.where((rel & 3) < 2, LOG2E_HI, LOG2E_LO)
    first4 = (rel >= 0) & (rel < 4)
    last4 = (rel >= 4) & (rel < 8)
    return (jnp.where(last4, pos_sel, 0.0), jnp.where(first4, l_sel, 0.0),
            jnp.where(first4, pos_sel, 0.0), jnp.where(last4, l_sel, 0.0))


def _shift_rows(ext, d):
    return pltpu.roll(ext, d, axis=0)


def _with_ones(v):
    return jnp.concatenate([v, jnp.ones_like(v)], axis=1)


def _score_bound(q_gain, k_gain):
    return (HEAD_DIM ** 0.5) * jnp.max(jnp.abs(q_gain)) * jnp.max(jnp.abs(k_gain))


def _even_in_kernel(x_ref, g_ref, w_ref, gsum_ref, qg_ref, kg_ref, pw_ref, ps_ref,
                    q1_ref, q2_ref, k1_ref, k2_ref, v_ref, yb_ref, carry_ref,
                    *, tm, sm, blocks_per_seq):
    blk = pl.program_id(0) % blocks_per_seq

    @pl.when(blk == 0)
    def _():
        carry_ref[...] = jnp.zeros_like(carry_ref)

    for sb in range(tm // sm):
        rows = slice(sb * sm, (sb + 1) * sm)
        x = x_ref[rows, :]
        h = (x * _rms_scale(x) * g_ref[...]).astype(jnp.bfloat16)
        proj = _dot(h, w_ref[...])
        qn = _group_rmsnorm(proj[:, 0:512], gsum_ref, qg_ref)
        kn = _group_rmsnorm(proj[:, 512:1024], gsum_ref, kg_ref)
        v_ref[rows, :] = proj[:, 1024:1536].astype(jnp.bfloat16)

        pos = lax.broadcasted_iota(jnp.int32, (sm, 1), 0) + (blk * tm + sb * sm)
        lane = lax.broadcasted_iota(jnp.int32, (1, LANES), 1)
        low = lane < HEAD_DIM
        for mp, (q_out, k_out) in enumerate(((q1_ref, k1_ref), (q2_ref, k2_ref))):
            q_pos, q_const, k_pos, k_const = _alibi_lanes(pos, lane, HEAD_DIM if mp == 0 else 0)
            data = low if mp == 0 else jnp.logical_not(low)
            for hd, slope in enumerate(_alibi_slopes(A_HEADS)):
                cols = slice(hd * LANES, (hd + 1) * LANES)
                q_out[rows, cols] = jnp.where(data, qn[:, cols], q_const - slope * q_pos).astype(jnp.bfloat16)
                k_out[rows, cols] = jnp.where(data, kn[:, cols], k_const + slope * k_pos).astype(jnp.bfloat16)

        u = proj[:, 1536:2048]
        ext = jnp.concatenate([carry_ref[...], u], axis=0)
        pos1 = (pos + 1).astype(jnp.float32)
        pooled = []
        for g, w in enumerate(B_WINDOWS):
            cols = slice(g * LANES, (g + 1) * LANES)
            s = ext[:, cols]
            d = 1
            while d < w:
                s = s + _shift_rows(s, d)
                d *= 2
            pooled.append((s[POOL_HALO:, :] / jnp.minimum(pos1, float(w)) - u[:, cols]).astype(jnp.bfloat16))
        for pg in range(len(B_WINDOWS) // 2):
            cols = slice(2 * pg * LANES, (2 * pg + 2) * LANES)
            yb = _dot(jnp.concatenate(pooled[2 * pg:2 * pg + 2], axis=1), pw_ref[pg]) * ps_ref[:, cols]
            yb_ref[rows, cols] = yb.astype(jnp.bfloat16)
        carry_ref[...] = u[sm - POOL_HALO:, :]


def _even_in(x2d, g, w, gsum, qg, kg, pw, ps, *, seq):
    t = x2d.shape[0]
    tm = TM_IN
    const = lambda i: (0, 0)
    row = lambda i: (i, 0)
    out = jax.ShapeDtypeStruct((t, 512), jnp.bfloat16)
    return pl.pallas_call(
        functools.partial(_even_in_kernel, tm=tm, sm=SM_IN, blocks_per_seq=seq // tm),
        grid=(t // tm,),
        in_specs=[
            pl.BlockSpec((tm, D_MODEL), row),
            pl.BlockSpec((1, D_MODEL), const),
            pl.BlockSpec((D_MODEL, 2048), const),
            pl.BlockSpec((MXU_DEPTH, MXU_DEPTH), const),
            pl.BlockSpec((1, 512), const),
            pl.BlockSpec((1, 512), const),
            pl.BlockSpec((2, MXU_DEPTH, MXU_DEPTH), lambda i: (0, 0, 0)),
            pl.BlockSpec((1, 512), const),
        ],
        out_specs=[pl.BlockSpec((tm, 512), row)] * 6,
        out_shape=[out] * 6,
        scratch_shapes=[pltpu.VMEM((POOL_HALO, 512), jnp.float32)],
        compiler_params=pltpu.CompilerParams(
            dimension_semantics=("arbitrary",), vmem_limit_bytes=VMEM_LIMIT),
        name="even_in_proj",
    )(x2d, g, w, gsum, qg, kg, pw, ps)


def _diff_attn_kernel(q1_ref, q2_ref, k1_ref, k2_ref, v_ref, lam_ref, sg_ref, o_ref,
                      *, tq, online):
    tk = tq // 2
    qi = pl.program_id(2)
    qs = (q1_ref[...], q2_ref[...])
    k_refs = (k1_ref, k2_ref)

    def key_rows(j):
        return pl.ds(pl.multiple_of(j * tk, tk), tk)

    def scores(j, q_list, first_key=None):
        out = []
        for q, k_ref in zip(q_list, k_refs):
            s = _dot_nt(q, k_ref[key_rows(j), :])
            if first_key is not None:
                ii = lax.broadcasted_iota(jnp.int32, s.shape, 0)
                jj = lax.broadcasted_iota(jnp.int32, s.shape, 1)
                s = jnp.where(jj + first_key > ii, NEG, s)
            out.append(s)
        return out

    def pv(j, ps):
        v1 = _with_ones(v_ref[key_rows(j), :])
        return [_dot(p, v1) for p in ps]

    if online:
        def tile(j, state, first_key=None):
            out = []
            for mp, s in enumerate(scores(j, qs, first_key)):
                m_new = jnp.max(s, axis=-1, keepdims=True)
                if state is not None:
                    m_new = jnp.maximum(state[2 * mp], m_new)
                out += [m_new, jnp.exp2(s - m_new).astype(jnp.bfloat16)]
            accs = pv(j, out[1::2])
            if state is not None:
                accs = [jnp.exp2(state[2 * mp] - out[2 * mp]) * state[2 * mp + 1] + accs[mp]
                        for mp in range(2)]
            return (out[0], accs[0], out[2], accs[1])

        state = tile(2 * qi + 1, tile(2 * qi, None, 0), tk)
        _, acc1, _, acc2 = lax.fori_loop(0, 2 * qi, tile, state)
    else:
        def tile(j, q_list, first_key=None):
            return pv(j, [jnp.exp2(s).astype(jnp.bfloat16) for s in scores(j, q_list, first_key)])

        upper = tile(2 * qi, qs, 0)
        lower = tile(2 * qi + 1, [q[tk:] for q in qs], 0)
        state = tuple(jnp.concatenate([a[:tk], a[tk:] + b], axis=0) for a, b in zip(upper, lower))

        def two_tiles(i, state):
            for t in range(2):
                state = tuple(a + b for a, b in zip(state, tile(2 * i + t, qs)))
            return state

        acc1, acc2 = lax.fori_loop(0, qi, two_tiles, state)

    lv = lam_ref[...]
    lam = (jnp.exp(jnp.sum(lv[0:1] * lv[1:2], axis=-1, keepdims=True))
           - jnp.exp(jnp.sum(lv[2:3] * lv[3:4], axis=-1, keepdims=True)) + LAMBDA_INIT_L0)
    o = acc1[:, :A_V] / acc1[:, A_V:] - lam * (acc2[:, :A_V] / acc2[:, A_V:])
    o = o * _rms_scale(o) * sg_ref[...] * (1.0 - LAMBDA_INIT_L0)
    o_ref[...] = o.astype(jnp.bfloat16)


def _diff_attn(q1, q2, k1, k2, v, lam_vecs, subln_gain, *, bsz, seq, online):
    tq = TQ
    nq = seq // tq
    q_spec = pl.BlockSpec((tq, LANES), lambda b, h, i: (b * nq + i, h))
    kv_spec = pl.BlockSpec((seq, LANES), lambda b, h, i: (b, h))
    return pl.pallas_call(
        functools.partial(_diff_attn_kernel, tq=tq, online=online),
        grid=(bsz, A_HEADS, nq),
        in_specs=[
            q_spec, q_spec, kv_spec, kv_spec, kv_spec,
            pl.BlockSpec((4, HEAD_DIM), lambda b, h, i: (0, 0)),
            pl.BlockSpec((1, A_V), lambda b, h, i: (0, 0)),
        ],
        out_specs=q_spec,
        out_shape=jax.ShapeDtypeStruct((bsz * seq, A_HEADS * A_V), jnp.bfloat16),
        compiler_params=pltpu.CompilerParams(
            dimension_semantics=("arbitrary",) * 3, vmem_limit_bytes=VMEM_LIMIT),
        name="diff_attention_online" if online else "diff_attention",
    )(q1, q2, k1, k2, v, lam_vecs, subln_gain)


def _post_kernel(x_ref, ya_ref, yb_ref, wo_ref, g_ref, wg_ref, wu_ref, wd_ref, o_ref):
    half = ya_ref.shape[1]
    x1 = x_ref[...] + _dot(ya_ref[...], wo_ref[0:half, :]) + _dot(yb_ref[...], wo_ref[half:, :])
    h = (x1 * _rms_scale(x1) * g_ref[...]).astype(jnp.bfloat16)
    gate = _dot(h, wg_ref[...])
    up = _dot(h, wu_ref[...])
    act = (gate * jax.nn.sigmoid(gate) * up).astype(jnp.bfloat16)
    o_ref[...] = x1 + _dot(act, wd_ref[...])


def _post(x2d, ya, yb, wo, g, wg, wu, wd, name):
    t = x2d.shape[0]
    tm = TM_PROJ
    row = lambda i: (i, 0)
    const = lambda i: (0, 0)
    resident = functools.partial(pl.BlockSpec, index_map=const, pipeline_mode=pl.Buffered(1))
    return pl.pallas_call(
        _post_kernel,
        grid=(t // tm,),
        in_specs=[
            pl.BlockSpec((tm, D_MODEL), row),
            pl.BlockSpec((tm, ya.shape[1]), row),
            pl.BlockSpec((tm, yb.shape[1]), row),
            resident((D_MODEL, D_MODEL)),
            pl.BlockSpec((1, D_MODEL), const),
            resident((D_MODEL, D_FF)),
            resident((D_MODEL, D_FF)),
            resident((D_FF, D_MODEL)),
        ],
        out_specs=pl.BlockSpec((tm, D_MODEL), row),
        out_shape=jax.ShapeDtypeStruct((t, D_MODEL), jnp.float32),
        compiler_params=pltpu.CompilerParams(
            dimension_semantics=("arbitrary",), vmem_limit_bytes=VMEM_LIMIT),
        name=name,
    )(x2d, ya, yb, wo, g, wg, wu, wd)


def _odd_in_kernel(x_ref, g_ref, w_ref, gsum_ref, qg_ref, kg_ref, cw_ref,
                   yc_ref, q_ref, k_ref, v_ref, carry_ref, *, tm, sm, blocks_per_seq):
    blk = pl.program_id(0) % blocks_per_seq

    @pl.when(blk == 0)
    def _():
        carry_ref[...] = jnp.zeros_like(carry_ref)

    for sb in range(tm // sm):
        rows = slice(sb * sm, (sb + 1) * sm)
        x = x_ref[rows, :]
        h = (x * _rms_scale(x) * g_ref[...]).astype(jnp.bfloat16)
        proj = _dot(h, w_ref[...])
        bg = proj[:, 0:512]
        z = proj[:, 512:1024] * proj[:, 1024:1536]
        ext = jnp.concatenate([carry_ref[...], z], axis=0)
        z1 = _shift_rows(ext, 1)[CONV_HALO:, :]
        z2 = _shift_rows(ext, 2)[CONV_HALO:, :]
        cw = cw_ref[...]
        yc = bg * (cw[0:1] * z2 + cw[1:2] * z1 + cw[2:3] * z)
        yc_ref[rows, :] = yc.astype(jnp.bfloat16)
        carry_ref[...] = z[sm - CONV_HALO:, :]

        q_ref[rows, :] = _group_rmsnorm(proj[:, 1536:2048], gsum_ref, qg_ref).astype(jnp.bfloat16)
        k_ref[rows, :] = _group_rmsnorm(proj[:, 2048:2560], gsum_ref, kg_ref).astype(jnp.bfloat16)
        v_ref[rows, :] = proj[:, 2560:3072].astype(jnp.bfloat16)


def _odd_in(x2d, g, w, gsum, qg, kg, cw, *, seq):
    t = x2d.shape[0]
    tm = TM_IN
    const = lambda i: (0, 0)
    row = lambda i: (i, 0)
    out = jax.ShapeDtypeStruct((t, 512), jnp.bfloat16)
    return pl.pallas_call(
        functools.partial(_odd_in_kernel, tm=tm, sm=SM_IN, blocks_per_seq=seq // tm),
        grid=(t // tm,),
        in_specs=[
            pl.BlockSpec((tm, D_MODEL), row),
            pl.BlockSpec((1, D_MODEL), const),
            pl.BlockSpec((D_MODEL, 3072), const),
            pl.BlockSpec((MXU_DEPTH, MXU_DEPTH), const),
            pl.BlockSpec((1, 512), const),
            pl.BlockSpec((1, 512), const),
            pl.BlockSpec((3, 512), const),
        ],
        out_specs=[pl.BlockSpec((tm, 512), row)] * 4,
        out_shape=[out] * 4,
        scratch_shapes=[pltpu.VMEM((CONV_HALO, 512), jnp.float32)],
        compiler_params=pltpu.CompilerParams(
            dimension_semantics=("arbitrary",), vmem_limit_bytes=VMEM_LIMIT),
        name="odd_in_proj",
    )(x2d, g, w, gsum, qg, kg, cw)


def _pair_tile(q, kp, vp, bias):
    lane = lax.broadcasted_iota(jnp.int32, (1, LANES), 1)
    low = lane < HEAD_DIM
    zero = jnp.zeros_like(q)
    q2 = jnp.concatenate([jnp.where(low, q, zero), jnp.where(low, zero, q)], axis=0)
    p = jnp.exp2(_dot_nt(q2, kp) + bias).astype(jnp.bfloat16)
    r = _dot(p, _with_ones(vp))
    num = jnp.where(low, r[:SUB, :LANES], r[SUB:, :LANES])
    den = jnp.where(low, r[:SUB, LANES:], r[SUB:, LANES:])
    return num, den


def _dilated_fused_kernel(q_ref, k_ref, v_ref, perm_ref, tab_ref, o_ref,
                          kn, vn, k4, v4, k16, v16, qp, res4, res16):
    blk = pl.program_id(1)
    first = blk == 0
    cur = (blk & 1) * NG
    prev = NG - cur
    bf = jnp.bfloat16

    @pl.when(first)
    def _():
        kn[0:D_SPAN, :] = jnp.zeros((D_SPAN, D_WIDTH), bf)
        vn[0:D_SPAN, :] = jnp.zeros((D_SPAN, D_WIDTH), bf)
        k4[0:2] = jnp.zeros((2, GRP, D_WIDTH), bf)
        v4[0:2] = jnp.zeros((2, GRP, D_WIDTH), bf)
        k16[pl.ds(prev, NG)] = jnp.zeros((NG, GRP, D_WIDTH), bf)
        v16[pl.ds(prev, NG)] = jnp.zeros((NG, GRP, D_WIDTH), bf)

    @pl.when(jnp.logical_not(first))
    def _():
        kn[0:D_SPAN, :] = kn[DBLK:DBLK + D_SPAN, :]
        vn[0:D_SPAN, :] = vn[DBLK:DBLK + D_SPAN, :]
        k4[0:2] = k4[NG:NG + 2]
        v4[0:2] = v4[NG:NG + 2]

    kn[D_SPAN:, :] = k_ref[...]
    vn[D_SPAN:, :] = v_ref[...]
    for g in range(NG):
        rows = slice(g * GRP, (g + 1) * GRP)
        for src, d4, d16 in ((k_ref, k4, k16), (v_ref, v4, v16)):
            x = src[rows, :]
            d4[2 + g] = _dot(perm_ref[0], x).astype(bf)
            d16[cur + g] = _dot(perm_ref[2], x).astype(bf)

    kk = lax.broadcasted_iota(jnp.int32, (1, 2 * SUB), 1)
    no_halo = jnp.where(kk < D_SPAN, jnp.where(first, NEG, 0.0), 0.0)

    def pair_cols(p):
        return slice(p * LANES, (p + 1) * LANES)

    for g in range(NG):
        qp[g] = _dot(perm_ref[2], q_ref[g * GRP:(g + 1) * GRP, :]).astype(bf)

    def class16(c, carry):
        r = pl.ds(pl.multiple_of(c * 16, 16), 16)
        for p in range(D_HEADS // 2):
            cols = pair_cols(p)
            q = jnp.concatenate([qp[g, r, cols] for g in range(NG)], axis=0)
            kp = jnp.concatenate([k16[prev + g, r, cols] for g in range(NG)]
                                 + [k16[cur + g, r, cols] for g in range(NG)], axis=0)
            vp = jnp.concatenate([v16[prev + g, r, cols] for g in range(NG)]
                                 + [v16[cur + g, r, cols] for g in range(NG)], axis=0)
            num, den = _pair_tile(q, kp, vp, tab_ref[2, p] + no_halo)
            for g in range(NG):
                res16[0, g, r, cols] = num[16 * g:16 * g + 16].astype(bf)
                res16[1, g, r, cols] = den[16 * g:16 * g + 16].astype(bf)
        return carry

    lax.fori_loop(0, 16, class16, 0)

    for g in range(NG):
        qp[g] = _dot(perm_ref[0], q_ref[g * GRP:(g + 1) * GRP, :]).astype(bf)

    def class4(c, carry):
        r = pl.ds(pl.multiple_of(c * 64, 64), 64)
        for qs in range(NG // 2):
            for p in range(D_HEADS // 2):
                cols = pair_cols(p)
                q = jnp.concatenate([qp[2 * qs + i, r, cols] for i in range(2)], axis=0)
                kp = jnp.concatenate([k4[2 * qs + i, r, cols] for i in range(4)], axis=0)
                vp = jnp.concatenate([v4[2 * qs + i, r, cols] for i in range(4)], axis=0)
                bias = tab_ref[1, p] + no_halo if qs == 0 else tab_ref[1, p]
                num, den = _pair_tile(q, kp, vp, bias)
                for i in range(2):
                    res4[0, 2 * qs + i, r, cols] = num[64 * i:64 * i + 64].astype(bf)
                    res4[1, 2 * qs + i, r, cols] = den[64 * i:64 * i + 64].astype(bf)
        return carry

    lax.fori_loop(0, 4, class4, 0)

    def group(g, carry):
        num = _dot(perm_ref[1], res4[0, g]) + _dot(perm_ref[3], res16[0, g])
        den = _dot(perm_ref[1], res4[1, g]) + _dot(perm_ref[3], res16[1, g])
        pen = jnp.where(g == 0, no_halo, 0.0)
        for t in range(GRP // SUB):
            q_rows = pl.ds(pl.multiple_of(g * GRP + t * SUB, SUB), SUB)
            k_rows = pl.ds(pl.multiple_of(g * GRP + t * SUB, SUB), 2 * SUB)
            sub = slice(t * SUB, (t + 1) * SUB)
            for p in range(D_HEADS // 2):
                cols = pair_cols(p)
                bias = tab_ref[0, p] + pen if t == 0 else tab_ref[0, p]
                n1, d1 = _pair_tile(q_ref[q_rows, cols], kn[k_rows, cols], vn[k_rows, cols], bias)
                o_ref[q_rows, cols] = ((n1 + num[sub, cols]) / (d1 + den[sub, cols])).astype(bf)
        return carry

    lax.fori_loop(0, NG, group, 0)


def _class_permutations():
    mats = []
    out = np.arange(GRP)
    for d in (4, 16):
        per = GRP // d
        p = np.zeros((GRP, GRP), np.float32)
        p[out, d * (out % per) + out // per] = 1.0
        mats += [p, p.T]
    return jnp.asarray(np.stack(mats), jnp.bfloat16)


def _dilated_bias_table(dil):
    ii = lax.broadcasted_iota(jnp.int32, (SUB, SUB + D_SPAN), 0)
    kk = lax.broadcasted_iota(jnp.int32, (SUB, SUB + D_SPAN), 1)
    delta = D_SPAN + ii - kk
    slopes = jnp.asarray(_alibi_slopes(D_HEADS), jnp.float32)
    bias = (-LOG2E) * slopes[:, None, None] * (delta * dil).astype(jnp.float32)[None]
    return jnp.where(((delta >= 0) & (delta <= D_SPAN))[None], bias, NEG)


def _dilated_fused(q, k, v, *, bsz, seq):
    nblk = seq // DBLK
    tabs = jnp.stack([_dilated_bias_table(d).reshape(D_HEADS // 2, 2 * SUB, SUB + D_SPAN)
                      for d in D_DILATIONS])
    blk_spec = pl.BlockSpec((DBLK, D_WIDTH), lambda b, i: (b * nblk + i, 0))
    group_buf = lambda n: pltpu.VMEM((n, GRP, D_WIDTH), jnp.bfloat16)
    return pl.pallas_call(
        _dilated_fused_kernel,
        grid=(bsz, nblk),
        in_specs=[
            blk_spec, blk_spec, blk_spec,
            pl.BlockSpec((4, GRP, GRP), lambda b, i: (0, 0, 0), pipeline_mode=pl.Buffered(1)),
            pl.BlockSpec(tabs.shape, lambda b, i: (0, 0, 0, 0), pipeline_mode=pl.Buffered(1)),
        ],
        out_specs=blk_spec,
        out_shape=jax.ShapeDtypeStruct((bsz * seq, D_WIDTH), jnp.bfloat16),
        scratch_shapes=[
            pltpu.VMEM((D_SPAN + DBLK, D_WIDTH), jnp.bfloat16),
            pltpu.VMEM((D_SPAN + DBLK, D_WIDTH), jnp.bfloat16),
            group_buf(2 + NG), group_buf(2 + NG),
            group_buf(2 * NG), group_buf(2 * NG),
            group_buf(NG),
            pltpu.VMEM((2, NG, GRP, D_WIDTH), jnp.bfloat16),
            pltpu.VMEM((2, NG, GRP, D_WIDTH), jnp.bfloat16),
        ],
        compiler_params=pltpu.CompilerParams(
            dimension_semantics=("arbitrary",) * 2, vmem_limit_bytes=VMEM_LIMIT),
        name="dilated_attention_fused",
    )(q, k, v, _class_permutations(), tabs)


def _dilated_kernel(q_ref, kh_ref, kc_ref, vh_ref, vc_ref, bias_ref, o_ref, lse_ref,
                    kbuf, vbuf, *, tl):
    j = pl.program_id(2)
    kbuf[0:D_SPAN, :] = kh_ref[...]
    kbuf[D_SPAN:, :] = kc_ref[...]
    vbuf[0:D_SPAN, :] = vh_ref[...]
    vbuf[D_SPAN:, :] = vc_ref[...]
    lane = lax.broadcasted_iota(jnp.int32, (1, LANES), 1)
    kk = lax.broadcasted_iota(jnp.int32, (1, SUB + D_SPAN), 1)
    halo_pen = jnp.where(kk < D_SPAN, jnp.where(j == 0, NEG, 0.0), 0.0)
    for qs in range(tl // SUB):
        rows = slice(qs * SUB, (qs + 1) * SUB)
        krows = slice(qs * SUB, (qs + 1) * SUB + D_SPAN)
        lse_tile = jnp.zeros((SUB, LANES), jnp.float32)
        for p in range(D_HEADS // 2):
            cols = slice(p * LANES, (p + 1) * LANES)
            q = q_ref[rows, cols]
            kp = kbuf[krows, cols]
            vp = vbuf[krows, cols]
            zero = jnp.zeros_like(q)
            halves = []
            for e in range(2):
                hd = 2 * p + e
                qm = jnp.where((lane < HEAD_DIM) if e == 0 else (lane >= HEAD_DIM), q, zero)
                s = _dot_nt(qm, kp) + bias_ref[hd]
                if qs == 0:
                    s = s + halo_pen
                m = jnp.max(s, axis=-1, keepdims=True)
                pe = jnp.exp2(s - m)
                l = jnp.sum(pe, axis=-1, keepdims=True)
                halves.append(_dot(pe.astype(jnp.bfloat16), vp) / l)
                lse_tile = jnp.where(lane == hd, m + jnp.log2(l), lse_tile)
            o_ref[rows, cols] = jnp.where(lane < HEAD_DIM, halves[0], halves[1]).astype(jnp.bfloat16)
        lse_ref[rows, :] = lse_tile


def _dilated_branch(q, k, v, *, bsz, seq, dil):
    tl = TL
    length = seq // dil
    nb = length // tl
    halo_per_blk = tl // D_SPAN
    view = lambda a: a.reshape(bsz, length, dil * D_WIDTH)
    cur = lambda b, r, j: (b, j, r)
    halo = lambda b, r, j: (b, jnp.maximum(j * halo_per_blk - 1, 0), r)
    o, lse = pl.pallas_call(
        functools.partial(_dilated_kernel, tl=tl),
        grid=(bsz, dil, nb),
        in_specs=[
            pl.BlockSpec((None, tl, D_WIDTH), cur),
            pl.BlockSpec((None, D_SPAN, D_WIDTH), halo),
            pl.BlockSpec((None, tl, D_WIDTH), cur),
            pl.BlockSpec((None, D_SPAN, D_WIDTH), halo),
            pl.BlockSpec((None, tl, D_WIDTH), cur),
            pl.BlockSpec((D_HEADS, SUB, SUB + D_SPAN), lambda b, r, j: (0, 0, 0)),
        ],
        out_specs=[
            pl.BlockSpec((None, tl, D_WIDTH), cur),
            pl.BlockSpec((None, tl, LANES), cur),
        ],
        out_shape=[
            jax.ShapeDtypeStruct((bsz, length, dil * D_WIDTH), jnp.bfloat16),
            jax.ShapeDtypeStruct((bsz, length, dil * LANES), jnp.float32),
        ],
        scratch_shapes=[pltpu.VMEM((tl + D_SPAN, D_WIDTH), jnp.bfloat16)] * 2,
        compiler_params=pltpu.CompilerParams(
            dimension_semantics=("arbitrary",) * 3, vmem_limit_bytes=VMEM_LIMIT),
        name=f"dilated_attention_d{dil}",
    )(view(q), view(k), view(k), view(v), view(v), _dilated_bias_table(dil))
    return o.reshape(bsz * seq, D_WIDTH), lse.reshape(bsz * seq, LANES)


def _branch_combine_kernel(o1_ref, o2_ref, o3_ref, l1_ref, l2_ref, l3_ref, y_ref):
    lses = [l1_ref[...], l2_ref[...], l3_ref[...]]
    mx = jnp.maximum(jnp.maximum(lses[0], lses[1]), lses[2])
    es = [jnp.exp2(l - mx) for l in lses]
    inv = 1.0 / (es[0] + es[1] + es[2])
    ws = [e * inv for e in es]
    lane = lax.broadcasted_iota(jnp.int32, (1, LANES), 1)
    for p in range(D_HEADS // 2):
        cols = slice(p * LANES, (p + 1) * LANES)
        yd = None
        for w, o in zip(ws, (o1_ref, o2_ref, o3_ref)):
            wf = jnp.where(lane < HEAD_DIM, w[:, 2 * p:2 * p + 1], w[:, 2 * p + 1:2 * p + 2])
            term = wf * o[:, cols].astype(jnp.float32)
            yd = term if yd is None else yd + term
        y_ref[:, cols] = yd.astype(jnp.bfloat16)


def _dilated_by_branch(q, k, v, *, bsz, seq):
    outs, lses = zip(*[_dilated_branch(q, k, v, bsz=bsz, seq=seq, dil=d) for d in D_DILATIONS])
    tm = TM_PROJ
    row = lambda i: (i, 0)
    return pl.pallas_call(
        _branch_combine_kernel,
        grid=(bsz * seq // tm,),
        in_specs=[pl.BlockSpec((tm, D_WIDTH), row)] * 3 + [pl.BlockSpec((tm, LANES), row)] * 3,
        out_specs=pl.BlockSpec((tm, D_WIDTH), row),
        out_shape=jax.ShapeDtypeStruct((bsz * seq, D_WIDTH), jnp.bfloat16),
        compiler_params=pltpu.CompilerParams(
            dimension_semantics=("arbitrary",), vmem_limit_bytes=VMEM_LIMIT),
        name="dilated_branch_combine",
    )(*outs, *lses)


def kernel(x, norm_mix, norm_ffn, ev_w_in, ev_w_out, ev_q_gain, ev_k_gain, ev_lambda_q1, ev_lambda_k1, ev_lambda_q2, ev_lambda_k2, ev_subln_gain, ev_pool_w, ev_pool_scale, od_w_in, od_w_out, od_conv_w, od_q_gain, od_k_gain, ffn_w_gate, ffn_w_up, ffn_w_down):
    bsz, seq, _ = x.shape
    bf = jnp.bfloat16
    f32 = jnp.float32
    q_scale = HEAD_DIM ** -0.5 * LOG2E
    x2d = x.reshape(bsz * seq, D_MODEL)
    grp = lax.broadcasted_iota(jnp.int32, (MXU_DEPTH, MXU_DEPTH), 0) // HEAD_DIM
    gsum = (grp == grp.T).astype(bf)
    pw = ev_pool_w[0].astype(bf)
    zpw = jnp.zeros_like(pw[0])
    pool_w = jnp.stack([jnp.block([[pw[2 * i], zpw], [zpw, pw[2 * i + 1]]]) for i in range(2)])
    tile8 = lambda gain, s: (jnp.tile(gain.astype(f32), 8) * s).reshape(1, 512)

    w = ev_w_in[0]
    regroup = lambda c: c.reshape(D_MODEL, 2, A_HEADS, HEAD_DIM).transpose(0, 2, 1, 3).reshape(D_MODEL, 512)
    w_even = jnp.concatenate([regroup(w[:, 0:512]), regroup(w[:, 512:1024]), w[:, 1024:]], axis=1).astype(bf)
    q1, q2, k1, k2, v, yb = _even_in(
        x2d, norm_mix[0].reshape(1, D_MODEL).astype(f32), w_even, gsum,
        tile8(ev_q_gain[0], q_scale), tile8(ev_k_gain[0], 1.0),
        pool_w, ev_pool_scale[0].reshape(1, 512).astype(f32), seq=seq)
    lam_vecs = jnp.stack([ev_lambda_q1[0], ev_lambda_k1[0], ev_lambda_q2[0], ev_lambda_k2[0]]).astype(f32)
    attn = functools.partial(_diff_attn, q1, q2, k1, k2, v, lam_vecs,
                             ev_subln_gain[0].reshape(1, A_V).astype(f32), bsz=bsz, seq=seq)
    ya = lax.cond(_score_bound(ev_q_gain[0], ev_k_gain[0]) <= MAX_UNSHIFTED_SCORE,
                  lambda: attn(online=False), lambda: attn(online=True))
    x2d = _post(x2d, ya, yb, ev_w_out[0].astype(bf),
                norm_ffn[0].reshape(1, D_MODEL).astype(f32), ffn_w_gate[0].astype(bf),
                ffn_w_up[0].astype(bf), ffn_w_down[0].astype(bf), "even_post")

    yc, q, k, v = _odd_in(
        x2d, norm_mix[1].reshape(1, D_MODEL).astype(f32), od_w_in[0].astype(bf), gsum,
        tile8(od_q_gain[0], q_scale), tile8(od_k_gain[0], 1.0), od_conv_w[0].astype(f32), seq=seq)
    yd = lax.cond(_score_bound(od_q_gain[0], od_k_gain[0]) <= MAX_UNSHIFTED_SCORE,
                  lambda: _dilated_fused(q, k, v, bsz=bsz, seq=seq),
                  lambda: _dilated_by_branch(q, k, v, bsz=bsz, seq=seq))
    x2d = _post(x2d, yc, yd, od_w_out[0].astype(bf),
                norm_ffn[1].reshape(1, D_MODEL).astype(f32), ffn_w_gate[1].astype(bf),
                ffn_w_up[1].astype(bf), ffn_w_down[1].astype(bf), "odd_post")
    return x2d.reshape(bsz, seq, D_MODEL)
```

```python
import functools
import math

import numpy as np
import jax
import jax.numpy as jnp
from jax import lax
from jax.experimental import pallas as pl
from jax.experimental.pallas import tpu as pltpu

D_MODEL = 1024
HEAD_DIM = 64
EPS = 1e-6
A_HEADS = 4
A_V = 128
B_WINDOWS = (2, 4, 8, 16)
POOL_HALO = 16
CONV_HALO = 8
D_HEADS = 8
D_WIDTH = D_HEADS * HEAD_DIM
D_DILATIONS = (1, 4, 16)
D_SPAN = 128
D_FF = 2816
LAMBDA_INIT_L0 = 0.8 - 0.6 * math.exp(-0.3 * 0)
NEG = -1e30
LOG2E = math.log2(math.e)
LOG2E_HI = float(np.float32(LOG2E).astype(jnp.bfloat16))
LOG2E_LO = float(np.float32(LOG2E - LOG2E_HI).astype(jnp.bfloat16))
MAX_UNSHIFTED_SCORE = 60.0

LANES = 128
MXU_DEPTH = 256
VMEM_LIMIT = 56 * 1024 * 1024

TM_PROJ = 512
TM_IN = 1024
SM_IN = 512
TQ = 1024
TL = 512
SUB = 128
DBLK = 2048
GRP = 256
NG = DBLK // GRP

_NT = (((1,), (1,)), ((), ()))


def _dot(a, b):
    return jnp.dot(a, b, preferred_element_type=jnp.float32)


def _dot_nt(a, b):
    return lax.dot_general(a, b, _NT, preferred_element_type=jnp.float32)


def _rms_scale(x):
    return lax.rsqrt(jnp.mean(x * x, axis=-1, keepdims=True) + EPS)


def _group_rmsnorm(t, gsum_ref, gain_ref):
    sq = (t * t).astype(jnp.bfloat16)
    half = gsum_ref.shape[0]
    ssq = jnp.concatenate([_dot(sq[:, :half], gsum_ref[...]), _dot(sq[:, half:], gsum_ref[...])],
                          axis=1)
    return t * lax.rsqrt(ssq * (1.0 / HEAD_DIM) + EPS) * gain_ref[...]


def _alibi_slopes(n):
    return [2.0 ** (-8.0 * (i + 1) / n) for i in range(n)]


def _alibi_lanes(pos, lane, first):
    rel = lane - first
    hi = ((pos >> 8) << 8).astype(jnp.float32)
    lo = (pos & 255).astype(jnp.float32)
    pos_sel = jnp.where((rel & 1) == 0, hi, lo)
    l_sel = jnp.where((rel & 3) < 2, LOG2E_HI, LOG2E_LO)
    first4 = (rel >= 0) & (rel < 4)
    last4 = (rel >= 4) & (rel < 8)
    return (jnp.where(last4, pos_sel, 0.0), jnp.where(first4, l_sel, 0.0),
            jnp.where(first4, pos_sel, 0.0), jnp.where(last4, l_sel, 0.0))


def _shift_rows(ext, d):
    return pltpu.roll(ext, d, axis=0)


def _with_ones(v):
    return jnp.concatenate([v, jnp.ones_like(v)], axis=1)


def _score_bound(q_gain, k_gain):
    return (HEAD_DIM ** 0.5) * jnp.max(jnp.abs(q_gain)) * jnp.max(jnp.abs(k_gain))


def _even_in_kernel(x_ref, g_ref, w_ref, gsum_ref, qg_ref, kg_ref, pw_ref, ps_ref,
                    q1_ref, q2_ref, k1_ref, k2_ref, v_ref, yb_ref, carry_ref,
                    *, tm, sm, blocks_per_seq):
    blk = pl.program_id(0) % blocks_per_seq

    @pl.when(blk == 0)
    def _():
        carry_ref[...] = jnp.zeros_like(carry_ref)

    for sb in range(tm // sm):
        rows = slice(sb * sm, (sb + 1) * sm)
        x = x_ref[rows, :]
        h = (x * _rms_scale(x) * g_ref[...]).astype(jnp.bfloat16)
        proj = lambda part: _dot(h, w_ref[:, part * 512:(part + 1) * 512])
        qn = _group_rmsnorm(proj(0), gsum_ref, qg_ref)
        kn = _group_rmsnorm(proj(1), gsum_ref, kg_ref)
        v_ref[rows, :] = proj(2).astype(jnp.bfloat16)

        pos = lax.broadcasted_iota(jnp.int32, (sm, 1), 0) + (blk * tm + sb * sm)
        lane = lax.broadcasted_iota(jnp.int32, (1, LANES), 1)
        low = lane < HEAD_DIM
        for mp, (q_out, k_out) in enumerate(((q1_ref, k1_ref), (q2_ref, k2_ref))):
            q_pos, q_const, k_pos, k_const = _alibi_lanes(pos, lane, HEAD_DIM if mp == 0 else 0)
            data = low if mp == 0 else jnp.logical_not(low)
            for hd, slope in enumerate(_alibi_slopes(A_HEADS)):
                cols = slice(hd * LANES, (hd + 1) * LANES)
                q_out[rows, cols] = jnp.where(data, qn[:, cols], q_const - slope * q_pos).astype(jnp.bfloat16)
                k_out[rows, cols] = jnp.where(data, kn[:, cols], k_const + slope * k_pos).astype(jnp.bfloat16)

        u = proj(3)
        ext = jnp.concatenate([carry_ref[...], u], axis=0)
        pos1 = (pos + 1).astype(jnp.float32)
        pooled = []
        for g, w in enumerate(B_WINDOWS):
            cols = slice(g * LANES, (g + 1) * LANES)
            s = ext[:, cols]
            d = 1
            while d < w:
                s = s + _shift_rows(s, d)
                d *= 2
            pooled.append((s[POOL_HALO:, :] / jnp.minimum(pos1, float(w)) - u[:, cols]).astype(jnp.bfloat16))
        for pg in range(len(B_WINDOWS) // 2):
            cols = slice(2 * pg * LANES, (2 * pg + 2) * LANES)
            yb = _dot(jnp.concatenate(pooled[2 * pg:2 * pg + 2], axis=1), pw_ref[pg]) * ps_ref[:, cols]
            yb_ref[rows, cols] = yb.astype(jnp.bfloat16)
        carry_ref[...] = u[sm - POOL_HALO:, :]


def _even_in(x2d, g, w, gsum, qg, kg, pw, ps, *, seq):
    t = x2d.shape[0]
    tm = TM_IN
    const = lambda i: (0, 0)
    row = lambda i: (i, 0)
    out = jax.ShapeDtypeStruct((t, 512), jnp.bfloat16)
    return pl.pallas_call(
        functools.partial(_even_in_kernel, tm=tm, sm=SM_IN, blocks_per_seq=seq // tm),
        grid=(t // tm,),
        in_specs=[
            pl.BlockSpec((tm, D_MODEL), row),
            pl.BlockSpec((1, D_MODEL), const),
            pl.BlockSpec((D_MODEL, 2048), const),
            pl.BlockSpec((MXU_DEPTH, MXU_DEPTH), const),
            pl.BlockSpec((1, 512), const),
            pl.BlockSpec((1, 512), const),
            pl.BlockSpec((2, MXU_DEPTH, MXU_DEPTH), lambda i: (0, 0, 0)),
            pl.BlockSpec((1, 512), const),
        ],
        out_specs=[pl.BlockSpec((tm, 512), row)] * 6,
        out_shape=[out] * 6,
        scratch_shapes=[pltpu.VMEM((POOL_HALO, 512), jnp.float32)],
        compiler_params=pltpu.CompilerParams(
            dimension_semantics=("arbitrary",), vmem_limit_bytes=VMEM_LIMIT),
        name="even_in_proj",
    )(x2d, g, w, gsum, qg, kg, pw, ps)


def _diff_attn_kernel(q1_ref, q2_ref, k1_ref, k2_ref, v_ref, lam_ref, sg_ref, o_ref,
                      *, tq, online):
    tk = tq // 2
    qi = pl.program_id(2)
    qs = (q1_ref[...], q2_ref[...])
    k_refs = (k1_ref, k2_ref)

    def key_rows(j):
        return pl.ds(pl.multiple_of(j * tk, tk), tk)

    def scores(j, q_list, first_key=None):
        out = []
        for q, k_ref in zip(q_list, k_refs):
            s = _dot_nt(q, k_ref[key_rows(j), :])
            if first_key is not None:
                ii = lax.broadcasted_iota(jnp.int32, s.shape, 0)
                jj = lax.broadcasted_iota(jnp.int32, s.shape, 1)
                s = jnp.where(jj + first_key > ii, NEG, s)
            out.append(s)
        return out

    def pv(j, ps):
        v1 = _with_ones(v_ref[key_rows(j), :])
        return [_dot(p, v1) for p in ps]

    if online:
        def tile(j, state, first_key=None):
            out = []
            for mp, s in enumerate(scores(j, qs, first_key)):
                m_new = jnp.max(s, axis=-1, keepdims=True)
                if state is not None:
                    m_new = jnp.maximum(state[2 * mp], m_new)
                out += [m_new, jnp.exp2(s - m_new).astype(jnp.bfloat16)]
            accs = pv(j, out[1::2])
            if state is not None:
                accs = [jnp.exp2(state[2 * mp] - out[2 * mp]) * state[2 * mp + 1] + accs[mp]
                        for mp in range(2)]
            return (out[0], accs[0], out[2], accs[1])

        state = tile(2 * qi + 1, tile(2 * qi, None, 0), tk)
        _, acc1, _, acc2 = lax.fori_loop(0, 2 * qi, tile, state)
    else:
        def tile(j, q_list, first_key=None):
            return pv(j, [jnp.exp2(s).astype(jnp.bfloat16) for s in scores(j, q_list, first_key)])

        upper = tile(2 * qi, qs, 0)
        lower = tile(2 * qi + 1, [q[tk:] for q in qs], 0)
        state = tuple(jnp.concatenate([a[:tk], a[tk:] + b], axis=0) for a, b in zip(upper, lower))

        def two_tiles(i, state):
            for t in range(2):
                state = tuple(a + b for a, b in zip(state, tile(2 * i + t, qs)))
            return state

        acc1, acc2 = lax.fori_loop(0, qi, two_tiles, state)

    lv = lam_ref[...]
    lam = (jnp.exp(jnp.sum(lv[0:1] * lv[1:2], axis=-1, keepdims=True))
           - jnp.exp(jnp.sum(lv[2:3] * lv[3:4], axis=-1, keepdims=True)) + LAMBDA_INIT_L0)
    o = acc1[:, :A_V] / acc1[:, A_V:] - lam * (acc2[:, :A_V] / acc2[:, A_V:])
    o = o * _rms_scale(o) * sg_ref[...] * (1.0 - LAMBDA_INIT_L0)
    o_ref[...] = o.astype(jnp.bfloat16)


def _diff_attn(q1, q2, k1, k2, v, lam_vecs, subln_gain, *, bsz, seq, online):
    tq = TQ
    nq = seq // tq
    q_spec = pl.BlockSpec((tq, LANES), lambda b, h, i: (b * nq + i, h))
    kv_spec = pl.BlockSpec((seq, LANES), lambda b, h, i: (b, h))
    return pl.pallas_call(
        functools.partial(_diff_attn_kernel, tq=tq, online=online),
        grid=(bsz, A_HEADS, nq),
        in_specs=[
            q_spec, q_spec, kv_spec, kv_spec, kv_spec,
            pl.BlockSpec((4, HEAD_DIM), lambda b, h, i: (0, 0)),
            pl.BlockSpec((1, A_V), lambda b, h, i: (0, 0)),
        ],
        out_specs=q_spec,
        out_shape=jax.ShapeDtypeStruct((bsz * seq, A_HEADS * A_V), jnp.bfloat16),
        compiler_params=pltpu.CompilerParams(
            dimension_semantics=("arbitrary",) * 3, vmem_limit_bytes=VMEM_LIMIT),
        name="diff_attention_online" if online else "diff_attention",
    )(q1, q2, k1, k2, v, lam_vecs, subln_gain)


def _post_kernel(x_ref, ya_ref, yb_ref, wo_ref, g_ref, wg_ref, wu_ref, wd_ref, o_ref):
    half = ya_ref.shape[1]
    x1 = x_ref[...] + _dot(ya_ref[...], wo_ref[0:half, :]) + _dot(yb_ref[...], wo_ref[half:, :])
    h = (x1 * _rms_scale(x1) * g_ref[...]).astype(jnp.bfloat16)
    gate = _dot(h, wg_ref[...])
    up = _dot(h, wu_ref[...])
    act = (gate * jax.nn.sigmoid(gate) * up).astype(jnp.bfloat16)
    o_ref[...] = x1 + _dot(act, wd_ref[...])


def _post(x2d, ya, yb, wo, g, wg, wu, wd, layer, name):
    t = x2d.shape[0]
    tm = TM_PROJ
    row = lambda i: (i, 0)
    const = lambda i: (0, 0)
    resident = functools.partial(pl.BlockSpec, index_map=const, pipeline_mode=pl.Buffered(1))
    ffn_w = lambda rows, cols: pl.BlockSpec((None, rows, cols), lambda i: (layer, 0, 0),
                                            pipeline_mode=pl.Buffered(1))
    return pl.pallas_call(
        _post_kernel,
        grid=(t // tm,),
        in_specs=[
            pl.BlockSpec((tm, D_MODEL), row),
            pl.BlockSpec((tm, ya.shape[1]), row),
            pl.BlockSpec((tm, yb.shape[1]), row),
            resident((D_MODEL, D_MODEL)),
            pl.BlockSpec((1, D_MODEL), const),
            ffn_w(D_MODEL, D_FF),
            ffn_w(D_MODEL, D_FF),
            ffn_w(D_FF, D_MODEL),
        ],
        out_specs=pl.BlockSpec((tm, D_MODEL), row),
        out_shape=jax.ShapeDtypeStruct((t, D_MODEL), jnp.float32),
        compiler_params=pltpu.CompilerParams(
            dimension_semantics=("arbitrary",), vmem_limit_bytes=VMEM_LIMIT),
        name=name,
    )(x2d, ya, yb, wo, g, wg, wu, wd)


def _odd_in_kernel(x_ref, g_ref, w_ref, gsum_ref, qg_ref, kg_ref, cw_ref,
                   yc_ref, q_ref, k_ref, v_ref, carry_ref, *, tm, sm, blocks_per_seq):
    blk = pl.program_id(0) % blocks_per_seq

    @pl.when(blk == 0)
    def _():
        carry_ref[...] = jnp.zeros_like(carry_ref)

    for sb in range(tm // sm):
        rows = slice(sb * sm, (sb + 1) * sm)
        x = x_ref[rows, :]
        h = (x * _rms_scale(x) * g_ref[...]).astype(jnp.bfloat16)
        proj = lambda part: _dot(h, w_ref[:, part * 512:(part + 1) * 512])
        z = proj(1) * proj(2)
        bg = proj(0)
        ext = jnp.concatenate([carry_ref[...], z], axis=0)
        z1 = _shift_rows(ext, 1)[CONV_HALO:, :]
        z2 = _shift_rows(ext, 2)[CONV_HALO:, :]
        cw = cw_ref[...]
        yc = bg * (cw[0:1] * z2 + cw[1:2] * z1 + cw[2:3] * z)
        yc_ref[rows, :] = yc.astype(jnp.bfloat16)
        carry_ref[...] = z[sm - CONV_HALO:, :]

        q_ref[rows, :] = _group_rmsnorm(proj(3), gsum_ref, qg_ref).astype(jnp.bfloat16)
        k_ref[rows, :] = _group_rmsnorm(proj(4), gsum_ref, kg_ref).astype(jnp.bfloat16)
        v_ref[rows, :] = proj(5).astype(jnp.bfloat16)


def _odd_in(x2d, g, w, gsum, qg, kg, cw, *, seq):
    t = x2d.shape[0]
    tm = TM_IN
    const = lambda i: (0, 0)
    row = lambda i: (i, 0)
    out = jax.ShapeDtypeStruct((t, 512), jnp.bfloat16)
    return pl.pallas_call(
        functools.partial(_odd_in_kernel, tm=tm, sm=SM_IN, blocks_per_seq=seq // tm),
        grid=(t // tm,),
        in_specs=[
            pl.BlockSpec((tm, D_MODEL), row),
            pl.BlockSpec((1, D_MODEL), const),
            pl.BlockSpec((D_MODEL, 3072), const),
            pl.BlockSpec((MXU_DEPTH, MXU_DEPTH), const),
            pl.BlockSpec((1, 512), const),
            pl.BlockSpec((1, 512), const),
            pl.BlockSpec((3, 512), const),
        ],
        out_specs=[pl.BlockSpec((tm, 512), row)] * 4,
        out_shape=[out] * 4,
        scratch_shapes=[pltpu.VMEM((CONV_HALO, 512), jnp.float32)],
        compiler_params=pltpu.CompilerParams(
            dimension_semantics=("arbitrary",), vmem_limit_bytes=VMEM_LIMIT),
        name="odd_in_proj",
    )(x2d, g, w, gsum, qg, kg, cw)


def _pair_tile(q, kp, vp, bias):
    lane = lax.broadcasted_iota(jnp.int32, (1, LANES), 1)
    low = lane < HEAD_DIM
    zero = jnp.zeros_like(q)
    q2 = jnp.concatenate([jnp.where(low, q, zero), jnp.where(low, zero, q)], axis=0)
    p = jnp.exp2(_dot_nt(q2, kp) + bias).astype(jnp.bfloat16)
    r = _dot(p, _with_ones(vp))
    num = jnp.where(low, r[:SUB, :LANES], r[SUB:, :LANES])
    den = jnp.where(low, r[:SUB, LANES:], r[SUB:, LANES:])
    return num, den


def _dilated_fused_kernel(q_ref, k_ref, v_ref, perm_ref, tab_ref, o_ref,
                          kn, vn, k4, v4, k16, v16, qp, res4, res16):
    blk = pl.program_id(1)
    first = blk == 0
    cur = (blk & 1) * NG
    prev = NG - cur
    bf = jnp.bfloat16

    @pl.when(first)
    def _():
        kn[0:D_SPAN, :] = jnp.zeros((D_SPAN, D_WIDTH), bf)
        vn[0:D_SPAN, :] = jnp.zeros((D_SPAN, D_WIDTH), bf)
        k4[0:2] = jnp.zeros((2, GRP, D_WIDTH), bf)
        v4[0:2] = jnp.zeros((2, GRP, D_WIDTH), bf)
        k16[pl.ds(prev, NG)] = jnp.zeros((NG, GRP, D_WIDTH), bf)
        v16[pl.ds(prev, NG)] = jnp.zeros((NG, GRP, D_WIDTH), bf)

    @pl.when(jnp.logical_not(first))
    def _():
        kn[0:D_SPAN, :] = kn[DBLK:DBLK + D_SPAN, :]
        vn[0:D_SPAN, :] = vn[DBLK:DBLK + D_SPAN, :]
        k4[0:2] = k4[NG:NG + 2]
        v4[0:2] = v4[NG:NG + 2]

    kn[D_SPAN:, :] = k_ref[...]
    vn[D_SPAN:, :] = v_ref[...]
    for g in range(NG):
        rows = slice(g * GRP, (g + 1) * GRP)
        for src, d4, d16 in ((k_ref, k4, k16), (v_ref, v4, v16)):
            x = src[rows, :]
            d4[2 + g] = _dot(perm_ref[0], x).astype(bf)
            d16[cur + g] = _dot(perm_ref[2], x).astype(bf)

    kk = lax.broadcasted_iota(jnp.int32, (1, 2 * SUB), 1)
    no_halo = jnp.where(kk < D_SPAN, jnp.where(first, NEG, 0.0), 0.0)

    def pair_cols(p):
        return slice(p * LANES, (p + 1) * LANES)

    for g in range(NG):
        qp[g] = _dot(perm_ref[2], q_ref[g * GRP:(g + 1) * GRP, :]).astype(bf)

    def classes16(i, carry):
        for t in range(4):
            r = pl.ds(pl.multiple_of(i * 64 + t * 16, 16), 16)
            for p in range(D_HEADS // 2):
                cols = pair_cols(p)
                q = jnp.concatenate([qp[g, r, cols] for g in range(NG)], axis=0)
                kp = jnp.concatenate([k16[prev + g, r, cols] for g in range(NG)]
                                     + [k16[cur + g, r, cols] for g in range(NG)], axis=0)
                vp = jnp.concatenate([v16[prev + g, r, cols] for g in range(NG)]
                                     + [v16[cur + g, r, cols] for g in range(NG)], axis=0)
                num, den = _pair_tile(q, kp, vp, tab_ref[2, p] + no_halo)
                for g in range(NG):
                    res16[0, g, r, cols] = num[16 * g:16 * g + 16].astype(bf)
                    res16[1, g, r, cols] = den[16 * g:16 * g + 16].astype(bf)
        return carry

    lax.fori_loop(0, 4, classes16, 0)

    for g in range(NG):
        qp[g] = _dot(perm_ref[0], q_ref[g * GRP:(g + 1) * GRP, :]).astype(bf)

    def class4(c, carry):
        r = pl.ds(pl.multiple_of(c * 64, 64), 64)
        for qs in range(NG // 2):
            for p in range(D_HEADS // 2):
                cols = pair_cols(p)
                q = jnp.concatenate([qp[2 * qs + i, r, cols] for i in range(2)], axis=0)
                kp = jnp.concatenate([k4[2 * qs + i, r, cols] for i in range(4)], axis=0)
                vp = jnp.concatenate([v4[2 * qs + i, r, cols] for i in range(4)], axis=0)
                bias = tab_ref[1, p] + no_halo if qs == 0 else tab_ref[1, p]
                num, den = _pair_tile(q, kp, vp, bias)
                for i in range(2):
                    res4[0, 2 * qs + i, r, cols] = num[64 * i:64 * i + 64].astype(bf)
                    res4[1, 2 * qs + i, r, cols] = den[64 * i:64 * i + 64].astype(bf)
        return carry

    lax.fori_loop(0, 4, class4, 0)

    def group(g, carry):
        num = _dot(perm_ref[1], res4[0, g]) + _dot(perm_ref[3], res16[0, g])
        den = _dot(perm_ref[1], res4[1, g]) + _dot(perm_ref[3], res16[1, g])
        pen = jnp.where(g == 0, no_halo, 0.0)
        for t in range(GRP // SUB):
            q_rows = pl.ds(pl.multiple_of(g * GRP + t * SUB, SUB), SUB)
            k_rows = pl.ds(pl.multiple_of(g * GRP + t * SUB, SUB), 2 * SUB)
            sub = slice(t * SUB, (t + 1) * SUB)
            for p in range(D_HEADS // 2):
                cols = pair_cols(p)
                bias = tab_ref[0, p] + pen if t == 0 else tab_ref[0, p]
                n1, d1 = _pair_tile(q_ref[q_rows, cols], kn[k_rows, cols], vn[k_rows, cols], bias)
                o_ref[q_rows, cols] = ((n1 + num[sub, cols]) / (d1 + den[sub, cols])).astype(bf)
        return carry

    lax.fori_loop(0, NG, group, 0)


def _class_permutations():
    mats = []
    out = np.arange(GRP)
    for d in (4, 16):
        per = GRP // d
        p = np.zeros((GRP, GRP), np.float32)
        p[out, d * (out % per) + out // per] = 1.0
        mats += [p, p.T]
    return jnp.asarray(np.stack(mats), jnp.bfloat16)


def _dilated_bias_table(dil):
    ii = lax.broadcasted_iota(jnp.int32, (SUB, SUB + D_SPAN), 0)
    kk = lax.broadcasted_iota(jnp.int32, (SUB, SUB + D_SPAN), 1)
    delta = D_SPAN + ii - kk
    slopes = jnp.asarray(_alibi_slopes(D_HEADS), jnp.float32)
    bias = (-LOG2E) * slopes[:, None, None] * (delta * dil).astype(jnp.float32)[None]
    return jnp.where(((delta >= 0) & (delta <= D_SPAN))[None], bias, NEG)


def _dilated_fused(q, k, v, *, bsz, seq):
    nblk = seq // DBLK
    tabs = jnp.stack([_dilated_bias_table(d).reshape(D_HEADS // 2, 2 * SUB, SUB + D_SPAN)
                      for d in D_DILATIONS])
    blk_spec = pl.BlockSpec((DBLK, D_WIDTH), lambda b, i: (b * nblk + i, 0))
    group_buf = lambda n: pltpu.VMEM((n, GRP, D_WIDTH), jnp.bfloat16)
    return pl.pallas_call(
        _dilated_fused_kernel,
        grid=(bsz, nblk),
        in_specs=[
            blk_spec, blk_spec, blk_spec,
            pl.BlockSpec((4, GRP, GRP), lambda b, i: (0, 0, 0), pipeline_mode=pl.Buffered(1)),
            pl.BlockSpec(tabs.shape, lambda b, i: (0, 0, 0, 0), pipeline_mode=pl.Buffered(1)),
        ],
        out_specs=blk_spec,
        out_shape=jax.ShapeDtypeStruct((bsz * seq, D_WIDTH), jnp.bfloat16),
        scratch_shapes=[
            pltpu.VMEM((D_SPAN + DBLK, D_WIDTH), jnp.bfloat16),
            pltpu.VMEM((D_SPAN + DBLK, D_WIDTH), jnp.bfloat16),
            group_buf(2 + NG), group_buf(2 + NG),
            group_buf(2 * NG), group_buf(2 * NG),
            group_buf(NG),
            pltpu.VMEM((2, NG, GRP, D_WIDTH), jnp.bfloat16),
            pltpu.VMEM((2, NG, GRP, D_WIDTH), jnp.bfloat16),
        ],
        compiler_params=pltpu.CompilerParams(
            dimension_semantics=("arbitrary",) * 2, vmem_limit_bytes=VMEM_LIMIT),
        name="dilated_attention_fused",
    )(q, k, v, _class_permutations(), tabs)


def _dilated_kernel(q_ref, kh_ref, kc_ref, vh_ref, vc_ref, bias_ref, o_ref, lse_ref,
                    kbuf, vbuf, *, tl):
    j = pl.program_id(2)
    kbuf[0:D_SPAN, :] = kh_ref[...]
    kbuf[D_SPAN:, :] = kc_ref[...]
    vbuf[0:D_SPAN, :] = vh_ref[...]
    vbuf[D_SPAN:, :] = vc_ref[...]
    lane = lax.broadcasted_iota(jnp.int32, (1, LANES), 1)
    kk = lax.broadcasted_iota(jnp.int32, (1, SUB + D_SPAN), 1)
    halo_pen = jnp.where(kk < D_SPAN, jnp.where(j == 0, NEG, 0.0), 0.0)
    for qs in range(tl // SUB):
        rows = slice(qs * SUB, (qs + 1) * SUB)
        krows = slice(qs * SUB, (qs + 1) * SUB + D_SPAN)
        lse_tile = jnp.zeros((SUB, LANES), jnp.float32)
        for p in range(D_HEADS // 2):
            cols = slice(p * LANES, (p + 1) * LANES)
            q = q_ref[rows, cols]
            kp = kbuf[krows, cols]
            vp = vbuf[krows, cols]
            zero = jnp.zeros_like(q)
            halves = []
            for e in range(2):
                hd = 2 * p + e
                qm = jnp.where((lane < HEAD_DIM) if e == 0 else (lane >= HEAD_DIM), q, zero)
                s = _dot_nt(qm, kp) + bias_ref[hd]
                if qs == 0:
                    s = s + halo_pen
                m = jnp.max(s, axis=-1, keepdims=True)
                pe = jnp.exp2(s - m)
                l = jnp.sum(pe, axis=-1, keepdims=True)
                halves.append(_dot(pe.astype(jnp.bfloat16), vp) / l)
                lse_tile = jnp.where(lane == hd, m + jnp.log2(l), lse_tile)
            o_ref[rows, cols] = jnp.where(lane < HEAD_DIM, halves[0], halves[1]).astype(jnp.bfloat16)
        lse_ref[rows, :] = lse_tile


def _dilated_branch(q, k, v, *, bsz, seq, dil):
    tl = TL
    length = seq // dil
    nb = length // tl
    halo_per_blk = tl // D_SPAN
    view = lambda a: a.reshape(bsz, length, dil * D_WIDTH)
    cur = lambda b, r, j: (b, j, r)
    halo = lambda b, r, j: (b, jnp.maximum(j * halo_per_blk - 1, 0), r)
    o, lse = pl.pallas_call(
        functools.partial(_dilated_kernel, tl=tl),
        grid=(bsz, dil, nb),
        in_specs=[
            pl.BlockSpec((None, tl, D_WIDTH), cur),
            pl.BlockSpec((None, D_SPAN, D_WIDTH), halo),
            pl.BlockSpec((None, tl, D_WIDTH), cur),
            pl.BlockSpec((None, D_SPAN, D_WIDTH), halo),
            pl.BlockSpec((None, tl, D_WIDTH), cur),
            pl.BlockSpec((D_HEADS, SUB, SUB + D_SPAN), lambda b, r, j: (0, 0, 0)),
        ],
        out_specs=[
            pl.BlockSpec((None, tl, D_WIDTH), cur),
            pl.BlockSpec((None, tl, LANES), cur),
        ],
        out_shape=[
            jax.ShapeDtypeStruct((bsz, length, dil * D_WIDTH), jnp.bfloat16),
            jax.ShapeDtypeStruct((bsz, length, dil * LANES), jnp.float32),
        ],
        scratch_shapes=[pltpu.VMEM((tl + D_SPAN, D_WIDTH), jnp.bfloat16)] * 2,
        compiler_params=pltpu.CompilerParams(
            dimension_semantics=("arbitrary",) * 3, vmem_limit_bytes=VMEM_LIMIT),
        name=f"dilated_attention_d{dil}",
    )(view(q), view(k), view(k), view(v), view(v), _dilated_bias_table(dil))
    return o.reshape(bsz * seq, D_WIDTH), lse.reshape(bsz * seq, LANES)


def _branch_combine_kernel(o1_ref, o2_ref, o3_ref, l1_ref, l2_ref, l3_ref, y_ref):
    lses = [l1_ref[...], l2_ref[...], l3_ref[...]]
    mx = jnp.maximum(jnp.maximum(lses[0], lses[1]), lses[2])
    es = [jnp.exp2(l - mx) for l in lses]
    inv = 1.0 / (es[0] + es[1] + es[2])
    ws = [e * inv for e in es]
    lane = lax.broadcasted_iota(jnp.int32, (1, LANES), 1)
    for p in range(D_HEADS // 2):
        cols = slice(p * LANES, (p + 1) * LANES)
        yd = None
        for w, o in zip(ws, (o1_ref, o2_ref, o3_ref)):
            wf = jnp.where(lane < HEAD_DIM, w[:, 2 * p:2 * p + 1], w[:, 2 * p + 1:2 * p + 2])
            term = wf * o[:, cols].astype(jnp.float32)
            yd = term if yd is None else yd + term
        y_ref[:, cols] = yd.astype(jnp.bfloat16)


def _dilated_by_branch(q, k, v, *, bsz, seq):
    outs, lses = zip(*[_dilated_branch(q, k, v, bsz=bsz, seq=seq, dil=d) for d in D_DILATIONS])
    tm = TM_PROJ
    row = lambda i: (i, 0)
    return pl.pallas_call(
        _branch_combine_kernel,
        grid=(bsz * seq // tm,),
        in_specs=[pl.BlockSpec((tm, D_WIDTH), row)] * 3 + [pl.BlockSpec((tm, LANES), row)] * 3,
        out_specs=pl.BlockSpec((tm, D_WIDTH), row),
        out_shape=jax.ShapeDtypeStruct((bsz * seq, D_WIDTH), jnp.bfloat16),
        compiler_params=pltpu.CompilerParams(
            dimension_semantics=("arbitrary",), vmem_limit_bytes=VMEM_LIMIT),
        name="dilated_branch_combine",
    )(*outs, *lses)


def kernel(x, norm_mix, norm_ffn, ev_w_in, ev_w_out, ev_q_gain, ev_k_gain, ev_lambda_q1, ev_lambda_k1, ev_lambda_q2, ev_lambda_k2, ev_subln_gain, ev_pool_w, ev_pool_scale, od_w_in, od_w_out, od_conv_w, od_q_gain, od_k_gain, ffn_w_gate, ffn_w_up, ffn_w_down):
    bsz, seq, _ = x.shape
    bf = jnp.bfloat16
    f32 = jnp.float32
    q_scale = HEAD_DIM ** -0.5 * LOG2E
    x2d = x.reshape(bsz * seq, D_MODEL)
    grp = lax.broadcasted_iota(jnp.int32, (MXU_DEPTH, MXU_DEPTH), 0) // HEAD_DIM
    gsum = (grp == grp.T).astype(bf)
    pw = ev_pool_w[0].astype(bf)
    zpw = jnp.zeros_like(pw[0])
    pool_w = jnp.stack([jnp.block([[pw[2 * i], zpw], [zpw, pw[2 * i + 1]]]) for i in range(2)])
    tile8 = lambda gain, s: (jnp.tile(gain.astype(f32), 8) * s).reshape(1, 512)

    w = ev_w_in[0]
    regroup = lambda c: c.reshape(D_MODEL, 2, A_HEADS, HEAD_DIM).transpose(0, 2, 1, 3).reshape(D_MODEL, 512)
    w_even = jnp.concatenate([regroup(w[:, 0:512]), regroup(w[:, 512:1024]), w[:, 1024:]], axis=1).astype(bf)
    q1, q2, k1, k2, v, yb = _even_in(
        x2d, norm_mix[0].reshape(1, D_MODEL).astype(f32), w_even, gsum,
        tile8(ev_q_gain[0], q_scale), tile8(ev_k_gain[0], 1.0),
        pool_w, ev_pool_scale[0].reshape(1, 512).astype(f32), seq=seq)
    lam_vecs = jnp.stack([ev_lambda_q1[0], ev_lambda_k1[0], ev_lambda_q2[0], ev_lambda_k2[0]]).astype(f32)
    attn = functools.partial(_diff_attn, q1, q2, k1, k2, v, lam_vecs,
                             ev_subln_gain[0].reshape(1, A_V).astype(f32), bsz=bsz, seq=seq)
    ya = lax.cond(_score_bound(ev_q_gain[0], ev_k_gain[0]) <= MAX_UNSHIFTED_SCORE,
                  lambda: attn(online=False), lambda: attn(online=True))
    ffn_w = (ffn_w_gate.astype(bf), ffn_w_up.astype(bf), ffn_w_down.astype(bf))
    x2d = _post(x2d, ya, yb, ev_w_out[0].astype(bf),
                norm_ffn[0].reshape(1, D_MODEL).astype(f32), *ffn_w, 0, "even_post")

    yc, q, k, v = _odd_in(
        x2d, norm_mix[1].reshape(1, D_MODEL).astype(f32), od_w_in[0].astype(bf), gsum,
        tile8(od_q_gain[0], q_scale), tile8(od_k_gain[0], 1.0), od_conv_w[0].astype(f32), seq=seq)
    yd = lax.cond(_score_bound(od_q_gain[0], od_k_gain[0]) <= MAX_UNSHIFTED_SCORE,
                  lambda: _dilated_fused(q, k, v, bsz=bsz, seq=seq),
                  lambda: _dilated_by_branch(q, k, v, bsz=bsz, seq=seq))
    x2d = _post(x2d, yc, yd, od_w_out[0].astype(bf),
                norm_ffn[1].reshape(1, D_MODEL).astype(f32), *ffn_w, 1, "odd_post")
    return x2d.reshape(bsz, seq, D_MODEL)
```

```python
import functools
import math

import numpy as np
import jax
import jax.numpy as jnp
from jax import lax
from jax.experimental import pallas as pl
from jax.experimental.pallas import tpu as pltpu

D_MODEL = 1024
HEAD_DIM = 64
EPS = 1e-6
A_HEADS = 4
A_V = 128
B_WINDOWS = (2, 4, 8, 16)
POOL_HALO = 16
CONV_HALO = 8
D_HEADS = 8
D_WIDTH = D_HEADS * HEAD_DIM
D_DILATIONS = (1, 4, 16)
D_SPAN = 128
D_FF = 2816
LAMBDA_INIT_L0 = 0.8 - 0.6 * math.exp(-0.3 * 0)
NEG = -1e30
LOG2E = math.log2(math.e)
LOG2E_HI = float(np.float32(LOG2E).astype(jnp.bfloat16))
LOG2E_LO = float(np.float32(LOG2E - LOG2E_HI).astype(jnp.bfloat16))
MAX_UNSHIFTED_SCORE = 60.0

LANES = 128
MXU_DEPTH = 256
VMEM_LIMIT = 56 * 1024 * 1024

TM_PROJ = 512
TM_IN = 1024
SM_IN = 512
TQ = 2048
TK = 512
TL = 512
SUB = 128
DBLK = 2048
GRP = 256
NG = DBLK // GRP

_NT = (((1,), (1,)), ((), ()))


def _dot(a, b):
    return jnp.dot(a, b, preferred_element_type=jnp.float32)


def _dot_nt(a, b):
    return lax.dot_general(a, b, _NT, preferred_element_type=jnp.float32)


def _rms_scale(x):
    return lax.rsqrt(jnp.mean(x * x, axis=-1, keepdims=True) + EPS)


def _group_rmsnorm(t, gsum_ref, gain_ref):
    sq = (t * t).astype(jnp.bfloat16)
    half = gsum_ref.shape[0]
    msq = jnp.concatenate([_dot(sq[:, :half], gsum_ref[...]), _dot(sq[:, half:], gsum_ref[...])],
                          axis=1)
    return t * lax.rsqrt(msq + EPS) * gain_ref[...]


def _alibi_slopes(n):
    return [2.0 ** (-8.0 * (i + 1) / n) for i in range(n)]


def _alibi_lanes(pos, lane, first):
    rel = lane - first
    hi = ((pos >> 8) << 8).astype(jnp.float32)
    lo = (pos & 255).astype(jnp.float32)
    pos_sel = jnp.where((rel & 1) == 0, hi, lo)
    l_sel = jnp.where((rel & 3) < 2, LOG2E_HI, LOG2E_LO)
    first4 = (rel >= 0) & (rel < 4)
    last4 = (rel >= 4) & (rel < 8)
    return (jnp.where(last4, pos_sel, 0.0), jnp.where(first4, l_sel, 0.0),
            jnp.where(first4, pos_sel, 0.0), jnp.where(last4, l_sel, 0.0))


def _shift_rows(ext, d):
    return pltpu.roll(ext, d, axis=0)


def _with_ones(v):
    return jnp.concatenate([v, jnp.ones_like(v)], axis=1)


def _score_bound(q_gain, k_gain):
    return (HEAD_DIM ** 0.5) * jnp.max(jnp.abs(q_gain)) * jnp.max(jnp.abs(k_gain))


def _even_in_kernel(x_ref, g_ref, w_ref, gsum_ref, qg_ref, kg_ref, pw_ref, ps_ref,
                    q1_ref, q2_ref, k1_ref, k2_ref, v_ref, yb_ref, carry_ref,
                    *, tm, sm, blocks_per_seq):
    blk = pl.program_id(0) % blocks_per_seq

    @pl.when(blk == 0)
    def _():
        carry_ref[...] = jnp.zeros_like(carry_ref)

    for sb in range(tm // sm):
        rows = slice(sb * sm, (sb + 1) * sm)
        x = x_ref[rows, :]
        h = (x * _rms_scale(x) * g_ref[...]).astype(jnp.bfloat16)
        proj = lambda part: _dot(h, w_ref[:, part * 512:(part + 1) * 512])
        qn = _group_rmsnorm(proj(0), gsum_ref, qg_ref)
        kn = _group_rmsnorm(proj(1), gsum_ref, kg_ref)
        v_ref[rows, :] = proj(2).astype(jnp.bfloat16)

        pos = lax.broadcasted_iota(jnp.int32, (sm, 1), 0) + (blk * tm + sb * sm)
        lane = lax.broadcasted_iota(jnp.int32, (1, LANES), 1)
        low = lane < HEAD_DIM
        for mp, (q_out, k_out) in enumerate(((q1_ref, k1_ref), (q2_ref, k2_ref))):
            q_pos, q_const, k_pos, k_const = _alibi_lanes(pos, lane, HEAD_DIM if mp == 0 else 0)
            data = low if mp == 0 else jnp.logical_not(low)
            for hd, slope in enumerate(_alibi_slopes(A_HEADS)):
                cols = slice(hd * LANES, (hd + 1) * LANES)
                q_out[rows, cols] = jnp.where(data, qn[:, cols], q_const - slope * q_pos).astype(jnp.bfloat16)
                k_out[rows, cols] = jnp.where(data, kn[:, cols], k_const + slope * k_pos).astype(jnp.bfloat16)

        u = proj(3)
        ext = jnp.concatenate([carry_ref[...], u], axis=0)
        pos1 = (pos + 1).astype(jnp.float32)
        pooled = []
        for g, w in enumerate(B_WINDOWS):
            cols = slice(g * LANES, (g + 1) * LANES)
            s = ext[:, cols]
            d = 1
            while d < w:
                s = s + _shift_rows(s, d)
                d *= 2
            pooled.append((s[POOL_HALO:, :] / jnp.minimum(pos1, float(w)) - u[:, cols]).astype(jnp.bfloat16))
        for pg in range(len(B_WINDOWS) // 2):
            cols = slice(2 * pg * LANES, (2 * pg + 2) * LANES)
            yb = _dot(jnp.concatenate(pooled[2 * pg:2 * pg + 2], axis=1), pw_ref[pg]) * ps_ref[:, cols]
            yb_ref[rows, cols] = yb.astype(jnp.bfloat16)
        carry_ref[...] = u[sm - POOL_HALO:, :]


def _even_in(x2d, g, w, gsum, qg, kg, pw, ps, *, seq):
    t = x2d.shape[0]
    tm = TM_IN
    const = lambda i: (0, 0)
    row = lambda i: (i, 0)
    out = jax.ShapeDtypeStruct((t, 512), jnp.bfloat16)
    return pl.pallas_call(
        functools.partial(_even_in_kernel, tm=tm, sm=SM_IN, blocks_per_seq=seq // tm),
        grid=(t // tm,),
        in_specs=[
            pl.BlockSpec((tm, D_MODEL), row),
            pl.BlockSpec((1, D_MODEL), const),
            pl.BlockSpec((D_MODEL, 2048), const),
            pl.BlockSpec((MXU_DEPTH, MXU_DEPTH), const),
            pl.BlockSpec((1, 512), const),
            pl.BlockSpec((1, 512), const),
            pl.BlockSpec((2, MXU_DEPTH, MXU_DEPTH), lambda i: (0, 0, 0)),
            pl.BlockSpec((1, 512), const),
        ],
        out_specs=[pl.BlockSpec((tm, 512), row)] * 6,
        out_shape=[out] * 6,
        scratch_shapes=[pltpu.VMEM((POOL_HALO, 512), jnp.float32)],
        compiler_params=pltpu.CompilerParams(
            dimension_semantics=("arbitrary",), vmem_limit_bytes=VMEM_LIMIT),
        name="even_in_proj",
    )(x2d, g, w, gsum, qg, kg, pw, ps)


def _diff_attn_kernel(q1_ref, q2_ref, k1_ref, k2_ref, v_ref, lam_ref, sg_ref, o_ref,
                      *, tq, tk, online):
    n = tq // tk
    qi = pl.program_id(2)
    qs = (q1_ref[...], q2_ref[...])
    k_refs = (k1_ref, k2_ref)

    def key_rows(j):
        return pl.ds(pl.multiple_of(j * tk, tk), tk)

    def scores(j, q_list, first_key=None):
        out = []
        for q, k_ref in zip(q_list, k_refs):
            s = _dot_nt(q, k_ref[key_rows(j), :])
            if first_key is not None:
                ii = lax.broadcasted_iota(jnp.int32, s.shape, 0)
                jj = lax.broadcasted_iota(jnp.int32, s.shape, 1)
                s = jnp.where(jj + first_key > ii, NEG, s)
            out.append(s)
        return out

    def pv(j, ps):
        v1 = _with_ones(v_ref[key_rows(j), :])
        return [_dot(p, v1) for p in ps]

    if online:
        def tile(j, state, first_key=None):
            out = []
            for mp, s in enumerate(scores(j, qs, first_key)):
                m_new = jnp.max(s, axis=-1, keepdims=True)
                if state is not None:
                    m_new = jnp.maximum(state[2 * mp], m_new)
                out += [m_new, jnp.exp2(s - m_new).astype(jnp.bfloat16)]
            accs = pv(j, out[1::2])
            if state is not None:
                accs = [jnp.exp2(state[2 * mp] - out[2 * mp]) * state[2 * mp + 1] + accs[mp]
                        for mp in range(2)]
            return (out[0], accs[0], out[2], accs[1])

        state = None
        for t in range(n):
            state = tile(n * qi + t, state, t * tk)
        _, acc1, _, acc2 = lax.fori_loop(0, n * qi, tile, state)
    else:
        def tile(j, q_list, first_key=None):
            return pv(j, [jnp.exp2(s).astype(jnp.bfloat16) for s in scores(j, q_list, first_key)])

        state = tile(n * qi, qs, 0)
        for t in range(1, n):
            part = tile(n * qi + t, [q[t * tk:] for q in qs], 0)
            state = tuple(jnp.concatenate([a[:t * tk], a[t * tk:] + b], axis=0)
                          for a, b in zip(state, part))

        def n_tiles(i, state):
            for t in range(n):
                state = tuple(a + b for a, b in zip(state, tile(n * i + t, qs)))
            return state

        acc1, acc2 = lax.fori_loop(0, qi, n_tiles, state)

    lv = lam_ref[...]
    lam = (jnp.exp(jnp.sum(lv[0:1] * lv[1:2], axis=-1, keepdims=True))
           - jnp.exp(jnp.sum(lv[2:3] * lv[3:4], axis=-1, keepdims=True)) + LAMBDA_INIT_L0)
    o = acc1[:, :A_V] / acc1[:, A_V:] - lam * (acc2[:, :A_V] / acc2[:, A_V:])
    o = o * _rms_scale(o) * sg_ref[...] * (1.0 - LAMBDA_INIT_L0)
    o_ref[...] = o.astype(jnp.bfloat16)


def _diff_attn(q1, q2, k1, k2, v, lam_vecs, subln_gain, *, bsz, seq, online):
    tq = TQ
    nq = seq // tq
    q_spec = pl.BlockSpec((tq, LANES), lambda b, h, i: (b * nq + i, h))
    kv_spec = pl.BlockSpec((seq, LANES), lambda b, h, i: (b, h))
    return pl.pallas_call(
        functools.partial(_diff_attn_kernel, tq=tq, tk=TK, online=online),
        grid=(bsz, A_HEADS, nq),
        in_specs=[
            q_spec, q_spec, kv_spec, kv_spec, kv_spec,
            pl.BlockSpec((4, HEAD_DIM), lambda b, h, i: (0, 0)),
            pl.BlockSpec((1, A_V), lambda b, h, i: (0, 0)),
        ],
        out_specs=q_spec,
        out_shape=jax.ShapeDtypeStruct((bsz * seq, A_HEADS * A_V), jnp.bfloat16),
        compiler_params=pltpu.CompilerParams(
            dimension_semantics=("arbitrary",) * 3, vmem_limit_bytes=VMEM_LIMIT),
        name="diff_attention_online" if online else "diff_attention",
    )(q1, q2, k1, k2, v, lam_vecs, subln_gain)


def _post_kernel(x_ref, ya_ref, yb_ref, wo_ref, g_ref, wg_ref, wu_ref, wd_ref, o_ref):
    half = ya_ref.shape[1]
    x1 = x_ref[...] + _dot(ya_ref[...], wo_ref[0:half, :]) + _dot(yb_ref[...], wo_ref[half:, :])
    h = (x1 * _rms_scale(x1) * g_ref[...]).astype(jnp.bfloat16)
    gate = _dot(h, wg_ref[...])
    up = _dot(h, wu_ref[...])
    act = (gate * jax.nn.sigmoid(gate) * up).astype(jnp.bfloat16)
    o_ref[...] = x1 + _dot(act, wd_ref[...])


def _post(x2d, ya, yb, wo, g, wg, wu, wd, layer, name):
    t = x2d.shape[0]
    tm = TM_PROJ
    row = lambda i: (i, 0)
    const = lambda i: (0, 0)
    resident = functools.partial(pl.BlockSpec, index_map=const, pipeline_mode=pl.Buffered(1))
    ffn_w = lambda rows, cols: pl.BlockSpec((None, rows, cols), lambda i: (layer, 0, 0),
                                            pipeline_mode=pl.Buffered(1))
    return pl.pallas_call(
        _post_kernel,
        grid=(t // tm,),
        in_specs=[
            pl.BlockSpec((tm, D_MODEL), row),
            pl.BlockSpec((tm, ya.shape[1]), row),
            pl.BlockSpec((tm, yb.shape[1]), row),
            resident((D_MODEL, D_MODEL)),
            pl.BlockSpec((1, D_MODEL), const),
            ffn_w(D_MODEL, D_FF),
            ffn_w(D_MODEL, D_FF),
            ffn_w(D_FF, D_MODEL),
        ],
        out_specs=pl.BlockSpec((tm, D_MODEL), row),
        out_shape=jax.ShapeDtypeStruct((t, D_MODEL), jnp.float32),
        compiler_params=pltpu.CompilerParams(
            dimension_semantics=("arbitrary",), vmem_limit_bytes=VMEM_LIMIT),
        name=name,
    )(x2d, ya, yb, wo, g, wg, wu, wd)


def _odd_in_kernel(x_ref, g_ref, w_ref, gsum_ref, qg_ref, kg_ref, cw_ref,
                   yc_ref, q_ref, k_ref, v_ref, carry_ref, *, tm, sm, blocks_per_seq):
    blk = pl.program_id(0) % blocks_per_seq

    @pl.when(blk == 0)
    def _():
        carry_ref[...] = jnp.zeros_like(carry_ref)

    for sb in range(tm // sm):
        rows = slice(sb * sm, (sb + 1) * sm)
        x = x_ref[rows, :]
        h = (x * _rms_scale(x) * g_ref[...]).astype(jnp.bfloat16)
        proj = lambda part: _dot(h, w_ref[:, part * 512:(part + 1) * 512])
        z = proj(1) * proj(2)
        bg = proj(0)
        ext = jnp.concatenate([carry_ref[...], z], axis=0)
        z1 = _shift_rows(ext, 1)[CONV_HALO:, :]
        z2 = _shift_rows(ext, 2)[CONV_HALO:, :]
        cw = cw_ref[...]
        yc = bg * (cw[0:1] * z2 + cw[1:2] * z1 + cw[2:3] * z)
        yc_ref[rows, :] = yc.astype(jnp.bfloat16)
        carry_ref[...] = z[sm - CONV_HALO:, :]

        q_ref[rows, :] = _group_rmsnorm(proj(3), gsum_ref, qg_ref).astype(jnp.bfloat16)
        k_ref[rows, :] = _group_rmsnorm(proj(4), gsum_ref, kg_ref).astype(jnp.bfloat16)
        v_ref[rows, :] = proj(5).astype(jnp.bfloat16)


def _odd_in(x2d, g, w, gsum, qg, kg, cw, *, seq):
    t = x2d.shape[0]
    tm = TM_IN
    const = lambda i: (0, 0)
    row = lambda i: (i, 0)
    out = jax.ShapeDtypeStruct((t, 512), jnp.bfloat16)
    return pl.pallas_call(
        functools.partial(_odd_in_kernel, tm=tm, sm=SM_IN, blocks_per_seq=seq // tm),
        grid=(t // tm,),
        in_specs=[
            pl.BlockSpec((tm, D_MODEL), row),
            pl.BlockSpec((1, D_MODEL), const),
            pl.BlockSpec((D_MODEL, 3072), const),
            pl.BlockSpec((MXU_DEPTH, MXU_DEPTH), const),
            pl.BlockSpec((1, 512), const),
            pl.BlockSpec((1, 512), const),
            pl.BlockSpec((3, 512), const),
        ],
        out_specs=[pl.BlockSpec((tm, 512), row)] * 4,
        out_shape=[out] * 4,
        scratch_shapes=[pltpu.VMEM((CONV_HALO, 512), jnp.float32)],
        compiler_params=pltpu.CompilerParams(
            dimension_semantics=("arbitrary",), vmem_limit_bytes=VMEM_LIMIT),
        name="odd_in_proj",
    )(x2d, g, w, gsum, qg, kg, cw)


def _pair_tile(q, kp, vp, bias):
    lane = lax.broadcasted_iota(jnp.int32, (1, LANES), 1)
    low = lane < HEAD_DIM
    zero = jnp.zeros_like(q)
    q2 = jnp.concatenate([jnp.where(low, q, zero), jnp.where(low, zero, q)], axis=0)
    p = jnp.exp2(_dot_nt(q2, kp) + bias).astype(jnp.bfloat16)
    r = _dot(p, _with_ones(vp))
    num = jnp.where(low, r[:SUB, :LANES], r[SUB:, :LANES])
    den = jnp.where(low, r[:SUB, LANES:], r[SUB:, LANES:])
    return num, den


def _dilated_fused_kernel(q_ref, k_ref, v_ref, perm_ref, tab_ref, o_ref,
                          kn, vn, k4, v4, k16, v16, qp, res4, res16):
    blk = pl.program_id(1)
    first = blk == 0
    cur = (blk & 1) * NG
    prev = NG - cur
    bf = jnp.bfloat16

    @pl.when(first)
    def _():
        kn[0:D_SPAN, :] = jnp.zeros((D_SPAN, D_WIDTH), bf)
        vn[0:D_SPAN, :] = jnp.zeros((D_SPAN, D_WIDTH), bf)
        k4[0:2] = jnp.zeros((2, GRP, D_WIDTH), bf)
        v4[0:2] = jnp.zeros((2, GRP, D_WIDTH), bf)
        k16[pl.ds(prev, NG)] = jnp.zeros((NG, GRP, D_WIDTH), bf)
        v16[pl.ds(prev, NG)] = jnp.zeros((NG, GRP, D_WIDTH), bf)

    @pl.when(jnp.logical_not(first))
    def _():
        kn[0:D_SPAN, :] = kn[DBLK:DBLK + D_SPAN, :]
        vn[0:D_SPAN, :] = vn[DBLK:DBLK + D_SPAN, :]
        k4[0:2] = k4[NG:NG + 2]
        v4[0:2] = v4[NG:NG + 2]

    kn[D_SPAN:, :] = k_ref[...]
    vn[D_SPAN:, :] = v_ref[...]
    for g in range(NG):
        rows = slice(g * GRP, (g + 1) * GRP)
        for src, d4, d16 in ((k_ref, k4, k16), (v_ref, v4, v16)):
            x = src[rows, :]
            d4[2 + g] = _dot(perm_ref[0], x).astype(bf)
            d16[cur + g] = _dot(perm_ref[2], x).astype(bf)

    kk = lax.broadcasted_iota(jnp.int32, (1, 2 * SUB), 1)
    no_halo = jnp.where(kk < D_SPAN, jnp.where(first, NEG, 0.0), 0.0)

    def pair_cols(p):
        return slice(p * LANES, (p + 1) * LANES)

    for g in range(NG):
        qp[g] = _dot(perm_ref[2], q_ref[g * GRP:(g + 1) * GRP, :]).astype(bf)

    def classes16(i, carry):
        for t in range(4):
            r = pl.ds(pl.multiple_of(i * 64 + t * 16, 16), 16)
            for p in range(D_HEADS // 2):
                cols = pair_cols(p)
                q = jnp.concatenate([qp[g, r, cols] for g in range(NG)], axis=0)
                kp = jnp.concatenate([k16[prev + g, r, cols] for g in range(NG)]
                                     + [k16[cur + g, r, cols] for g in range(NG)], axis=0)
                vp = jnp.concatenate([v16[prev + g, r, cols] for g in range(NG)]
                                     + [v16[cur + g, r, cols] for g in range(NG)], axis=0)
                num, den = _pair_tile(q, kp, vp, tab_ref[2, p] + no_halo)
                for g in range(NG):
                    res16[0, g, r, cols] = num[16 * g:16 * g + 16].astype(bf)
                    res16[1, g, r, cols] = den[16 * g:16 * g + 16].astype(bf)
        return carry

    lax.fori_loop(0, 4, classes16, 0)

    for g in range(NG):
        qp[g] = _dot(perm_ref[0], q_ref[g * GRP:(g + 1) * GRP, :]).astype(bf)

    def class4(c, carry):
        r = pl.ds(pl.multiple_of(c * 64, 64), 64)
        for qs in range(NG // 2):
            for p in range(D_HEADS // 2):
                cols = pair_cols(p)
                q = jnp.concatenate([qp[2 * qs + i, r, cols] for i in range(2)], axis=0)
                kp = jnp.concatenate([k4[2 * qs + i, r, cols] for i in range(4)], axis=0)
                vp = jnp.concatenate([v4[2 * qs + i, r, cols] for i in range(4)], axis=0)
                bias = tab_ref[1, p] + no_halo if qs == 0 else tab_ref[1, p]
                num, den = _pair_tile(q, kp, vp, bias)
                for i in range(2):
                    res4[0, 2 * qs + i, r, cols] = num[64 * i:64 * i + 64].astype(bf)
                    res4[1, 2 * qs + i, r, cols] = den[64 * i:64 * i + 64].astype(bf)
        return carry

    lax.fori_loop(0, 4, class4, 0)

    def group(g, carry):
        num = _dot(perm_ref[1], res4[0, g]) + _dot(perm_ref[3], res16[0, g])
        den = _dot(perm_ref[1], res4[1, g]) + _dot(perm_ref[3], res16[1, g])
        pen = jnp.where(g == 0, no_halo, 0.0)
        for t in range(GRP // SUB):
            q_rows = pl.ds(pl.multiple_of(g * GRP + t * SUB, SUB), SUB)
            k_rows = pl.ds(pl.multiple_of(g * GRP + t * SUB, SUB), 2 * SUB)
            sub = slice(t * SUB, (t + 1) * SUB)
            for p in range(D_HEADS // 2):
                cols = pair_cols(p)
                bias = tab_ref[0, p] + pen if t == 0 else tab_ref[0, p]
                n1, d1 = _pair_tile(q_ref[q_rows, cols], kn[k_rows, cols], vn[k_rows, cols], bias)
                o_ref[q_rows, cols] = ((n1 + num[sub, cols]) / (d1 + den[sub, cols])).astype(bf)
        return carry

    lax.fori_loop(0, NG, group, 0)


def _class_permutations():
    mats = []
    out = np.arange(GRP)
    for d in (4, 16):
        per = GRP // d
        p = np.zeros((GRP, GRP), np.float32)
        p[out, d * (out % per) + out // per] = 1.0
        mats += [p, p.T]
    return jnp.asarray(np.stack(mats), jnp.bfloat16)


def _dilated_bias_table(dil):
    ii = lax.broadcasted_iota(jnp.int32, (SUB, SUB + D_SPAN), 0)
    kk = lax.broadcasted_iota(jnp.int32, (SUB, SUB + D_SPAN), 1)
    delta = D_SPAN + ii - kk
    slopes = jnp.asarray(_alibi_slopes(D_HEADS), jnp.float32)
    bias = (-LOG2E) * slopes[:, None, None] * (delta * dil).astype(jnp.float32)[None]
    return jnp.where(((delta >= 0) & (delta <= D_SPAN))[None], bias, NEG)


def _dilated_fused(q, k, v, *, bsz, seq):
    nblk = seq // DBLK
    tabs = jnp.stack([_dilated_bias_table(d).reshape(D_HEADS // 2, 2 * SUB, SUB + D_SPAN)
                      for d in D_DILATIONS])
    blk_spec = pl.BlockSpec((DBLK, D_WIDTH), lambda b, i: (b * nblk + i, 0))
    group_buf = lambda n: pltpu.VMEM((n, GRP, D_WIDTH), jnp.bfloat16)
    return pl.pallas_call(
        _dilated_fused_kernel,
        grid=(bsz, nblk),
        in_specs=[
            blk_spec, blk_spec, blk_spec,
            pl.BlockSpec((4, GRP, GRP), lambda b, i: (0, 0, 0), pipeline_mode=pl.Buffered(1)),
            pl.BlockSpec(tabs.shape, lambda b, i: (0, 0, 0, 0), pipeline_mode=pl.Buffered(1)),
        ],
        out_specs=blk_spec,
        out_shape=jax.ShapeDtypeStruct((bsz * seq, D_WIDTH), jnp.bfloat16),
        scratch_shapes=[
            pltpu.VMEM((D_SPAN + DBLK, D_WIDTH), jnp.bfloat16),
            pltpu.VMEM((D_SPAN + DBLK, D_WIDTH), jnp.bfloat16),
            group_buf(2 + NG), group_buf(2 + NG),
            group_buf(2 * NG), group_buf(2 * NG),
            group_buf(NG),
            pltpu.VMEM((2, NG, GRP, D_WIDTH), jnp.bfloat16),
            pltpu.VMEM((2, NG, GRP, D_WIDTH), jnp.bfloat16),
        ],
        compiler_params=pltpu.CompilerParams(
            dimension_semantics=("arbitrary",) * 2, vmem_limit_bytes=VMEM_LIMIT),
        name="dilated_attention_fused",
    )(q, k, v, _class_permutations(), tabs)


def _dilated_kernel(q_ref, kh_ref, kc_ref, vh_ref, vc_ref, bias_ref, o_ref, lse_ref,
                    kbuf, vbuf, *, tl):
    j = pl.program_id(2)
    kbuf[0:D_SPAN, :] = kh_ref[...]
    kbuf[D_SPAN:, :] = kc_ref[...]
    vbuf[0:D_SPAN, :] = vh_ref[...]
    vbuf[D_SPAN:, :] = vc_ref[...]
    lane = lax.broadcasted_iota(jnp.int32, (1, LANES), 1)
    kk = lax.broadcasted_iota(jnp.int32, (1, SUB + D_SPAN), 1)
    halo_pen = jnp.where(kk < D_SPAN, jnp.where(j == 0, NEG, 0.0), 0.0)
    for qs in range(tl // SUB):
        rows = slice(qs * SUB, (qs + 1) * SUB)
        krows = slice(qs * SUB, (qs + 1) * SUB + D_SPAN)
        lse_tile = jnp.zeros((SUB, LANES), jnp.float32)
        for p in range(D_HEADS // 2):
            cols = slice(p * LANES, (p + 1) * LANES)
            q = q_ref[rows, cols]
            kp = kbuf[krows, cols]
            vp = vbuf[krows, cols]
            zero = jnp.zeros_like(q)
            halves = []
            for e in range(2):
                hd = 2 * p + e
                qm = jnp.where((lane < HEAD_DIM) if e == 0 else (lane >= HEAD_DIM), q, zero)
                s = _dot_nt(qm, kp) + bias_ref[hd]
                if qs == 0:
                    s = s + halo_pen
                m = jnp.max(s, axis=-1, keepdims=True)
                pe = jnp.exp2(s - m)
                l = jnp.sum(pe, axis=-1, keepdims=True)
                halves.append(_dot(pe.astype(jnp.bfloat16), vp) / l)
                lse_tile = jnp.where(lane == hd, m + jnp.log2(l), lse_tile)
            o_ref[rows, cols] = jnp.where(lane < HEAD_DIM, halves[0], halves[1]).astype(jnp.bfloat16)
        lse_ref[rows, :] = lse_tile


def _dilated_branch(q, k, v, *, bsz, seq, dil):
    tl = TL
    length = seq // dil
    nb = length // tl
    halo_per_blk = tl // D_SPAN
    view = lambda a: a.reshape(bsz, length, dil * D_WIDTH)
    cur = lambda b, r, j: (b, j, r)
    halo = lambda b, r, j: (b, jnp.maximum(j * halo_per_blk - 1, 0), r)
    o, lse = pl.pallas_call(
        functools.partial(_dilated_kernel, tl=tl),
        grid=(bsz, dil, nb),
        in_specs=[
            pl.BlockSpec((None, tl, D_WIDTH), cur),
            pl.BlockSpec((None, D_SPAN, D_WIDTH), halo),
            pl.BlockSpec((None, tl, D_WIDTH), cur),
            pl.BlockSpec((None, D_SPAN, D_WIDTH), halo),
            pl.BlockSpec((None, tl, D_WIDTH), cur),
            pl.BlockSpec((D_HEADS, SUB, SUB + D_SPAN), lambda b, r, j: (0, 0, 0)),
        ],
        out_specs=[
            pl.BlockSpec((None, tl, D_WIDTH), cur),
            pl.BlockSpec((None, tl, LANES), cur),
        ],
        out_shape=[
            jax.ShapeDtypeStruct((bsz, length, dil * D_WIDTH), jnp.bfloat16),
            jax.ShapeDtypeStruct((bsz, length, dil * LANES), jnp.float32),
        ],
        scratch_shapes=[pltpu.VMEM((tl + D_SPAN, D_WIDTH), jnp.bfloat16)] * 2,
        compiler_params=pltpu.CompilerParams(
            dimension_semantics=("arbitrary",) * 3, vmem_limit_bytes=VMEM_LIMIT),
        name=f"dilated_attention_d{dil}",
    )(view(q), view(k), view(k), view(v), view(v), _dilated_bias_table(dil))
    return o.reshape(bsz * seq, D_WIDTH), lse.reshape(bsz * seq, LANES)


def _branch_combine_kernel(o1_ref, o2_ref, o3_ref, l1_ref, l2_ref, l3_ref, y_ref):
    lses = [l1_ref[...], l2_ref[...], l3_ref[...]]
    mx = jnp.maximum(jnp.maximum(lses[0], lses[1]), lses[2])
    es = [jnp.exp2(l - mx) for l in lses]
    inv = 1.0 / (es[0] + es[1] + es[2])
    ws = [e * inv for e in es]
    lane = lax.broadcasted_iota(jnp.int32, (1, LANES), 1)
    for p in range(D_HEADS // 2):
        cols = slice(p * LANES, (p + 1) * LANES)
        yd = None
        for w, o in zip(ws, (o1_ref, o2_ref, o3_ref)):
            wf = jnp.where(lane < HEAD_DIM, w[:, 2 * p:2 * p + 1], w[:, 2 * p + 1:2 * p + 2])
            term = wf * o[:, cols].astype(jnp.float32)
            yd = term if yd is None else yd + term
        y_ref[:, cols] = yd.astype(jnp.bfloat16)


def _dilated_by_branch(q, k, v, *, bsz, seq):
    outs, lses = zip(*[_dilated_branch(q, k, v, bsz=bsz, seq=seq, dil=d) for d in D_DILATIONS])
    tm = TM_PROJ
    row = lambda i: (i, 0)
    return pl.pallas_call(
        _branch_combine_kernel,
        grid=(bsz * seq // tm,),
        in_specs=[pl.BlockSpec((tm, D_WIDTH), row)] * 3 + [pl.BlockSpec((tm, LANES), row)] * 3,
        out_specs=pl.BlockSpec((tm, D_WIDTH), row),
        out_shape=jax.ShapeDtypeStruct((bsz * seq, D_WIDTH), jnp.bfloat16),
        compiler_params=pltpu.CompilerParams(
            dimension_semantics=("arbitrary",), vmem_limit_bytes=VMEM_LIMIT),
        name="dilated_branch_combine",
    )(*outs, *lses)


def kernel(x, norm_mix, norm_ffn, ev_w_in, ev_w_out, ev_q_gain, ev_k_gain, ev_lambda_q1, ev_lambda_k1, ev_lambda_q2, ev_lambda_k2, ev_subln_gain, ev_pool_w, ev_pool_scale, od_w_in, od_w_out, od_conv_w, od_q_gain, od_k_gain, ffn_w_gate, ffn_w_up, ffn_w_down):
    bsz, seq, _ = x.shape
    bf = jnp.bfloat16
    f32 = jnp.float32
    q_scale = HEAD_DIM ** -0.5 * LOG2E
    x2d = x.reshape(bsz * seq, D_MODEL)
    grp = lax.broadcasted_iota(jnp.int32, (MXU_DEPTH, MXU_DEPTH), 0) // HEAD_DIM
    gsum = jnp.where(grp == grp.T, 1.0 / HEAD_DIM, 0.0).astype(bf)
    pw = ev_pool_w[0].astype(bf)
    zpw = jnp.zeros_like(pw[0])
    pool_w = jnp.stack([jnp.block([[pw[2 * i], zpw], [zpw, pw[2 * i + 1]]]) for i in range(2)])
    tile8 = lambda gain, s: (jnp.tile(gain.astype(f32), 8) * s).reshape(1, 512)

    w = ev_w_in[0]
    regroup = lambda c: c.reshape(D_MODEL, 2, A_HEADS, HEAD_DIM).transpose(0, 2, 1, 3).reshape(D_MODEL, 512)
    w_even = jnp.concatenate([regroup(w[:, 0:512]), regroup(w[:, 512:1024]), w[:, 1024:]], axis=1).astype(bf)
    q1, q2, k1, k2, v, yb = _even_in(
        x2d, norm_mix[0].reshape(1, D_MODEL).astype(f32), w_even, gsum,
        tile8(ev_q_gain[0], q_scale), tile8(ev_k_gain[0], 1.0),
        pool_w, ev_pool_scale[0].reshape(1, 512).astype(f32), seq=seq)
    lam_vecs = jnp.stack([ev_lambda_q1[0], ev_lambda_k1[0], ev_lambda_q2[0], ev_lambda_k2[0]]).astype(f32)
    attn = functools.partial(_diff_attn, q1, q2, k1, k2, v, lam_vecs,
                             ev_subln_gain[0].reshape(1, A_V).astype(f32), bsz=bsz, seq=seq)
    ya = lax.cond(_score_bound(ev_q_gain[0], ev_k_gain[0]) <= MAX_UNSHIFTED_SCORE,
                  lambda: attn(online=False), lambda: attn(online=True))
    ffn_w = (ffn_w_gate.astype(bf), ffn_w_up.astype(bf), ffn_w_down.astype(bf))
    x2d = _post(x2d, ya, yb, ev_w_out[0].astype(bf),
                norm_ffn[0].reshape(1, D_MODEL).astype(f32), *ffn_w, 0, "even_post")

    yc, q, k, v = _odd_in(
        x2d, norm_mix[1].reshape(1, D_MODEL).astype(f32), od_w_in[0].astype(bf), gsum,
        tile8(od_q_gain[0], q_scale), tile8(od_k_gain[0], 1.0), od_conv_w[0].astype(f32), seq=seq)
    yd = lax.cond(_score_bound(od_q_gain[0], od_k_gain[0]) <= MAX_UNSHIFTED_SCORE,
                  lambda: _dilated_fused(q, k, v, bsz=bsz, seq=seq),
                  lambda: _dilated_by_branch(q, k, v, bsz=bsz, seq=seq))
    x2d = _post(x2d, yc, yd, od_w_out[0].astype(bf),
                norm_ffn[1].reshape(1, D_MODEL).astype(f32), *ffn_w, 1, "odd_post")
    return x2d.reshape(bsz, seq, D_MODEL)
```

```python
import functools
import math

import numpy as np
import jax
import jax.numpy as jnp
from jax import lax
from jax.experimental import pallas as pl
from jax.experimental.pallas import tpu as pltpu

D_MODEL = 1024
HEAD_DIM = 64
EPS = 1e-6
A_HEADS = 4
A_V = 128
B_WINDOWS = (2, 4, 8, 16)
POOL_HALO = 16
CONV_HALO = 8
D_HEADS = 8
D_WIDTH = D_HEADS * HEAD_DIM
D_DILATIONS = (1, 4, 16)
D_SPAN = 128
D_FF = 2816
LAMBDA_INIT_L0 = 0.8 - 0.6 * math.exp(-0.3 * 0)
NEG = -1e30
LOG2E = math.log2(math.e)
LOG2E_HI = float(np.float32(LOG2E).astype(jnp.bfloat16))
LOG2E_LO = float(np.float32(LOG2E - LOG2E_HI).astype(jnp.bfloat16))
MAX_UNSHIFTED_SCORE = 60.0

LANES = 128
MXU_DEPTH = 256
VMEM_LIMIT = 56 * 1024 * 1024

TM_PROJ = 512
TM_IN = 1024
SM_IN = 512
TQ = 2048
TK = 512
TL = 512
SUB = 128
DBLK = 2048
GRP = 256
NG = DBLK // GRP

_NT = (((1,), (1,)), ((), ()))


def _dot(a, b):
    return jnp.dot(a, b, preferred_element_type=jnp.float32)


def _dot_nt(a, b):
    return lax.dot_general(a, b, _NT, preferred_element_type=jnp.float32)


def _rms_scale(x):
    return lax.rsqrt(jnp.mean(x * x, axis=-1, keepdims=True) + EPS)


def _group_rmsnorm(t, gsum_ref, gain_ref):
    sq = (t * t).astype(jnp.bfloat16)
    half = gsum_ref.shape[0]
    msq = jnp.concatenate([_dot(sq[:, :half], gsum_ref[...]), _dot(sq[:, half:], gsum_ref[...])],
                          axis=1)
    return t * lax.rsqrt(msq + EPS) * gain_ref[...]


def _alibi_slopes(n):
    return [2.0 ** (-8.0 * (i + 1) / n) for i in range(n)]


def _alibi_lanes(pos, lane, first):
    rel = lane - first
    hi = ((pos >> 8) << 8).astype(jnp.float32)
    lo = (pos & 255).astype(jnp.float32)
    pos_sel = jnp.where((rel & 1) == 0, hi, lo)
    l_sel = jnp.where((rel & 3) < 2, LOG2E_HI, LOG2E_LO)
    first4 = (rel >= 0) & (rel < 4)
    last4 = (rel >= 4) & (rel < 8)
    return (jnp.where(last4, pos_sel, 0.0), jnp.where(first4, l_sel, 0.0),
            jnp.where(first4, pos_sel, 0.0), jnp.where(last4, l_sel, 0.0))


def _shift_rows(ext, d):
    return pltpu.roll(ext, d, axis=0)


def _with_ones(v):
    return jnp.concatenate([v, jnp.ones_like(v)], axis=1)


def _score_bound(q_gain, k_gain):
    return (HEAD_DIM ** 0.5) * jnp.max(jnp.abs(q_gain)) * jnp.max(jnp.abs(k_gain))


def _even_in_kernel(x_ref, g_ref, w_ref, gsum_ref, qg_ref, kg_ref, pw_ref, ps_ref,
                    q1_ref, q2_ref, k1_ref, k2_ref, v_ref, yb_ref, carry_ref,
                    *, tm, sm, blocks_per_seq):
    blk = pl.program_id(0) % blocks_per_seq

    @pl.when(blk == 0)
    def _():
        carry_ref[...] = jnp.zeros_like(carry_ref)

    for sb in range(tm // sm):
        rows = slice(sb * sm, (sb + 1) * sm)
        x = x_ref[rows, :]
        h = (x * _rms_scale(x) * g_ref[...]).astype(jnp.bfloat16)
        proj = lambda part: _dot(h, w_ref[:, part * 512:(part + 1) * 512])
        qn = _group_rmsnorm(proj(0), gsum_ref, qg_ref)
        kn = _group_rmsnorm(proj(1), gsum_ref, kg_ref)
        v_ref[rows, :] = proj(2).astype(jnp.bfloat16)

        pos = lax.broadcasted_iota(jnp.int32, (sm, 1), 0) + (blk * tm + sb * sm)
        lane = lax.broadcasted_iota(jnp.int32, (1, LANES), 1)
        low = lane < HEAD_DIM
        for mp, (q_out, k_out) in enumerate(((q1_ref, k1_ref), (q2_ref, k2_ref))):
            q_pos, q_const, k_pos, k_const = _alibi_lanes(pos, lane, HEAD_DIM if mp == 0 else 0)
            data = low if mp == 0 else jnp.logical_not(low)
            for hd, slope in enumerate(_alibi_slopes(A_HEADS)):
                cols = slice(hd * LANES, (hd + 1) * LANES)
                q_out[rows, cols] = jnp.where(data, qn[:, cols], q_const - slope * q_pos).astype(jnp.bfloat16)
                k_out[rows, cols] = jnp.where(data, kn[:, cols], k_const + slope * k_pos).astype(jnp.bfloat16)

        u = proj(3)
        ext = jnp.concatenate([carry_ref[...], u], axis=0)
        pos1 = (pos + 1).astype(jnp.float32)
        pooled = []
        for g, w in enumerate(B_WINDOWS):
            cols = slice(g * LANES, (g + 1) * LANES)
            s = ext[:, cols]
            d = 1
            while d < w:
                s = s + _shift_rows(s, d)
                d *= 2
            pooled.append((s[POOL_HALO:, :] / jnp.minimum(pos1, float(w)) - u[:, cols]).astype(jnp.bfloat16))
        for pg in range(len(B_WINDOWS) // 2):
            cols = slice(2 * pg * LANES, (2 * pg + 2) * LANES)
            yb = _dot(jnp.concatenate(pooled[2 * pg:2 * pg + 2], axis=1), pw_ref[pg]) * ps_ref[:, cols]
            yb_ref[rows, cols] = yb.astype(jnp.bfloat16)
        carry_ref[...] = u[sm - POOL_HALO:, :]


def _even_in(x2d, g, w, gsum, qg, kg, pw, ps, *, seq):
    t = x2d.shape[0]
    tm = TM_IN
    const = lambda i: (0, 0)
    row = lambda i: (i, 0)
    out = jax.ShapeDtypeStruct((t, 512), jnp.bfloat16)
    return pl.pallas_call(
        functools.partial(_even_in_kernel, tm=tm, sm=SM_IN, blocks_per_seq=seq // tm),
        grid=(t // tm,),
        in_specs=[
            pl.BlockSpec((tm, D_MODEL), row),
            pl.BlockSpec((1, D_MODEL), const),
            pl.BlockSpec((D_MODEL, 2048), const),
            pl.BlockSpec((MXU_DEPTH, MXU_DEPTH), const),
            pl.BlockSpec((1, 512), const),
            pl.BlockSpec((1, 512), const),
            pl.BlockSpec((2, MXU_DEPTH, MXU_DEPTH), lambda i: (0, 0, 0)),
            pl.BlockSpec((1, 512), const),
        ],
        out_specs=[pl.BlockSpec((tm, 512), row)] * 6,
        out_shape=[out] * 6,
        scratch_shapes=[pltpu.VMEM((POOL_HALO, 512), jnp.float32)],
        compiler_params=pltpu.CompilerParams(
            dimension_semantics=("arbitrary",), vmem_limit_bytes=VMEM_LIMIT),
        name="even_in_proj",
    )(x2d, g, w, gsum, qg, kg, pw, ps)


def _diff_attn_kernel(q1_ref, q2_ref, k1_ref, k2_ref, v_ref, lam_ref, o_ref, acc_ref,
                      *, tq, tk, online):
    n = tq // tk
    qi = pl.program_id(2)
    qs = (q1_ref[...], q2_ref[...])
    k_refs = (k1_ref, k2_ref)

    def key_rows(j):
        return pl.ds(pl.multiple_of(j * tk, tk), tk)

    def scores(j, q_list, first_key=None):
        out = []
        for q, k_ref in zip(q_list, k_refs):
            s = _dot_nt(q, k_ref[key_rows(j), :])
            if first_key is not None:
                ii = lax.broadcasted_iota(jnp.int32, s.shape, 0)
                jj = lax.broadcasted_iota(jnp.int32, s.shape, 1)
                s = jnp.where(jj + first_key > ii, NEG, s)
            out.append(s)
        return out

    def pv(j, ps):
        v1 = _with_ones(v_ref[key_rows(j), :])
        return [_dot(p, v1) for p in ps]

    if online:
        def tile(j, state, first_key=None):
            out = []
            for mp, s in enumerate(scores(j, qs, first_key)):
                m_new = jnp.max(s, axis=-1, keepdims=True)
                if state is not None:
                    m_new = jnp.maximum(state[2 * mp], m_new)
                out += [m_new, jnp.exp2(s - m_new).astype(jnp.bfloat16)]
            accs = pv(j, out[1::2])
            if state is not None:
                accs = [jnp.exp2(state[2 * mp] - out[2 * mp]) * state[2 * mp + 1] + accs[mp]
                        for mp in range(2)]
            return (out[0], accs[0], out[2], accs[1])

        state = None
        for t in range(n):
            state = tile(n * qi + t, state, t * tk)
        _, acc1, _, acc2 = lax.fori_loop(0, n * qi, tile, state)
    else:
        def tile(j, q_list, first_key=None):
            return pv(j, [jnp.exp2(s).astype(jnp.bfloat16) for s in scores(j, q_list, first_key)])

        for mp, part in enumerate(tile(n * qi, qs, 0)):
            acc_ref[mp] = part
        for t in range(1, n):
            for mp, part in enumerate(tile(n * qi + t, [q[t * tk:] for q in qs], 0)):
                acc_ref[mp, t * tk:, :] += part

        def n_tiles(i, carry):
            total = tile(n * i, qs)
            for t in range(1, n):
                total = [a + b for a, b in zip(total, tile(n * i + t, qs))]
            for mp in range(2):
                acc_ref[mp] += total[mp]
            return carry

        lax.fori_loop(0, qi, n_tiles, 0)
        acc1, acc2 = acc_ref[0], acc_ref[1]

    lv = lam_ref[...]
    lam = (jnp.exp(jnp.sum(lv[0:1] * lv[1:2], axis=-1, keepdims=True))
           - jnp.exp(jnp.sum(lv[2:3] * lv[3:4], axis=-1, keepdims=True)) + LAMBDA_INIT_L0)
    o = acc1[:, :A_V] / acc1[:, A_V:] - lam * (acc2[:, :A_V] / acc2[:, A_V:])
    o_ref[...] = o.astype(jnp.bfloat16)


def _diff_attn(q1, q2, k1, k2, v, lam_vecs, *, bsz, seq, online):
    tq = TQ
    nq = seq // tq
    q_spec = pl.BlockSpec((tq, LANES), lambda b, h, i: (b * nq + i, h))
    kv_spec = pl.BlockSpec((seq, LANES), lambda b, h, i: (b, h))
    return pl.pallas_call(
        functools.partial(_diff_attn_kernel, tq=tq, tk=TK, online=online),
        grid=(bsz, A_HEADS, nq),
        in_specs=[
            q_spec, q_spec, kv_spec, kv_spec, kv_spec,
            pl.BlockSpec((4, HEAD_DIM), lambda b, h, i: (0, 0)),
        ],
        out_specs=q_spec,
        out_shape=jax.ShapeDtypeStruct((bsz * seq, A_HEADS * A_V), jnp.bfloat16),
        scratch_shapes=[pltpu.VMEM((2, tq, 2 * A_V), jnp.float32)],
        compiler_params=pltpu.CompilerParams(
            dimension_semantics=("arbitrary",) * 3, vmem_limit_bytes=VMEM_LIMIT),
        name="diff_attention_online" if online else "diff_attention",
    )(q1, q2, k1, k2, v, lam_vecs)


def _post_kernel(x_ref, ya_ref, yb_ref, wo_ref, g_ref, wg_ref, wu_ref, wd_ref, *rest):
    o_ref = rest[-1]
    half = ya_ref.shape[1]
    ya = ya_ref[...]
    if len(rest) == 2:
        heads = []
        for hd in range(half // A_V):
            o = ya[:, hd * A_V:(hd + 1) * A_V].astype(jnp.float32)
            heads.append((o * _rms_scale(o) * rest[0][...]).astype(jnp.bfloat16))
        ya = jnp.concatenate(heads, axis=1)
    x1 = x_ref[...] + _dot(ya, wo_ref[0:half, :]) + _dot(yb_ref[...], wo_ref[half:, :])
    h = (x1 * _rms_scale(x1) * g_ref[...]).astype(jnp.bfloat16)
    gate = _dot(h, wg_ref[...])
    up = _dot(h, wu_ref[...])
    act = (gate * jax.nn.sigmoid(gate) * up).astype(jnp.bfloat16)
    o_ref[...] = x1 + _dot(act, wd_ref[...])


def _post(x2d, ya, yb, wo, g, wg, wu, wd, layer, name, subnorm_gain=None):
    t = x2d.shape[0]
    tm = TM_PROJ
    row = lambda i: (i, 0)
    const = lambda i: (0, 0)
    resident = functools.partial(pl.BlockSpec, index_map=const, pipeline_mode=pl.Buffered(1))
    ffn_w = lambda rows, cols: pl.BlockSpec((None, rows, cols), lambda i: (layer, 0, 0),
                                            pipeline_mode=pl.Buffered(1))
    extra = [] if subnorm_gain is None else [subnorm_gain]
    return pl.pallas_call(
        _post_kernel,
        grid=(t // tm,),
        in_specs=[
            pl.BlockSpec((tm, D_MODEL), row),
            pl.BlockSpec((tm, ya.shape[1]), row),
            pl.BlockSpec((tm, yb.shape[1]), row),
            resident((D_MODEL, D_MODEL)),
            pl.BlockSpec((1, D_MODEL), const),
            ffn_w(D_MODEL, D_FF),
            ffn_w(D_MODEL, D_FF),
            ffn_w(D_FF, D_MODEL),
        ] + [pl.BlockSpec((1, A_V), const)] * len(extra),
        out_specs=pl.BlockSpec((tm, D_MODEL), row),
        out_shape=jax.ShapeDtypeStruct((t, D_MODEL), jnp.float32),
        compiler_params=pltpu.CompilerParams(
            dimension_semantics=("arbitrary",), vmem_limit_bytes=VMEM_LIMIT),
        name=name,
    )(x2d, ya, yb, wo, g, wg, wu, wd, *extra)


def _odd_in_kernel(x_ref, g_ref, w_ref, gsum_ref, qg_ref, kg_ref, cw_ref,
                   yc_ref, q_ref, k_ref, v_ref, carry_ref, *, tm, sm, blocks_per_seq):
    blk = pl.program_id(0) % blocks_per_seq

    @pl.when(blk == 0)
    def _():
        carry_ref[...] = jnp.zeros_like(carry_ref)

    for sb in range(tm // sm):
        rows = slice(sb * sm, (sb + 1) * sm)
        x = x_ref[rows, :]
        h = (x * _rms_scale(x) * g_ref[...]).astype(jnp.bfloat16)
        proj = lambda part: _dot(h, w_ref[:, part * 512:(part + 1) * 512])
        z = proj(1) * proj(2)
        bg = proj(0)
        ext = jnp.concatenate([carry_ref[...], z], axis=0)
        z1 = _shift_rows(ext, 1)[CONV_HALO:, :]
        z2 = _shift_rows(ext, 2)[CONV_HALO:, :]
        cw = cw_ref[...]
        yc = bg * (cw[0:1] * z2 + cw[1:2] * z1 + cw[2:3] * z)
        yc_ref[rows, :] = yc.astype(jnp.bfloat16)
        carry_ref[...] = z[sm - CONV_HALO:, :]

        q_ref[rows, :] = _group_rmsnorm(proj(3), gsum_ref, qg_ref).astype(jnp.bfloat16)
        k_ref[rows, :] = _group_rmsnorm(proj(4), gsum_ref, kg_ref).astype(jnp.bfloat16)
        v_ref[rows, :] = proj(5).astype(jnp.bfloat16)


def _odd_in(x2d, g, w, gsum, qg, kg, cw, *, seq):
    t = x2d.shape[0]
    tm = TM_IN
    const = lambda i: (0, 0)
    row = lambda i: (i, 0)
    out = jax.ShapeDtypeStruct((t, 512), jnp.bfloat16)
    return pl.pallas_call(
        functools.partial(_odd_in_kernel, tm=tm, sm=SM_IN, blocks_per_seq=seq // tm),
        grid=(t // tm,),
        in_specs=[
            pl.BlockSpec((tm, D_MODEL), row),
            pl.BlockSpec((1, D_MODEL), const),
            pl.BlockSpec((D_MODEL, 3072), const),
            pl.BlockSpec((MXU_DEPTH, MXU_DEPTH), const),
            pl.BlockSpec((1, 512), const),
            pl.BlockSpec((1, 512), const),
            pl.BlockSpec((3, 512), const),
        ],
        out_specs=[pl.BlockSpec((tm, 512), row)] * 4,
        out_shape=[out] * 4,
        scratch_shapes=[pltpu.VMEM((CONV_HALO, 512), jnp.float32)],
        compiler_params=pltpu.CompilerParams(
            dimension_semantics=("arbitrary",), vmem_limit_bytes=VMEM_LIMIT),
        name="odd_in_proj",
    )(x2d, g, w, gsum, qg, kg, cw)


def _pair_tile(q, kp, vp, bias):
    lane = lax.broadcasted_iota(jnp.int32, (1, LANES), 1)
    low = lane < HEAD_DIM
    zero = jnp.zeros_like(q)
    q2 = jnp.concatenate([jnp.where(low, q, zero), jnp.where(low, zero, q)], axis=0)
    p = jnp.exp2(_dot_nt(q2, kp) + bias).astype(jnp.bfloat16)
    r = _dot(p, _with_ones(vp))
    num = jnp.where(low, r[:SUB, :LANES], r[SUB:, :LANES])
    den = jnp.where(low, r[:SUB, LANES:], r[SUB:, LANES:])
    return num, den


def _dilated_fused_kernel(q_ref, k_ref, v_ref, perm_ref, tab_ref, o_ref,
                          kn, vn, k4, v4, k16, v16, qp, res4, res16):
    blk = pl.program_id(1)
    first = blk == 0
    cur = (blk & 1) * NG
    prev = NG - cur
    bf = jnp.bfloat16

    @pl.when(first)
    def _():
        kn[0:D_SPAN, :] = jnp.zeros((D_SPAN, D_WIDTH), bf)
        vn[0:D_SPAN, :] = jnp.zeros((D_SPAN, D_WIDTH), bf)
        k4[0:2] = jnp.zeros((2, GRP, D_WIDTH), bf)
        v4[0:2] = jnp.zeros((2, GRP, D_WIDTH), bf)
        k16[pl.ds(prev, NG)] = jnp.zeros((NG, GRP, D_WIDTH), bf)
        v16[pl.ds(prev, NG)] = jnp.zeros((NG, GRP, D_WIDTH), bf)

    @pl.when(jnp.logical_not(first))
    def _():
        kn[0:D_SPAN, :] = kn[DBLK:DBLK + D_SPAN, :]
        vn[0:D_SPAN, :] = vn[DBLK:DBLK + D_SPAN, :]
        k4[0:2] = k4[NG:NG + 2]
        v4[0:2] = v4[NG:NG + 2]

    kn[D_SPAN:, :] = k_ref[...]
    vn[D_SPAN:, :] = v_ref[...]
    for g in range(NG):
        rows = slice(g * GRP, (g + 1) * GRP)
        for src, d4, d16 in ((k_ref, k4, k16), (v_ref, v4, v16)):
            x = src[rows, :]
            d4[2 + g] = _dot(perm_ref[0], x).astype(bf)
            d16[cur + g] = _dot(perm_ref[2], x).astype(bf)

    kk = lax.broadcasted_iota(jnp.int32, (1, 2 * SUB), 1)
    no_halo = jnp.where(kk < D_SPAN, jnp.where(first, NEG, 0.0), 0.0)

    def pair_cols(p):
        return slice(p * LANES, (p + 1) * LANES)

    for g in range(NG):
        qp[g] = _dot(perm_ref[2], q_ref[g * GRP:(g + 1) * GRP, :]).astype(bf)

    def classes16(i, carry):
        for t in range(8):
            r = pl.ds(pl.multiple_of(i * 128 + t * 16, 16), 16)
            for p in range(D_HEADS // 2):
                cols = pair_cols(p)
                q = jnp.concatenate([qp[g, r, cols] for g in range(NG)], axis=0)
                kp = jnp.concatenate([k16[prev + g, r, cols] for g in range(NG)]
                                     + [k16[cur + g, r, cols] for g in range(NG)], axis=0)
                vp = jnp.concatenate([v16[prev + g, r, cols] for g in range(NG)]
                                     + [v16[cur + g, r, cols] for g in range(NG)], axis=0)
                num, den = _pair_tile(q, kp, vp, tab_ref[2, p] + no_halo)
                for g in range(NG):
                    res16[0, g, r, cols] = num[16 * g:16 * g + 16].astype(bf)
                    res16[1, g, r, cols] = den[16 * g:16 * g + 16].astype(bf)
        return carry

    lax.fori_loop(0, 2, classes16, 0)

    for g in range(NG):
        qp[g] = _dot(perm_ref[0], q_ref[g * GRP:(g + 1) * GRP, :]).astype(bf)

    def classes4(i, carry):
        for t in range(2):
            r = pl.ds(pl.multiple_of(i * 128 + t * 64, 64), 64)
            for qs in range(NG // 2):
                for p in range(D_HEADS // 2):
                    cols = pair_cols(p)
                    q = jnp.concatenate([qp[2 * qs + j, r, cols] for j in range(2)], axis=0)
                    kp = jnp.concatenate([k4[2 * qs + j, r, cols] for j in range(4)], axis=0)
                    vp = jnp.concatenate([v4[2 * qs + j, r, cols] for j in range(4)], axis=0)
                    bias = tab_ref[1, p] + no_halo if qs == 0 else tab_ref[1, p]
                    num, den = _pair_tile(q, kp, vp, bias)
                    for j in range(2):
                        res4[0, 2 * qs + j, r, cols] = num[64 * j:64 * j + 64].astype(bf)
                        res4[1, 2 * qs + j, r, cols] = den[64 * j:64 * j + 64].astype(bf)
        return carry

    lax.fori_loop(0, 2, classes4, 0)

    def groups(i, carry):
        for u in range(2):
            g = 2 * i + u
            num = _dot(perm_ref[1], res4[0, g]) + _dot(perm_ref[3], res16[0, g])
            den = _dot(perm_ref[1], res4[1, g]) + _dot(perm_ref[3], res16[1, g])
            for t in range(GRP // SUB):
                q_rows = pl.ds(pl.multiple_of(g * GRP + t * SUB, SUB), SUB)
                k_rows = pl.ds(pl.multiple_of(g * GRP + t * SUB, SUB), 2 * SUB)
                sub = slice(t * SUB, (t + 1) * SUB)
                for p in range(D_HEADS // 2):
                    cols = pair_cols(p)
                    bias = tab_ref[0, p]
                    if u == 0 and t == 0:
                        bias = bias + jnp.where(i == 0, no_halo, 0.0)
                    n1, d1 = _pair_tile(q_ref[q_rows, cols], kn[k_rows, cols], vn[k_rows, cols], bias)
                    o_ref[q_rows, cols] = ((n1 + num[sub, cols]) / (d1 + den[sub, cols])).astype(bf)
        return carry

    lax.fori_loop(0, NG // 2, groups, 0)


def _class_permutations():
    mats = []
    out = np.arange(GRP)
    for d in (4, 16):
        per = GRP // d
        p = np.zeros((GRP, GRP), np.float32)
        p[out, d * (out % per) + out // per] = 1.0
        mats += [p, p.T]
    return jnp.asarray(np.stack(mats), jnp.bfloat16)


def _dilated_bias_table(dil):
    ii = lax.broadcasted_iota(jnp.int32, (SUB, SUB + D_SPAN), 0)
    kk = lax.broadcasted_iota(jnp.int32, (SUB, SUB + D_SPAN), 1)
    delta = D_SPAN + ii - kk
    slopes = jnp.asarray(_alibi_slopes(D_HEADS), jnp.float32)
    bias = (-LOG2E) * slopes[:, None, None] * (delta * dil).astype(jnp.float32)[None]
    return jnp.where(((delta >= 0) & (delta <= D_SPAN))[None], bias, NEG)


def _dilated_fused(q, k, v, *, bsz, seq):
    nblk = seq // DBLK
    tabs = jnp.stack([_dilated_bias_table(d).reshape(D_HEADS // 2, 2 * SUB, SUB + D_SPAN)
                      for d in D_DILATIONS])
    blk_spec = pl.BlockSpec((DBLK, D_WIDTH), lambda b, i: (b * nblk + i, 0))
    group_buf = lambda n: pltpu.VMEM((n, GRP, D_WIDTH), jnp.bfloat16)
    return pl.pallas_call(
        _dilated_fused_kernel,
        grid=(bsz, nblk),
        in_specs=[
            blk_spec, blk_spec, blk_spec,
            pl.BlockSpec((4, GRP, GRP), lambda b, i: (0, 0, 0), pipeline_mode=pl.Buffered(1)),
            pl.BlockSpec(tabs.shape, lambda b, i: (0, 0, 0, 0), pipeline_mode=pl.Buffered(1)),
        ],
        out_specs=blk_spec,
        out_shape=jax.ShapeDtypeStruct((bsz * seq, D_WIDTH), jnp.bfloat16),
        scratch_shapes=[
            pltpu.VMEM((D_SPAN + DBLK, D_WIDTH), jnp.bfloat16),
            pltpu.VMEM((D_SPAN + DBLK, D_WIDTH), jnp.bfloat16),
            group_buf(2 + NG), group_buf(2 + NG),
            group_buf(2 * NG), group_buf(2 * NG),
            group_buf(NG),
            pltpu.VMEM((2, NG, GRP, D_WIDTH), jnp.bfloat16),
            pltpu.VMEM((2, NG, GRP, D_WIDTH), jnp.bfloat16),
        ],
        compiler_params=pltpu.CompilerParams(
            dimension_semantics=("arbitrary",) * 2, vmem_limit_bytes=VMEM_LIMIT),
        name="dilated_attention_fused",
    )(q, k, v, _class_permutations(), tabs)


def _dilated_kernel(q_ref, kh_ref, kc_ref, vh_ref, vc_ref, bias_ref, o_ref, lse_ref,
                    kbuf, vbuf, *, tl):
    j = pl.program_id(2)
    kbuf[0:D_SPAN, :] = kh_ref[...]
    kbuf[D_SPAN:, :] = kc_ref[...]
    vbuf[0:D_SPAN, :] = vh_ref[...]
    vbuf[D_SPAN:, :] = vc_ref[...]
    lane = lax.broadcasted_iota(jnp.int32, (1, LANES), 1)
    kk = lax.broadcasted_iota(jnp.int32, (1, SUB + D_SPAN), 1)
    halo_pen = jnp.where(kk < D_SPAN, jnp.where(j == 0, NEG, 0.0), 0.0)
    for qs in range(tl // SUB):
        rows = slice(qs * SUB, (qs + 1) * SUB)
        krows = slice(qs * SUB, (qs + 1) * SUB + D_SPAN)
        lse_tile = jnp.zeros((SUB, LANES), jnp.float32)
        for p in range(D_HEADS // 2):
            cols = slice(p * LANES, (p + 1) * LANES)
            q = q_ref[rows, cols]
            kp = kbuf[krows, cols]
            vp = vbuf[krows, cols]
            zero = jnp.zeros_like(q)
            halves = []
            for e in range(2):
                hd = 2 * p + e
                qm = jnp.where((lane < HEAD_DIM) if e == 0 else (lane >= HEAD_DIM), q, zero)
                s = _dot_nt(qm, kp) + bias_ref[hd]
                if qs == 0:
                    s = s + halo_pen
                m = jnp.max(s, axis=-1, keepdims=True)
                pe = jnp.exp2(s - m)
                l = jnp.sum(pe, axis=-1, keepdims=True)
                halves.append(_dot(pe.astype(jnp.bfloat16), vp) / l)
                lse_tile = jnp.where(lane == hd, m + jnp.log2(l), lse_tile)
            o_ref[rows, cols] = jnp.where(lane < HEAD_DIM, halves[0], halves[1]).astype(jnp.bfloat16)
        lse_ref[rows, :] = lse_tile


def _dilated_branch(q, k, v, *, bsz, seq, dil):
    tl = TL
    length = seq // dil
    nb = length // tl
    halo_per_blk = tl // D_SPAN
    view = lambda a: a.reshape(bsz, length, dil * D_WIDTH)
    cur = lambda b, r, j: (b, j, r)
    halo = lambda b, r, j: (b, jnp.maximum(j * halo_per_blk - 1, 0), r)
    o, lse = pl.pallas_call(
        functools.partial(_dilated_kernel, tl=tl),
        grid=(bsz, dil, nb),
        in_specs=[
            pl.BlockSpec((None, tl, D_WIDTH), cur),
            pl.BlockSpec((None, D_SPAN, D_WIDTH), halo),
            pl.BlockSpec((None, tl, D_WIDTH), cur),
            pl.BlockSpec((None, D_SPAN, D_WIDTH), halo),
            pl.BlockSpec((None, tl, D_WIDTH), cur),
            pl.BlockSpec((D_HEADS, SUB, SUB + D_SPAN), lambda b, r, j: (0, 0, 0)),
        ],
        out_specs=[
            pl.BlockSpec((None, tl, D_WIDTH), cur),
            pl.BlockSpec((None, tl, LANES), cur),
        ],
        out_shape=[
            jax.ShapeDtypeStruct((bsz, length, dil * D_WIDTH), jnp.bfloat16),
            jax.ShapeDtypeStruct((bsz, length, dil * LANES), jnp.float32),
        ],
        scratch_shapes=[pltpu.VMEM((tl + D_SPAN, D_WIDTH), jnp.bfloat16)] * 2,
        compiler_params=pltpu.CompilerParams(
            dimension_semantics=("arbitrary",) * 3, vmem_limit_bytes=VMEM_LIMIT),
        name=f"dilated_attention_d{dil}",
    )(view(q), view(k), view(k), view(v), view(v), _dilated_bias_table(dil))
    return o.reshape(bsz * seq, D_WIDTH), lse.reshape(bsz * seq, LANES)


def _branch_combine_kernel(o1_ref, o2_ref, o3_ref, l1_ref, l2_ref, l3_ref, y_ref):
    lses = [l1_ref[...], l2_ref[...], l3_ref[...]]
    mx = jnp.maximum(jnp.maximum(lses[0], lses[1]), lses[2])
    es = [jnp.exp2(l - mx) for l in lses]
    inv = 1.0 / (es[0] + es[1] + es[2])
    ws = [e * inv for e in es]
    lane = lax.broadcasted_iota(jnp.int32, (1, LANES), 1)
    for p in range(D_HEADS // 2):
        cols = slice(p * LANES, (p + 1) * LANES)
        yd = None
        for w, o in zip(ws, (o1_ref, o2_ref, o3_ref)):
            wf = jnp.where(lane < HEAD_DIM, w[:, 2 * p:2 * p + 1], w[:, 2 * p + 1:2 * p + 2])
            term = wf * o[:, cols].astype(jnp.float32)
            yd = term if yd is None else yd + term
        y_ref[:, cols] = yd.astype(jnp.bfloat16)


def _dilated_by_branch(q, k, v, *, bsz, seq):
    outs, lses = zip(*[_dilated_branch(q, k, v, bsz=bsz, seq=seq, dil=d) for d in D_DILATIONS])
    tm = TM_PROJ
    row = lambda i: (i, 0)
    return pl.pallas_call(
        _branch_combine_kernel,
        grid=(bsz * seq // tm,),
        in_specs=[pl.BlockSpec((tm, D_WIDTH), row)] * 3 + [pl.BlockSpec((tm, LANES), row)] * 3,
        out_specs=pl.BlockSpec((tm, D_WIDTH), row),
        out_shape=jax.ShapeDtypeStruct((bsz * seq, D_WIDTH), jnp.bfloat16),
        compiler_params=pltpu.CompilerParams(
            dimension_semantics=("arbitrary",), vmem_limit_bytes=VMEM_LIMIT),
        name="dilated_branch_combine",
    )(*outs, *lses)


def kernel(x, norm_mix, norm_ffn, ev_w_in, ev_w_out, ev_q_gain, ev_k_gain, ev_lambda_q1, ev_lambda_k1, ev_lambda_q2, ev_lambda_k2, ev_subln_gain, ev_pool_w, ev_pool_scale, od_w_in, od_w_out, od_conv_w, od_q_gain, od_k_gain, ffn_w_gate, ffn_w_up, ffn_w_down):
    bsz, seq, _ = x.shape
    bf = jnp.bfloat16
    f32 = jnp.float32
    q_scale = HEAD_DIM ** -0.5 * LOG2E
    x2d = x.reshape(bsz * seq, D_MODEL)
    grp = lax.broadcasted_iota(jnp.int32, (MXU_DEPTH, MXU_DEPTH), 0) // HEAD_DIM
    gsum = jnp.where(grp == grp.T, 1.0 / HEAD_DIM, 0.0).astype(bf)
    pw = ev_pool_w[0].astype(bf)
    zpw = jnp.zeros_like(pw[0])
    pool_w = jnp.stack([jnp.block([[pw[2 * i], zpw], [zpw, pw[2 * i + 1]]]) for i in range(2)])
    tile8 = lambda gain, s: (jnp.tile(gain.astype(f32), 8) * s).reshape(1, 512)

    w = ev_w_in[0]
    regroup = lambda c: c.reshape(D_MODEL, 2, A_HEADS, HEAD_DIM).transpose(0, 2, 1, 3).reshape(D_MODEL, 512)
    w_even = jnp.concatenate([regroup(w[:, 0:512]), regroup(w[:, 512:1024]), w[:, 1024:]], axis=1).astype(bf)
    q1, q2, k1, k2, v, yb = _even_in(
        x2d, norm_mix[0].reshape(1, D_MODEL).astype(f32), w_even, gsum,
        tile8(ev_q_gain[0], q_scale), tile8(ev_k_gain[0], 1.0),
        pool_w, ev_pool_scale[0].reshape(1, 512).astype(f32), seq=seq)
    lam_vecs = jnp.stack([ev_lambda_q1[0], ev_lambda_k1[0], ev_lambda_q2[0], ev_lambda_k2[0]]).astype(f32)
    attn = functools.partial(_diff_attn, q1, q2, k1, k2, v, lam_vecs, bsz=bsz, seq=seq)
    ya = lax.cond(_score_bound(ev_q_gain[0], ev_k_gain[0]) <= MAX_UNSHIFTED_SCORE,
                  lambda: attn(online=False), lambda: attn(online=True))
    ffn_w = (ffn_w_gate.astype(bf), ffn_w_up.astype(bf), ffn_w_down.astype(bf))
    x2d = _post(x2d, ya, yb, ev_w_out[0].astype(bf),
                norm_ffn[0].reshape(1, D_MODEL).astype(f32), *ffn_w, 0, "even_post",
                subnorm_gain=(ev_subln_gain[0].astype(f32) * (1.0 - LAMBDA_INIT_L0)).reshape(1, A_V))

    yc, q, k, v = _odd_in(
        x2d, norm_mix[1].reshape(1, D_MODEL).astype(f32), od_w_in[0].astype(bf), gsum,
        tile8(od_q_gain[0], q_scale), tile8(od_k_gain[0], 1.0), od_conv_w[0].astype(f32), seq=seq)
    yd = lax.cond(_score_bound(od_q_gain[0], od_k_gain[0]) <= MAX_UNSHIFTED_SCORE,
                  lambda: _dilated_fused(q, k, v, bsz=bsz, seq=seq),
                  lambda: _dilated_by_branch(q, k, v, bsz=bsz, seq=seq))
    x2d = _post(x2d, yc, yd, od_w_out[0].astype(bf),
                norm_ffn[1].reshape(1, D_MODEL).astype(f32), *ffn_w, 1, "odd_post")
    return x2d.reshape(bsz, seq, D_MODEL)
```

```python
import functools
import math

import numpy as np
import jax
import jax.numpy as jnp
from jax import lax
from jax.experimental import pallas as pl
from jax.experimental.pallas import tpu as pltpu

D_MODEL = 1024
HEAD_DIM = 64
EPS = 1e-6
A_HEADS = 4
A_V = 128
B_WINDOWS = (2, 4, 8, 16)
POOL_HALO = 16
CONV_HALO = 8
D_HEADS = 8
D_WIDTH = D_HEADS * HEAD_DIM
D_DILATIONS = (1, 4, 16)
D_SPAN = 128
D_FF = 2816
LAMBDA_INIT_L0 = 0.8 - 0.6 * math.exp(-0.3 * 0)
NEG = -1e30
LOG2E = math.log2(math.e)
LOG2E_HI = float(np.float32(LOG2E).astype(jnp.bfloat16))
LOG2E_LO = float(np.float32(LOG2E - LOG2E_HI).astype(jnp.bfloat16))
MAX_UNSHIFTED_SCORE = 60.0

LANES = 128
MXU_DEPTH = 256
VMEM_LIMIT = 56 * 1024 * 1024

TM_PROJ = 512
TM_IN = 1024
SM_IN = 512
TQ = 2048
TK = 512
TL = 512
SUB = 128
DBLK = 2048
GRP = 256
NG = DBLK // GRP

_NT = (((1,), (1,)), ((), ()))


def _dot(a, b):
    return jnp.dot(a, b, preferred_element_type=jnp.float32)


def _dot_nt(a, b):
    return lax.dot_general(a, b, _NT, preferred_element_type=jnp.float32)


def _rms_scale(x):
    return lax.rsqrt(jnp.mean(x * x, axis=-1, keepdims=True) + EPS)


def _group_rmsnorm(t, gsum_ref, gain):
    sq = (t * t).astype(jnp.bfloat16)
    depth = gsum_ref.shape[0]
    msq = jnp.concatenate([_dot(sq[:, c:c + depth], gsum_ref[...])
                           for c in range(0, t.shape[1], depth)], axis=1)
    return t * lax.rsqrt(msq + EPS) * gain


def _alibi_slopes(n):
    return [2.0 ** (-8.0 * (i + 1) / n) for i in range(n)]


def _alibi_lanes(pos, lane, first):
    rel = lane - first
    hi = ((pos >> 8) << 8).astype(jnp.float32)
    lo = (pos & 255).astype(jnp.float32)
    pos_sel = jnp.where((rel & 1) == 0, hi, lo)
    l_sel = jnp.where((rel & 3) < 2, LOG2E_HI, LOG2E_LO)
    first4 = (rel >= 0) & (rel < 4)
    last4 = (rel >= 4) & (rel < 8)
    return (jnp.where(first4, l_sel, jnp.where(last4, -pos_sel, 0.0)),
            jnp.where(first4, pos_sel, jnp.where(last4, l_sel, 0.0)))


def _shift_rows(ext, d):
    return pltpu.roll(ext, d, axis=0)


def _with_ones(v):
    return jnp.concatenate([v, jnp.ones_like(v)], axis=1)


def _score_bound(q_gain, k_gain):
    return (HEAD_DIM ** 0.5) * jnp.max(jnp.abs(q_gain)) * jnp.max(jnp.abs(k_gain))


def _even_in_kernel(x_ref, g_ref, w_ref, gsum_ref, qg_ref, kg_ref, pw_ref, ps_ref,
                    q1_ref, q2_ref, k1_ref, k2_ref, v_ref, yb_ref, carry_ref,
                    *, tm, sm, blocks_per_seq):
    blk = pl.program_id(0) % blocks_per_seq

    @pl.when(blk == 0)
    def _():
        carry_ref[...] = jnp.zeros_like(carry_ref)

    for sb in range(tm // sm):
        rows = slice(sb * sm, (sb + 1) * sm)
        x = x_ref[rows, :]
        h = (x * _rms_scale(x) * g_ref[...]).astype(jnp.bfloat16)
        proj = lambda part: _dot(h, w_ref[:, part * 512:(part + 1) * 512])
        qn = _group_rmsnorm(proj(0), gsum_ref, qg_ref[...])
        kn = _group_rmsnorm(proj(1), gsum_ref, kg_ref[...])
        v_ref[rows, :] = proj(2).astype(jnp.bfloat16)

        pos = lax.broadcasted_iota(jnp.int32, (sm, 1), 0) + (blk * tm + sb * sm)
        lane = lax.broadcasted_iota(jnp.int32, (1, LANES), 1)
        low = lane < HEAD_DIM
        for mp, (q_out, k_out) in enumerate(((q1_ref, k1_ref), (q2_ref, k2_ref))):
            q_lanes, k_lanes = _alibi_lanes(pos, lane, HEAD_DIM if mp == 0 else 0)
            data = low if mp == 0 else jnp.logical_not(low)
            for hd, slope in enumerate(_alibi_slopes(A_HEADS)):
                cols = slice(hd * LANES, (hd + 1) * LANES)
                q_out[rows, cols] = jnp.where(data, qn[:, cols], slope * q_lanes).astype(jnp.bfloat16)
                k_out[rows, cols] = jnp.where(data, kn[:, cols], k_lanes).astype(jnp.bfloat16)

        u = proj(3)
        ext = jnp.concatenate([carry_ref[...], u], axis=0)
        pos1 = (pos + 1).astype(jnp.float32)
        pooled = []
        for g, w in enumerate(B_WINDOWS):
            cols = slice(g * LANES, (g + 1) * LANES)
            s = ext[:, cols]
            d = 1
            while d < w:
                s = s + _shift_rows(s, d)
                d *= 2
            pooled.append((s[POOL_HALO:, :] / jnp.minimum(pos1, float(w)) - u[:, cols]).astype(jnp.bfloat16))
        for pg in range(len(B_WINDOWS) // 2):
            cols = slice(2 * pg * LANES, (2 * pg + 2) * LANES)
            yb = _dot(jnp.concatenate(pooled[2 * pg:2 * pg + 2], axis=1), pw_ref[pg]) * ps_ref[:, cols]
            yb_ref[rows, cols] = yb.astype(jnp.bfloat16)
        carry_ref[...] = u[sm - POOL_HALO:, :]


def _even_in(x2d, g, w, gsum, qg, kg, pw, ps, *, seq):
    t = x2d.shape[0]
    tm = TM_IN
    const = lambda i: (0, 0)
    row = lambda i: (i, 0)
    out = jax.ShapeDtypeStruct((t, 512), jnp.bfloat16)
    return pl.pallas_call(
        functools.partial(_even_in_kernel, tm=tm, sm=SM_IN, blocks_per_seq=seq // tm),
        grid=(t // tm,),
        in_specs=[
            pl.BlockSpec((tm, D_MODEL), row),
            pl.BlockSpec((1, D_MODEL), const),
            pl.BlockSpec((D_MODEL, 2048), const),
            pl.BlockSpec((MXU_DEPTH, MXU_DEPTH), const),
            pl.BlockSpec((1, 512), const),
            pl.BlockSpec((1, 512), const),
            pl.BlockSpec((2, MXU_DEPTH, MXU_DEPTH), lambda i: (0, 0, 0)),
            pl.BlockSpec((1, 512), const),
        ],
        out_specs=[pl.BlockSpec((tm, 512), row)] * 6,
        out_shape=[out] * 6,
        scratch_shapes=[pltpu.VMEM((POOL_HALO, 512), jnp.float32)],
        compiler_params=pltpu.CompilerParams(
            dimension_semantics=("arbitrary",), vmem_limit_bytes=VMEM_LIMIT),
        name="even_in_proj",
    )(x2d, g, w, gsum, qg, kg, pw, ps)


def _diff_attn_kernel(q1_ref, q2_ref, k1_ref, k2_ref, v_ref, lam_ref, o_ref, acc_ref,
                      *, tq, tk, online):
    n = tq // tk
    qi = pl.program_id(2)
    qs = (q1_ref[...], q2_ref[...])
    k_refs = (k1_ref, k2_ref)

    def key_rows(j):
        return pl.ds(pl.multiple_of(j * tk, tk), tk)

    def scores(j, q_list, first_key=None):
        out = []
        for q, k_ref in zip(q_list, k_refs):
            s = _dot_nt(q, k_ref[key_rows(j), :])
            if first_key is not None:
                ii = lax.broadcasted_iota(jnp.int32, s.shape, 0)
                jj = lax.broadcasted_iota(jnp.int32, s.shape, 1)
                s = jnp.where(jj + first_key > ii, NEG, s)
            out.append(s)
        return out

    def pv(j, ps):
        v1 = _with_ones(v_ref[key_rows(j), :])
        return [_dot(p, v1) for p in ps]

    if online:
        def tile(j, state, first_key=None):
            out = []
            for mp, s in enumerate(scores(j, qs, first_key)):
                m_new = jnp.max(s, axis=-1, keepdims=True)
                if state is not None:
                    m_new = jnp.maximum(state[2 * mp], m_new)
                out += [m_new, jnp.exp2(s - m_new).astype(jnp.bfloat16)]
            accs = pv(j, out[1::2])
            if state is not None:
                accs = [jnp.exp2(state[2 * mp] - out[2 * mp]) * state[2 * mp + 1] + accs[mp]
                        for mp in range(2)]
            return (out[0], accs[0], out[2], accs[1])

        state = None
        for t in range(n):
            state = tile(n * qi + t, state, t * tk)
        _, acc1, _, acc2 = lax.fori_loop(0, n * qi, tile, state)
    else:
        def tile(j, q_list, first_key=None):
            return pv(j, [jnp.exp2(s).astype(jnp.bfloat16) for s in scores(j, q_list, first_key)])

        for mp, part in enumerate(tile(n * qi, qs, 0)):
            acc_ref[mp] = part
        for t in range(1, n):
            for mp, part in enumerate(tile(n * qi + t, [q[t * tk:] for q in qs], 0)):
                acc_ref[mp, t * tk:, :] += part

        def n_tiles(i, carry):
            total = tile(n * i, qs)
            for t in range(1, n):
                total = [a + b for a, b in zip(total, tile(n * i + t, qs))]
            for mp in range(2):
                acc_ref[mp] += total[mp]
            return carry

        lax.fori_loop(0, qi, n_tiles, 0)
        acc1, acc2 = acc_ref[0], acc_ref[1]

    lv = lam_ref[...]
    lam = (jnp.exp(jnp.sum(lv[0:1] * lv[1:2], axis=-1, keepdims=True))
           - jnp.exp(jnp.sum(lv[2:3] * lv[3:4], axis=-1, keepdims=True)) + LAMBDA_INIT_L0)
    o = acc1[:, :A_V] / acc1[:, A_V:] - lam * (acc2[:, :A_V] / acc2[:, A_V:])
    o_ref[...] = o.astype(jnp.bfloat16)


def _diff_attn(q1, q2, k1, k2, v, lam_vecs, *, bsz, seq, online):
    tq = TQ
    nq = seq // tq
    q_spec = pl.BlockSpec((tq, LANES), lambda b, h, i: (b * nq + i, h))
    kv_spec = pl.BlockSpec((seq, LANES), lambda b, h, i: (b, h))
    return pl.pallas_call(
        functools.partial(_diff_attn_kernel, tq=tq, tk=TK, online=online),
        grid=(bsz, A_HEADS, nq),
        in_specs=[
            q_spec, q_spec, kv_spec, kv_spec, kv_spec,
            pl.BlockSpec((4, HEAD_DIM), lambda b, h, i: (0, 0)),
        ],
        out_specs=q_spec,
        out_shape=jax.ShapeDtypeStruct((bsz * seq, A_HEADS * A_V), jnp.bfloat16),
        scratch_shapes=[pltpu.VMEM((2, tq, 2 * A_V), jnp.float32)],
        compiler_params=pltpu.CompilerParams(
            dimension_semantics=("arbitrary",) * 3, vmem_limit_bytes=VMEM_LIMIT),
        name="diff_attention_online" if online else "diff_attention",
    )(q1, q2, k1, k2, v, lam_vecs)


def _post_kernel(x_ref, ya_ref, yb_ref, wo_ref, g_ref, wg_ref, wu_ref, wd_ref, *rest):
    o_ref = rest[-1]
    half = ya_ref.shape[1]
    ya = ya_ref[...]
    if len(rest) == 2:
        heads = []
        for hd in range(half // A_V):
            o = ya[:, hd * A_V:(hd + 1) * A_V].astype(jnp.float32)
            heads.append((o * _rms_scale(o) * rest[0][...]).astype(jnp.bfloat16))
        ya = jnp.concatenate(heads, axis=1)
    x1 = x_ref[...] + _dot(ya, wo_ref[0:half, :]) + _dot(yb_ref[...], wo_ref[half:, :])
    h = (x1 * _rms_scale(x1) * g_ref[...]).astype(jnp.bfloat16)
    gate = _dot(h, wg_ref[...])
    up = _dot(h, wu_ref[...])
    act = (gate * jax.nn.sigmoid(gate) * up).astype(jnp.bfloat16)
    o_ref[...] = x1 + _dot(act, wd_ref[...])


def _post(x2d, ya, yb, wo, g, wg, wu, wd, layer, name, subnorm_gain=None):
    t = x2d.shape[0]
    tm = TM_PROJ
    row = lambda i: (i, 0)
    const = lambda i: (0, 0)
    resident = functools.partial(pl.BlockSpec, index_map=const, pipeline_mode=pl.Buffered(1))
    ffn_w = lambda rows, cols: pl.BlockSpec((None, rows, cols), lambda i: (layer, 0, 0),
                                            pipeline_mode=pl.Buffered(1))
    extra = [] if subnorm_gain is None else [subnorm_gain]
    return pl.pallas_call(
        _post_kernel,
        grid=(t // tm,),
        in_specs=[
            pl.BlockSpec((tm, D_MODEL), row),
            pl.BlockSpec((tm, ya.shape[1]), row),
            pl.BlockSpec((tm, yb.shape[1]), row),
            resident((D_MODEL, D_MODEL)),
            pl.BlockSpec((1, D_MODEL), const),
            ffn_w(D_MODEL, D_FF),
            ffn_w(D_MODEL, D_FF),
            ffn_w(D_FF, D_MODEL),
        ] + [pl.BlockSpec((1, A_V), const)] * len(extra),
        out_specs=pl.BlockSpec((tm, D_MODEL), row),
        out_shape=jax.ShapeDtypeStruct((t, D_MODEL), jnp.float32),
        compiler_params=pltpu.CompilerParams(
            dimension_semantics=("arbitrary",), vmem_limit_bytes=VMEM_LIMIT),
        name=name,
    )(x2d, ya, yb, wo, g, wg, wu, wd, *extra)


def _odd_in_kernel(x_ref, g_ref, w_ref, gsum_ref, qg_ref, kg_ref, cw_ref,
                   yc_ref, q_ref, k_ref, v_ref, carry_ref, *, tm, sm, blocks_per_seq):
    blk = pl.program_id(0) % blocks_per_seq

    @pl.when(blk == 0)
    def _():
        carry_ref[...] = jnp.zeros_like(carry_ref)

    for sb in range(tm // sm):
        rows = slice(sb * sm, (sb + 1) * sm)
        x = x_ref[rows, :]
        h = (x * _rms_scale(x) * g_ref[...]).astype(jnp.bfloat16)
        proj = lambda part: _dot(h, w_ref[:, part * 512:(part + 1) * 512])
        z = proj(1) * proj(2)
        bg = proj(0)
        ext = jnp.concatenate([carry_ref[...], z], axis=0)
        z1 = _shift_rows(ext, 1)[CONV_HALO:, :]
        z2 = _shift_rows(ext, 2)[CONV_HALO:, :]
        cw = cw_ref[...]
        yc = bg * (cw[0:1] * z2 + cw[1:2] * z1 + cw[2:3] * z)
        yc_ref[rows, :] = yc.astype(jnp.bfloat16)
        carry_ref[...] = z[sm - CONV_HALO:, :]

        q_ref[rows, :] = _group_rmsnorm(proj(3), gsum_ref, qg_ref[...]).astype(jnp.bfloat16)
        k_ref[rows, :] = _group_rmsnorm(proj(4), gsum_ref, kg_ref[...]).astype(jnp.bfloat16)
        v_ref[rows, :] = proj(5).astype(jnp.bfloat16)


def _odd_in(x2d, g, w, gsum, qg, kg, cw, *, seq):
    t = x2d.shape[0]
    tm = TM_IN
    const = lambda i: (0, 0)
    row = lambda i: (i, 0)
    out = jax.ShapeDtypeStruct((t, 512), jnp.bfloat16)
    return pl.pallas_call(
        functools.partial(_odd_in_kernel, tm=tm, sm=SM_IN, blocks_per_seq=seq // tm),
        grid=(t // tm,),
        in_specs=[
            pl.BlockSpec((tm, D_MODEL), row),
            pl.BlockSpec((1, D_MODEL), const),
            pl.BlockSpec((D_MODEL, 3072), const),
            pl.BlockSpec((MXU_DEPTH, MXU_DEPTH), const),
            pl.BlockSpec((1, 512), const),
            pl.BlockSpec((1, 512), const),
            pl.BlockSpec((3, 512), const),
        ],
        out_specs=[pl.BlockSpec((tm, 512), row)] * 4,
        out_shape=[out] * 4,
        scratch_shapes=[pltpu.VMEM((CONV_HALO, 512), jnp.float32)],
        compiler_params=pltpu.CompilerParams(
            dimension_semantics=("arbitrary",), vmem_limit_bytes=VMEM_LIMIT),
        name="odd_in_proj",
    )(x2d, g, w, gsum, qg, kg, cw)


def _pair_tile(q, kp, vp, bias):
    lane = lax.broadcasted_iota(jnp.int32, (1, LANES), 1)
    low = lane < HEAD_DIM
    zero = jnp.zeros_like(q)
    q2 = jnp.concatenate([jnp.where(low, q, zero), jnp.where(low, zero, q)], axis=0)
    p = jnp.exp2(_dot_nt(q2, kp) + bias).astype(jnp.bfloat16)
    r = _dot(p, _with_ones(vp))
    num = jnp.where(low, r[:SUB, :LANES], r[SUB:, :LANES])
    den = jnp.where(low, r[:SUB, LANES:], r[SUB:, LANES:])
    return num, den


def _dilated_fused_kernel(q_ref, k_ref, v_ref, perm_ref, tab_ref, o_ref,
                          kn, vn, k4, v4, k16, v16, qp, res4, res16):
    blk = pl.program_id(1)
    first = blk == 0
    cur = (blk & 1) * NG
    prev = NG - cur
    bf = jnp.bfloat16

    @pl.when(first)
    def _():
        kn[0:D_SPAN, :] = jnp.zeros((D_SPAN, D_WIDTH), bf)
        vn[0:D_SPAN, :] = jnp.zeros((D_SPAN, D_WIDTH), bf)
        k4[0:2] = jnp.zeros((2, GRP, D_WIDTH), bf)
        v4[0:2] = jnp.zeros((2, GRP, D_WIDTH), bf)
        k16[pl.ds(prev, NG)] = jnp.zeros((NG, GRP, D_WIDTH), bf)
        v16[pl.ds(prev, NG)] = jnp.zeros((NG, GRP, D_WIDTH), bf)

    @pl.when(jnp.logical_not(first))
    def _():
        kn[0:D_SPAN, :] = kn[DBLK:DBLK + D_SPAN, :]
        vn[0:D_SPAN, :] = vn[DBLK:DBLK + D_SPAN, :]
        k4[0:2] = k4[NG:NG + 2]
        v4[0:2] = v4[NG:NG + 2]

    kn[D_SPAN:, :] = k_ref[...]
    vn[D_SPAN:, :] = v_ref[...]
    for g in range(NG):
        rows = slice(g * GRP, (g + 1) * GRP)
        for src, d4, d16 in ((k_ref, k4, k16), (v_ref, v4, v16)):
            x = src[rows, :]
            d4[2 + g] = _dot(perm_ref[0], x).astype(bf)
            d16[cur + g] = _dot(perm_ref[2], x).astype(bf)

    kk = lax.broadcasted_iota(jnp.int32, (1, 2 * SUB), 1)
    no_halo = jnp.where(kk < D_SPAN, jnp.where(first, NEG, 0.0), 0.0)

    def pair_cols(p):
        return slice(p * LANES, (p + 1) * LANES)

    for g in range(NG):
        qp[g] = _dot(perm_ref[2], q_ref[g * GRP:(g + 1) * GRP, :]).astype(bf)

    def classes16(i, carry):
        for t in range(8):
            r = pl.ds(pl.multiple_of(i * 128 + t * 16, 16), 16)
            for p in range(D_HEADS // 2):
                cols = pair_cols(p)
                q = jnp.concatenate([qp[g, r, cols] for g in range(NG)], axis=0)
                kp = jnp.concatenate([k16[prev + g, r, cols] for g in range(NG)]
                                     + [k16[cur + g, r, cols] for g in range(NG)], axis=0)
                vp = jnp.concatenate([v16[prev + g, r, cols] for g in range(NG)]
                                     + [v16[cur + g, r, cols] for g in range(NG)], axis=0)
                num, den = _pair_tile(q, kp, vp, tab_ref[2, p] + no_halo)
                for g in range(NG):
                    res16[0, g, r, cols] = num[16 * g:16 * g + 16].astype(bf)
                    res16[1, g, r, cols] = den[16 * g:16 * g + 16].astype(bf)
        return carry

    lax.fori_loop(0, 2, classes16, 0)

    for g in range(NG):
        qp[g] = _dot(perm_ref[0], q_ref[g * GRP:(g + 1) * GRP, :]).astype(bf)

    def classes4(i, carry):
        for t in range(2):
            r = pl.ds(pl.multiple_of(i * 128 + t * 64, 64), 64)
            for qs in range(NG // 2):
                for p in range(D_HEADS // 2):
                    cols = pair_cols(p)
                    q = jnp.concatenate([qp[2 * qs + j, r, cols] for j in range(2)], axis=0)
                    kp = jnp.concatenate([k4[2 * qs + j, r, cols] for j in range(4)], axis=0)
                    vp = jnp.concatenate([v4[2 * qs + j, r, cols] for j in range(4)], axis=0)
                    bias = tab_ref[1, p] + no_halo if qs == 0 else tab_ref[1, p]
                    num, den = _pair_tile(q, kp, vp, bias)
                    for j in range(2):
                        res4[0, 2 * qs + j, r, cols] = num[64 * j:64 * j + 64].astype(bf)
                        res4[1, 2 * qs + j, r, cols] = den[64 * j:64 * j + 64].astype(bf)
        return carry

    lax.fori_loop(0, 2, classes4, 0)

    def groups(i, carry):
        for u in range(2):
            g = 2 * i + u
            num = _dot(perm_ref[1], res4[0, g]) + _dot(perm_ref[3], res16[0, g])
            den = _dot(perm_ref[1], res4[1, g]) + _dot(perm_ref[3], res16[1, g])
            for t in range(GRP // SUB):
                q_rows = pl.ds(pl.multiple_of(g * GRP + t * SUB, SUB), SUB)
                k_rows = pl.ds(pl.multiple_of(g * GRP + t * SUB, SUB), 2 * SUB)
                sub = slice(t * SUB, (t + 1) * SUB)
                for p in range(D_HEADS // 2):
                    cols = pair_cols(p)
                    bias = tab_ref[0, p]
                    if u == 0 and t == 0:
                        bias = bias + jnp.where(i == 0, no_halo, 0.0)
                    n1, d1 = _pair_tile(q_ref[q_rows, cols], kn[k_rows, cols], vn[k_rows, cols], bias)
                    o_ref[q_rows, cols] = ((n1 + num[sub, cols]) / (d1 + den[sub, cols])).astype(bf)
        return carry

    lax.fori_loop(0, NG // 2, groups, 0)


def _class_permutations():
    mats = []
    out = np.arange(GRP)
    for d in (4, 16):
        per = GRP // d
        p = np.zeros((GRP, GRP), np.float32)
        p[out, d * (out % per) + out // per] = 1.0
        mats += [p, p.T]
    return jnp.asarray(np.stack(mats), jnp.bfloat16)


def _dilated_bias_table(dil):
    ii = lax.broadcasted_iota(jnp.int32, (SUB, SUB + D_SPAN), 0)
    kk = lax.broadcasted_iota(jnp.int32, (SUB, SUB + D_SPAN), 1)
    delta = D_SPAN + ii - kk
    slopes = jnp.asarray(_alibi_slopes(D_HEADS), jnp.float32)
    bias = (-LOG2E) * slopes[:, None, None] * (delta * dil).astype(jnp.float32)[None]
    return jnp.where(((delta >= 0) & (delta <= D_SPAN))[None], bias, NEG)


def _dilated_fused(q, k, v, *, bsz, seq):
    nblk = seq // DBLK
    tabs = jnp.stack([_dilated_bias_table(d).reshape(D_HEADS // 2, 2 * SUB, SUB + D_SPAN)
                      for d in D_DILATIONS])
    blk_spec = pl.BlockSpec((DBLK, D_WIDTH), lambda b, i: (b * nblk + i, 0))
    group_buf = lambda n: pltpu.VMEM((n, GRP, D_WIDTH), jnp.bfloat16)
    return pl.pallas_call(
        _dilated_fused_kernel,
        grid=(bsz, nblk),
        in_specs=[
            blk_spec, blk_spec, blk_spec,
            pl.BlockSpec((4, GRP, GRP), lambda b, i: (0, 0, 0), pipeline_mode=pl.Buffered(1)),
            pl.BlockSpec(tabs.shape, lambda b, i: (0, 0, 0, 0), pipeline_mode=pl.Buffered(1)),
        ],
        out_specs=blk_spec,
        out_shape=jax.ShapeDtypeStruct((bsz * seq, D_WIDTH), jnp.bfloat16),
        scratch_shapes=[
            pltpu.VMEM((D_SPAN + DBLK, D_WIDTH), jnp.bfloat16),
            pltpu.VMEM((D_SPAN + DBLK, D_WIDTH), jnp.bfloat16),
            group_buf(2 + NG), group_buf(2 + NG),
            group_buf(2 * NG), group_buf(2 * NG),
            group_buf(NG),
            pltpu.VMEM((2, NG, GRP, D_WIDTH), jnp.bfloat16),
            pltpu.VMEM((2, NG, GRP, D_WIDTH), jnp.bfloat16),
        ],
        compiler_params=pltpu.CompilerParams(
            dimension_semantics=("arbitrary",) * 2, vmem_limit_bytes=VMEM_LIMIT),
        name="dilated_attention_fused",
    )(q, k, v, _class_permutations(), tabs)


def _dilated_kernel(q_ref, kh_ref, kc_ref, vh_ref, vc_ref, bias_ref, o_ref, lse_ref,
                    kbuf, vbuf, *, tl):
    j = pl.program_id(2)
    kbuf[0:D_SPAN, :] = kh_ref[...]
    kbuf[D_SPAN:, :] = kc_ref[...]
    vbuf[0:D_SPAN, :] = vh_ref[...]
    vbuf[D_SPAN:, :] = vc_ref[...]
    lane = lax.broadcasted_iota(jnp.int32, (1, LANES), 1)
    kk = lax.broadcasted_iota(jnp.int32, (1, SUB + D_SPAN), 1)
    halo_pen = jnp.where(kk < D_SPAN, jnp.where(j == 0, NEG, 0.0), 0.0)
    for qs in range(tl // SUB):
        rows = slice(qs * SUB, (qs + 1) * SUB)
        krows = slice(qs * SUB, (qs + 1) * SUB + D_SPAN)
        lse_tile = jnp.zeros((SUB, LANES), jnp.float32)
        for p in range(D_HEADS // 2):
            cols = slice(p * LANES, (p + 1) * LANES)
            q = q_ref[rows, cols]
            kp = kbuf[krows, cols]
            vp = vbuf[krows, cols]
            zero = jnp.zeros_like(q)
            halves = []
            for e in range(2):
                hd = 2 * p + e
                qm = jnp.where((lane < HEAD_DIM) if e == 0 else (lane >= HEAD_DIM), q, zero)
                s = _dot_nt(qm, kp) + bias_ref[hd]
                if qs == 0:
                    s = s + halo_pen
                m = jnp.max(s, axis=-1, keepdims=True)
                pe = jnp.exp2(s - m)
                l = jnp.sum(pe, axis=-1, keepdims=True)
                halves.append(_dot(pe.astype(jnp.bfloat16), vp) / l)
                lse_tile = jnp.where(lane == hd, m + jnp.log2(l), lse_tile)
            o_ref[rows, cols] = jnp.where(lane < HEAD_DIM, halves[0], halves[1]).astype(jnp.bfloat16)
        lse_ref[rows, :] = lse_tile


def _dilated_branch(q, k, v, *, bsz, seq, dil):
    tl = TL
    length = seq // dil
    nb = length // tl
    halo_per_blk = tl // D_SPAN
    view = lambda a: a.reshape(bsz, length, dil * D_WIDTH)
    cur = lambda b, r, j: (b, j, r)
    halo = lambda b, r, j: (b, jnp.maximum(j * halo_per_blk - 1, 0), r)
    o, lse = pl.pallas_call(
        functools.partial(_dilated_kernel, tl=tl),
        grid=(bsz, dil, nb),
        in_specs=[
            pl.BlockSpec((None, tl, D_WIDTH), cur),
            pl.BlockSpec((None, D_SPAN, D_WIDTH), halo),
            pl.BlockSpec((None, tl, D_WIDTH), cur),
            pl.BlockSpec((None, D_SPAN, D_WIDTH), halo),
            pl.BlockSpec((None, tl, D_WIDTH), cur),
            pl.BlockSpec((D_HEADS, SUB, SUB + D_SPAN), lambda b, r, j: (0, 0, 0)),
        ],
        out_specs=[
            pl.BlockSpec((None, tl, D_WIDTH), cur),
            pl.BlockSpec((None, tl, LANES), cur),
        ],
        out_shape=[
            jax.ShapeDtypeStruct((bsz, length, dil * D_WIDTH), jnp.bfloat16),
            jax.ShapeDtypeStruct((bsz, length, dil * LANES), jnp.float32),
        ],
        scratch_shapes=[pltpu.VMEM((tl + D_SPAN, D_WIDTH), jnp.bfloat16)] * 2,
        compiler_params=pltpu.CompilerParams(
            dimension_semantics=("arbitrary",) * 3, vmem_limit_bytes=VMEM_LIMIT),
        name=f"dilated_attention_d{dil}",
    )(view(q), view(k), view(k), view(v), view(v), _dilated_bias_table(dil))
    return o.reshape(bsz * seq, D_WIDTH), lse.reshape(bsz * seq, LANES)


def _branch_combine_kernel(o1_ref, o2_ref, o3_ref, l1_ref, l2_ref, l3_ref, y_ref):
    lses = [l1_ref[...], l2_ref[...], l3_ref[...]]
    mx = jnp.maximum(jnp.maximum(lses[0], lses[1]), lses[2])
    es = [jnp.exp2(l - mx) for l in lses]
    inv = 1.0 / (es[0] + es[1] + es[2])
    ws = [e * inv for e in es]
    lane = lax.broadcasted_iota(jnp.int32, (1, LANES), 1)
    for p in range(D_HEADS // 2):
        cols = slice(p * LANES, (p + 1) * LANES)
        yd = None
        for w, o in zip(ws, (o1_ref, o2_ref, o3_ref)):
            wf = jnp.where(lane < HEAD_DIM, w[:, 2 * p:2 * p + 1], w[:, 2 * p + 1:2 * p + 2])
            term = wf * o[:, cols].astype(jnp.float32)
            yd = term if yd is None else yd + term
        y_ref[:, cols] = yd.astype(jnp.bfloat16)


def _dilated_by_branch(q, k, v, *, bsz, seq):
    outs, lses = zip(*[_dilated_branch(q, k, v, bsz=bsz, seq=seq, dil=d) for d in D_DILATIONS])
    tm = TM_PROJ
    row = lambda i: (i, 0)
    return pl.pallas_call(
        _branch_combine_kernel,
        grid=(bsz * seq // tm,),
        in_specs=[pl.BlockSpec((tm, D_WIDTH), row)] * 3 + [pl.BlockSpec((tm, LANES), row)] * 3,
        out_specs=pl.BlockSpec((tm, D_WIDTH), row),
        out_shape=jax.ShapeDtypeStruct((bsz * seq, D_WIDTH), jnp.bfloat16),
        compiler_params=pltpu.CompilerParams(
            dimension_semantics=("arbitrary",), vmem_limit_bytes=VMEM_LIMIT),
        name="dilated_branch_combine",
    )(*outs, *lses)


def kernel(x, norm_mix, norm_ffn, ev_w_in, ev_w_out, ev_q_gain, ev_k_gain, ev_lambda_q1, ev_lambda_k1, ev_lambda_q2, ev_lambda_k2, ev_subln_gain, ev_pool_w, ev_pool_scale, od_w_in, od_w_out, od_conv_w, od_q_gain, od_k_gain, ffn_w_gate, ffn_w_up, ffn_w_down):
    bsz, seq, _ = x.shape
    bf = jnp.bfloat16
    f32 = jnp.float32
    q_scale = HEAD_DIM ** -0.5 * LOG2E
    x2d = x.reshape(bsz * seq, D_MODEL)
    grp = lax.broadcasted_iota(jnp.int32, (MXU_DEPTH, MXU_DEPTH), 0) // HEAD_DIM
    gsum = jnp.where(grp == grp.T, 1.0 / HEAD_DIM, 0.0).astype(bf)
    pw = ev_pool_w[0].astype(bf)
    zpw = jnp.zeros_like(pw[0])
    pool_w = jnp.stack([jnp.block([[pw[2 * i], zpw], [zpw, pw[2 * i + 1]]]) for i in range(2)])
    tile8 = lambda gain, s: (jnp.tile(gain.astype(f32), 8) * s).reshape(1, 512)

    w = ev_w_in[0]
    regroup = lambda c: c.reshape(D_MODEL, 2, A_HEADS, HEAD_DIM).transpose(0, 2, 1, 3).reshape(D_MODEL, 512)
    w_even = jnp.concatenate([regroup(w[:, 0:512]), regroup(w[:, 512:1024]), w[:, 1024:]], axis=1).astype(bf)
    q1, q2, k1, k2, v, yb = _even_in(
        x2d, norm_mix[0].reshape(1, D_MODEL).astype(f32), w_even, gsum,
        tile8(ev_q_gain[0], q_scale), tile8(ev_k_gain[0], 1.0),
        pool_w, ev_pool_scale[0].reshape(1, 512).astype(f32), seq=seq)
    lam_vecs = jnp.stack([ev_lambda_q1[0], ev_lambda_k1[0], ev_lambda_q2[0], ev_lambda_k2[0]]).astype(f32)
    attn = functools.partial(_diff_attn, q1, q2, k1, k2, v, lam_vecs, bsz=bsz, seq=seq)
    ya = lax.cond(_score_bound(ev_q_gain[0], ev_k_gain[0]) <= MAX_UNSHIFTED_SCORE,
                  lambda: attn(online=False), lambda: attn(online=True))
    ffn_w = (ffn_w_gate.astype(bf), ffn_w_up.astype(bf), ffn_w_down.astype(bf))
    x2d = _post(x2d, ya, yb, ev_w_out[0].astype(bf),
                norm_ffn[0].reshape(1, D_MODEL).astype(f32), *ffn_w, 0, "even_post",
                subnorm_gain=(ev_subln_gain[0].astype(f32) * (1.0 - LAMBDA_INIT_L0)).reshape(1, A_V))

    yc, q, k, v = _odd_in(
        x2d, norm_mix[1].reshape(1, D_MODEL).astype(f32), od_w_in[0].astype(bf), gsum,
        tile8(od_q_gain[0], q_scale), tile8(od_k_gain[0], 1.0), od_conv_w[0].astype(f32), seq=seq)
    yd = lax.cond(_score_bound(od_q_gain[0], od_k_gain[0]) <= MAX_UNSHIFTED_SCORE,
                  lambda: _dilated_fused(q, k, v, bsz=bsz, seq=seq),
                  lambda: _dilated_by_branch(q, k, v, bsz=bsz, seq=seq))
    x2d = _post(x2d, yc, yd, od_w_out[0].astype(bf),
                norm_ffn[1].reshape(1, D_MODEL).astype(f32), *ffn_w, 1, "odd_post")
    return x2d.reshape(bsz, seq, D_MODEL)
```

```python
import functools
import math

import numpy as np
import jax
import jax.numpy as jnp
from jax import lax
from jax.experimental import pallas as pl
from jax.experimental.pallas import tpu as pltpu

D_MODEL = 1024
HEAD_DIM = 64
EPS = 1e-6
A_HEADS = 4
A_V = 128
B_WINDOWS = (2, 4, 8, 16)
POOL_HALO = 16
CONV_HALO = 8
D_HEADS = 8
D_WIDTH = D_HEADS * HEAD_DIM
D_DILATIONS = (1, 4, 16)
D_SPAN = 128
D_FF = 2816
LAMBDA_INIT_L0 = 0.8 - 0.6 * math.exp(-0.3 * 0)
NEG = -1e30
LOG2E = math.log2(math.e)
LOG2E_HI = float(np.float32(LOG2E).astype(jnp.bfloat16))
LOG2E_LO = float(np.float32(LOG2E - LOG2E_HI).astype(jnp.bfloat16))
MAX_UNSHIFTED_SCORE = 60.0

LANES = 128
MXU_DEPTH = 256
VMEM_LIMIT = 56 * 1024 * 1024

TM_PROJ = 512
TM_IN = 1024
SM_IN = 512
TQ = 2048
TK = 512
TK_DIAG = 256
TL = 512
SUB = 128
DBLK = 2048
GRP = 256
NG = DBLK // GRP

_NT = (((1,), (1,)), ((), ()))


def _dot(a, b):
    return jnp.dot(a, b, preferred_element_type=jnp.float32)


def _dot_nt(a, b):
    return lax.dot_general(a, b, _NT, preferred_element_type=jnp.float32)


def _rms_scale(x):
    return lax.rsqrt(jnp.mean(x * x, axis=-1, keepdims=True) + EPS)


def _group_rmsnorm(t, gsum_ref, gain):
    sq = (t * t).astype(jnp.bfloat16)
    depth = gsum_ref.shape[0]
    msq = jnp.concatenate([_dot(sq[:, c:c + depth], gsum_ref[...])
                           for c in range(0, t.shape[1], depth)], axis=1)
    return t * lax.rsqrt(msq + EPS) * gain


def _alibi_slopes(n):
    return [2.0 ** (-8.0 * (i + 1) / n) for i in range(n)]


def _alibi_lanes(pos, lane, first):
    rel = lane - first
    hi = ((pos >> 8) << 8).astype(jnp.float32)
    lo = (pos & 255).astype(jnp.float32)
    pos_sel = jnp.where((rel & 1) == 0, hi, lo)
    l_sel = jnp.where((rel & 3) < 2, LOG2E_HI, LOG2E_LO)
    first4 = (rel >= 0) & (rel < 4)
    last4 = (rel >= 4) & (rel < 8)
    return (jnp.where(first4, l_sel, jnp.where(last4, -pos_sel, 0.0)),
            jnp.where(first4, pos_sel, jnp.where(last4, l_sel, 0.0)))


def _shift_rows(ext, d):
    return pltpu.roll(ext, d, axis=0)


def _with_ones(v):
    return jnp.concatenate([v, jnp.ones_like(v)], axis=1)


def _score_bound(q_gain, k_gain):
    return (HEAD_DIM ** 0.5) * jnp.max(jnp.abs(q_gain)) * jnp.max(jnp.abs(k_gain))


def _even_in_kernel(x_ref, g_ref, w_ref, gsum_ref, qg_ref, kg_ref, pw_ref, ps_ref,
                    q1_ref, q2_ref, k1_ref, k2_ref, v_ref, yb_ref, carry_ref,
                    *, tm, sm, blocks_per_seq):
    blk = pl.program_id(0) % blocks_per_seq

    @pl.when(blk == 0)
    def _():
        carry_ref[...] = jnp.zeros_like(carry_ref)

    for sb in range(tm // sm):
        rows = slice(sb * sm, (sb + 1) * sm)
        x = x_ref[rows, :]
        h = (x * _rms_scale(x) * g_ref[...]).astype(jnp.bfloat16)
        proj = lambda part: _dot(h, w_ref[:, part * 512:(part + 1) * 512])
        qn = _group_rmsnorm(proj(0), gsum_ref, qg_ref[...])
        kn = _group_rmsnorm(proj(1), gsum_ref, kg_ref[...])
        v_ref[rows, :] = proj(2).astype(jnp.bfloat16)

        pos = lax.broadcasted_iota(jnp.int32, (sm, 1), 0) + (blk * tm + sb * sm)
        lane = lax.broadcasted_iota(jnp.int32, (1, LANES), 1)
        low = lane < HEAD_DIM
        for mp, (q_out, k_out) in enumerate(((q1_ref, k1_ref), (q2_ref, k2_ref))):
            q_lanes, k_lanes = _alibi_lanes(pos, lane, HEAD_DIM if mp == 0 else 0)
            data = low if mp == 0 else jnp.logical_not(low)
            for hd, slope in enumerate(_alibi_slopes(A_HEADS)):
                cols = slice(hd * LANES, (hd + 1) * LANES)
                q_out[rows, cols] = jnp.where(data, qn[:, cols], slope * q_lanes).astype(jnp.bfloat16)
                k_out[rows, cols] = jnp.where(data, kn[:, cols], k_lanes).astype(jnp.bfloat16)

        u = proj(3)
        ext = jnp.concatenate([carry_ref[...], u], axis=0)
        pos1 = (pos + 1).astype(jnp.float32)
        pooled = []
        for g, w in enumerate(B_WINDOWS):
            cols = slice(g * LANES, (g + 1) * LANES)
            s = ext[:, cols]
            d = 1
            while d < w:
                s = s + _shift_rows(s, d)
                d *= 2
            pooled.append((s[POOL_HALO:, :] / jnp.minimum(pos1, float(w)) - u[:, cols]).astype(jnp.bfloat16))
        for pg in range(len(B_WINDOWS) // 2):
            cols = slice(2 * pg * LANES, (2 * pg + 2) * LANES)
            yb = _dot(jnp.concatenate(pooled[2 * pg:2 * pg + 2], axis=1), pw_ref[pg]) * ps_ref[:, cols]
            yb_ref[rows, cols] = yb.astype(jnp.bfloat16)
        carry_ref[...] = u[sm - POOL_HALO:, :]


def _even_in(x2d, g, w, gsum, qg, kg, pw, ps, *, seq):
    t = x2d.shape[0]
    tm = TM_IN
    const = lambda i: (0, 0)
    row = lambda i: (i, 0)
    out = jax.ShapeDtypeStruct((t, 512), jnp.bfloat16)
    return pl.pallas_call(
        functools.partial(_even_in_kernel, tm=tm, sm=SM_IN, blocks_per_seq=seq // tm),
        grid=(t // tm,),
        in_specs=[
            pl.BlockSpec((tm, D_MODEL), row),
            pl.BlockSpec((1, D_MODEL), const),
            pl.BlockSpec((D_MODEL, 2048), const),
            pl.BlockSpec((MXU_DEPTH, MXU_DEPTH), const),
            pl.BlockSpec((1, 512), const),
            pl.BlockSpec((1, 512), const),
            pl.BlockSpec((2, MXU_DEPTH, MXU_DEPTH), lambda i: (0, 0, 0)),
            pl.BlockSpec((1, 512), const),
        ],
        out_specs=[pl.BlockSpec((tm, 512), row)] * 6,
        out_shape=[out] * 6,
        scratch_shapes=[pltpu.VMEM((POOL_HALO, 512), jnp.float32)],
        compiler_params=pltpu.CompilerParams(
            dimension_semantics=("arbitrary",), vmem_limit_bytes=VMEM_LIMIT),
        name="even_in_proj",
    )(x2d, g, w, gsum, qg, kg, pw, ps)


def _diff_attn_kernel(q1_ref, q2_ref, k1_ref, k2_ref, v_ref, lam_ref, o_ref, acc_ref,
                      *, tq, tk, td, online):
    n = tq // tk
    qi = pl.program_id(2)
    qs = (q1_ref[...], q2_ref[...])
    k_refs = (k1_ref, k2_ref)

    def key_rows(j, size):
        return pl.ds(pl.multiple_of(j * size, size), size)

    def scores(j, q_list, first_key=None, size=tk):
        out = []
        for q, k_ref in zip(q_list, k_refs):
            s = _dot_nt(q, k_ref[key_rows(j, size), :])
            if first_key is not None:
                ii = lax.broadcasted_iota(jnp.int32, s.shape, 0)
                jj = lax.broadcasted_iota(jnp.int32, s.shape, 1)
                s = jnp.where(jj + first_key > ii, NEG, s)
            out.append(s)
        return out

    def pv(j, ps, size=tk):
        v1 = _with_ones(v_ref[key_rows(j, size), :])
        return [_dot(p, v1) for p in ps]

    if online:
        def tile(j, state, first_key=None):
            out = []
            for mp, s in enumerate(scores(j, qs, first_key)):
                m_new = jnp.max(s, axis=-1, keepdims=True)
                if state is not None:
                    m_new = jnp.maximum(state[2 * mp], m_new)
                out += [m_new, jnp.exp2(s - m_new).astype(jnp.bfloat16)]
            accs = pv(j, out[1::2])
            if state is not None:
                accs = [jnp.exp2(state[2 * mp] - out[2 * mp]) * state[2 * mp + 1] + accs[mp]
                        for mp in range(2)]
            return (out[0], accs[0], out[2], accs[1])

        state = None
        for t in range(n):
            state = tile(n * qi + t, state, t * tk)
        _, acc1, _, acc2 = lax.fori_loop(0, n * qi, tile, state)
    else:
        def tile(j, q_list, first_key=None, size=tk):
            ps = [jnp.exp2(s).astype(jnp.bfloat16) for s in scores(j, q_list, first_key, size)]
            return pv(j, ps, size)

        for t in range(tq // td):
            for mp, part in enumerate(tile((tq // td) * qi + t, [q[t * td:] for q in qs], 0, td)):
                if t == 0:
                    acc_ref[mp] = part
                else:
                    acc_ref[mp, t * td:, :] += part

        def n_tiles(i, carry):
            total = tile(n * i, qs)
            for t in range(1, n):
                total = [a + b for a, b in zip(total, tile(n * i + t, qs))]
            for mp in range(2):
                acc_ref[mp] += total[mp]
            return carry

        lax.fori_loop(0, qi, n_tiles, 0)
        acc1, acc2 = acc_ref[0], acc_ref[1]

    lv = lam_ref[...]
    lam = (jnp.exp(jnp.sum(lv[0:1] * lv[1:2], axis=-1, keepdims=True))
           - jnp.exp(jnp.sum(lv[2:3] * lv[3:4], axis=-1, keepdims=True)) + LAMBDA_INIT_L0)
    o = acc1[:, :A_V] / acc1[:, A_V:] - lam * (acc2[:, :A_V] / acc2[:, A_V:])
    o_ref[...] = o.astype(jnp.bfloat16)


def _diff_attn(q1, q2, k1, k2, v, lam_vecs, *, bsz, seq, online):
    tq = TQ
    nq = seq // tq
    q_spec = pl.BlockSpec((tq, LANES), lambda b, h, i: (b * nq + i, h))
    kv_spec = pl.BlockSpec((seq, LANES), lambda b, h, i: (b, h))
    return pl.pallas_call(
        functools.partial(_diff_attn_kernel, tq=tq, tk=TK, td=TK_DIAG, online=online),
        grid=(bsz, A_HEADS, nq),
        in_specs=[
            q_spec, q_spec, kv_spec, kv_spec, kv_spec,
            pl.BlockSpec((4, HEAD_DIM), lambda b, h, i: (0, 0)),
        ],
        out_specs=q_spec,
        out_shape=jax.ShapeDtypeStruct((bsz * seq, A_HEADS * A_V), jnp.bfloat16),
        scratch_shapes=[pltpu.VMEM((2, tq, 2 * A_V), jnp.float32)],
        compiler_params=pltpu.CompilerParams(
            dimension_semantics=("arbitrary",) * 3, vmem_limit_bytes=VMEM_LIMIT),
        name="diff_attention_online" if online else "diff_attention",
    )(q1, q2, k1, k2, v, lam_vecs)


def _post_kernel(x_ref, ya_ref, yb_ref, wo_ref, g_ref, wg_ref, wu_ref, wd_ref, *rest):
    o_ref = rest[-1]
    half = ya_ref.shape[1]
    ya = ya_ref[...]
    if len(rest) == 2:
        heads = []
        for hd in range(half // A_V):
            o = ya[:, hd * A_V:(hd + 1) * A_V].astype(jnp.float32)
            heads.append((o * _rms_scale(o) * rest[0][...]).astype(jnp.bfloat16))
        ya = jnp.concatenate(heads, axis=1)
    x1 = x_ref[...] + _dot(ya, wo_ref[0:half, :]) + _dot(yb_ref[...], wo_ref[half:, :])
    h = (x1 * _rms_scale(x1) * g_ref[...]).astype(jnp.bfloat16)
    gate = _dot(h, wg_ref[...])
    up = _dot(h, wu_ref[...])
    act = (gate * jax.nn.sigmoid(gate) * up).astype(jnp.bfloat16)
    o_ref[...] = x1 + _dot(act, wd_ref[...])


def _post(x2d, ya, yb, wo, g, wg, wu, wd, layer, name, subnorm_gain=None):
    t = x2d.shape[0]
    tm = TM_PROJ
    row = lambda i: (i, 0)
    const = lambda i: (0, 0)
    resident = functools.partial(pl.BlockSpec, index_map=const, pipeline_mode=pl.Buffered(1))
    ffn_w = lambda rows, cols: pl.BlockSpec((None, rows, cols), lambda i: (layer, 0, 0),
                                            pipeline_mode=pl.Buffered(1))
    extra = [] if subnorm_gain is None else [subnorm_gain]
    return pl.pallas_call(
        _post_kernel,
        grid=(t // tm,),
        in_specs=[
            pl.BlockSpec((tm, D_MODEL), row),
            pl.BlockSpec((tm, ya.shape[1]), row),
            pl.BlockSpec((tm, yb.shape[1]), row),
            resident((D_MODEL, D_MODEL)),
            pl.BlockSpec((1, D_MODEL), const),
            ffn_w(D_MODEL, D_FF),
            ffn_w(D_MODEL, D_FF),
            ffn_w(D_FF, D_MODEL),
        ] + [pl.BlockSpec((1, A_V), const)] * len(extra),
        out_specs=pl.BlockSpec((tm, D_MODEL), row),
        out_shape=jax.ShapeDtypeStruct((t, D_MODEL), jnp.float32),
        compiler_params=pltpu.CompilerParams(
            dimension_semantics=("arbitrary",), vmem_limit_bytes=VMEM_LIMIT),
        name=name,
    )(x2d, ya, yb, wo, g, wg, wu, wd, *extra)


def _odd_in_kernel(x_ref, g_ref, w_ref, gsum_ref, qg_ref, kg_ref, cw_ref,
                   yc_ref, q_ref, k_ref, v_ref, carry_ref, *, tm, sm, blocks_per_seq):
    blk = pl.program_id(0) % blocks_per_seq

    @pl.when(blk == 0)
    def _():
        carry_ref[...] = jnp.zeros_like(carry_ref)

    for sb in range(tm // sm):
        rows = slice(sb * sm, (sb + 1) * sm)
        x = x_ref[rows, :]
        h = (x * _rms_scale(x) * g_ref[...]).astype(jnp.bfloat16)
        proj = lambda part: _dot(h, w_ref[:, part * 512:(part + 1) * 512])
        z = proj(1) * proj(2)
        bg = proj(0)
        ext = jnp.concatenate([carry_ref[...], z], axis=0)
        z1 = _shift_rows(ext, 1)[CONV_HALO:, :]
        z2 = _shift_rows(ext, 2)[CONV_HALO:, :]
        cw = cw_ref[...]
        yc = bg * (cw[0:1] * z2 + cw[1:2] * z1 + cw[2:3] * z)
        yc_ref[rows, :] = yc.astype(jnp.bfloat16)
        carry_ref[...] = z[sm - CONV_HALO:, :]

        q_ref[rows, :] = _group_rmsnorm(proj(3), gsum_ref, qg_ref[...]).astype(jnp.bfloat16)
        k_ref[rows, :] = _group_rmsnorm(proj(4), gsum_ref, kg_ref[...]).astype(jnp.bfloat16)
        v_ref[rows, :] = proj(5).astype(jnp.bfloat16)


def _odd_in(x2d, g, w, gsum, qg, kg, cw, *, seq):
    t = x2d.shape[0]
    tm = TM_IN
    const = lambda i: (0, 0)
    row = lambda i: (i, 0)
    out = jax.ShapeDtypeStruct((t, 512), jnp.bfloat16)
    return pl.pallas_call(
        functools.partial(_odd_in_kernel, tm=tm, sm=SM_IN, blocks_per_seq=seq // tm),
        grid=(t // tm,),
        in_specs=[
            pl.BlockSpec((tm, D_MODEL), row),
            pl.BlockSpec((1, D_MODEL), const),
            pl.BlockSpec((D_MODEL, 3072), const),
            pl.BlockSpec((MXU_DEPTH, MXU_DEPTH), const),
            pl.BlockSpec((1, 512), const),
            pl.BlockSpec((1, 512), const),
            pl.BlockSpec((3, 512), const),
        ],
        out_specs=[pl.BlockSpec((tm, 512), row)] * 4,
        out_shape=[out] * 4,
        scratch_shapes=[pltpu.VMEM((CONV_HALO, 512), jnp.float32)],
        compiler_params=pltpu.CompilerParams(
            dimension_semantics=("arbitrary",), vmem_limit_bytes=VMEM_LIMIT),
        name="odd_in_proj",
    )(x2d, g, w, gsum, qg, kg, cw)


def _pair_tile(q, kp, vp, bias):
    lane = lax.broadcasted_iota(jnp.int32, (1, LANES), 1)
    low = lane < HEAD_DIM
    zero = jnp.zeros_like(q)
    q2 = jnp.concatenate([jnp.where(low, q, zero), jnp.where(low, zero, q)], axis=0)
    p = jnp.exp2(_dot_nt(q2, kp) + bias).astype(jnp.bfloat16)
    r = _dot(p, _with_ones(vp))
    num = jnp.where(low, r[:SUB, :LANES], r[SUB:, :LANES])
    den = jnp.where(low, r[:SUB, LANES:], r[SUB:, LANES:])
    return num, den


def _dilated_fused_kernel(q_ref, k_ref, v_ref, perm_ref, tab_ref, o_ref,
                          kn, vn, k4, v4, k16, v16, qp, res4, res16):
    blk = pl.program_id(1)
    first = blk == 0
    cur = (blk & 1) * NG
    prev = NG - cur
    bf = jnp.bfloat16

    @pl.when(first)
    def _():
        kn[0:D_SPAN, :] = jnp.zeros((D_SPAN, D_WIDTH), bf)
        vn[0:D_SPAN, :] = jnp.zeros((D_SPAN, D_WIDTH), bf)
        k4[0:2] = jnp.zeros((2, GRP, D_WIDTH), bf)
        v4[0:2] = jnp.zeros((2, GRP, D_WIDTH), bf)
        k16[pl.ds(prev, NG)] = jnp.zeros((NG, GRP, D_WIDTH), bf)
        v16[pl.ds(prev, NG)] = jnp.zeros((NG, GRP, D_WIDTH), bf)

    @pl.when(jnp.logical_not(first))
    def _():
        kn[0:D_SPAN, :] = kn[DBLK:DBLK + D_SPAN, :]
        vn[0:D_SPAN, :] = vn[DBLK:DBLK + D_SPAN, :]
        k4[0:2] = k4[NG:NG + 2]
        v4[0:2] = v4[NG:NG + 2]

    kn[D_SPAN:, :] = k_ref[...]
    vn[D_SPAN:, :] = v_ref[...]
    for g in range(NG):
        rows = slice(g * GRP, (g + 1) * GRP)
        for src, d4, d16 in ((k_ref, k4, k16), (v_ref, v4, v16)):
            x = src[rows, :]
            d4[2 + g] = _dot(perm_ref[0], x).astype(bf)
            d16[cur + g] = _dot(perm_ref[2], x).astype(bf)

    kk = lax.broadcasted_iota(jnp.int32, (1, 2 * SUB), 1)
    no_halo = jnp.where(kk < D_SPAN, jnp.where(first, NEG, 0.0), 0.0)

    def pair_cols(p):
        return slice(p * LANES, (p + 1) * LANES)

    for g in range(NG):
        qp[g] = _dot(perm_ref[2], q_ref[g * GRP:(g + 1) * GRP, :]).astype(bf)

    def classes16(i, carry):
        for t in range(8):
            r = pl.ds(pl.multiple_of(i * 128 + t * 16, 16), 16)
            for p in range(D_HEADS // 2):
                cols = pair_cols(p)
                q = jnp.concatenate([qp[g, r, cols] for g in range(NG)], axis=0)
                kp = jnp.concatenate([k16[prev + g, r, cols] for g in range(NG)]
                                     + [k16[cur + g, r, cols] for g in range(NG)], axis=0)
                vp = jnp.concatenate([v16[prev + g, r, cols] for g in range(NG)]
                                     + [v16[cur + g, r, cols] for g in range(NG)], axis=0)
                num, den = _pair_tile(q, kp, vp, tab_ref[2, p] + no_halo)
                for g in range(NG):
                    res16[0, g, r, cols] = num[16 * g:16 * g + 16].astype(bf)
                    res16[1, g, r, cols] = den[16 * g:16 * g + 16].astype(bf)
        return carry

    lax.fori_loop(0, 2, classes16, 0)

    for g in range(NG):
        qp[g] = _dot(perm_ref[0], q_ref[g * GRP:(g + 1) * GRP, :]).astype(bf)

    def classes4(i, carry):
        for t in range(2):
            r = pl.ds(pl.multiple_of(i * 128 + t * 64, 64), 64)
            for qs in range(NG // 2):
                for p in range(D_HEADS // 2):
                    cols = pair_cols(p)
                    q = jnp.concatenate([qp[2 * qs + j, r, cols] for j in range(2)], axis=0)
                    kp = jnp.concatenate([k4[2 * qs + j, r, cols] for j in range(4)], axis=0)
                    vp = jnp.concatenate([v4[2 * qs + j, r, cols] for j in range(4)], axis=0)
                    bias = tab_ref[1, p] + no_halo if qs == 0 else tab_ref[1, p]
                    num, den = _pair_tile(q, kp, vp, bias)
                    for j in range(2):
                        res4[0, 2 * qs + j, r, cols] = num[64 * j:64 * j + 64].astype(bf)
                        res4[1, 2 * qs + j, r, cols] = den[64 * j:64 * j + 64].astype(bf)
        return carry

    lax.fori_loop(0, 2, classes4, 0)

    def groups(i, carry):
        for u in range(2):
            g = 2 * i + u
            num = _dot(perm_ref[1], res4[0, g]) + _dot(perm_ref[3], res16[0, g])
            den = _dot(perm_ref[1], res4[1, g]) + _dot(perm_ref[3], res16[1, g])
            for t in range(GRP // SUB):
                q_rows = pl.ds(pl.multiple_of(g * GRP + t * SUB, SUB), SUB)
                k_rows = pl.ds(pl.multiple_of(g * GRP + t * SUB, SUB), 2 * SUB)
                sub = slice(t * SUB, (t + 1) * SUB)
                for p in range(D_HEADS // 2):
                    cols = pair_cols(p)
                    bias = tab_ref[0, p]
                    if u == 0 and t == 0:
                        bias = bias + jnp.where(i == 0, no_halo, 0.0)
                    n1, d1 = _pair_tile(q_ref[q_rows, cols], kn[k_rows, cols], vn[k_rows, cols], bias)
                    o_ref[q_rows, cols] = ((n1 + num[sub, cols]) / (d1 + den[sub, cols])).astype(bf)
        return carry

    lax.fori_loop(0, NG // 2, groups, 0)


def _class_permutations():
    mats = []
    out = np.arange(GRP)
    for d in (4, 16):
        per = GRP // d
        p = np.zeros((GRP, GRP), np.float32)
        p[out, d * (out % per) + out // per] = 1.0
        mats += [p, p.T]
    return jnp.asarray(np.stack(mats), jnp.bfloat16)


def _dilated_bias_table(dil):
    ii = lax.broadcasted_iota(jnp.int32, (SUB, SUB + D_SPAN), 0)
    kk = lax.broadcasted_iota(jnp.int32, (SUB, SUB + D_SPAN), 1)
    delta = D_SPAN + ii - kk
    slopes = jnp.asarray(_alibi_slopes(D_HEADS), jnp.float32)
    bias = (-LOG2E) * slopes[:, None, None] * (delta * dil).astype(jnp.float32)[None]
    return jnp.where(((delta >= 0) & (delta <= D_SPAN))[None], bias, NEG)


def _dilated_fused(q, k, v, *, bsz, seq):
    nblk = seq // DBLK
    tabs = jnp.stack([_dilated_bias_table(d).reshape(D_HEADS // 2, 2 * SUB, SUB + D_SPAN)
                      for d in D_DILATIONS])
    blk_spec = pl.BlockSpec((DBLK, D_WIDTH), lambda b, i: (b * nblk + i, 0))
    group_buf = lambda n: pltpu.VMEM((n, GRP, D_WIDTH), jnp.bfloat16)
    return pl.pallas_call(
        _dilated_fused_kernel,
        grid=(bsz, nblk),
        in_specs=[
            blk_spec, blk_spec, blk_spec,
            pl.BlockSpec((4, GRP, GRP), lambda b, i: (0, 0, 0), pipeline_mode=pl.Buffered(1)),
            pl.BlockSpec(tabs.shape, lambda b, i: (0, 0, 0, 0), pipeline_mode=pl.Buffered(1)),
        ],
        out_specs=blk_spec,
        out_shape=jax.ShapeDtypeStruct((bsz * seq, D_WIDTH), jnp.bfloat16),
        scratch_shapes=[
            pltpu.VMEM((D_SPAN + DBLK, D_WIDTH), jnp.bfloat16),
            pltpu.VMEM((D_SPAN + DBLK, D_WIDTH), jnp.bfloat16),
            group_buf(2 + NG), group_buf(2 + NG),
            group_buf(2 * NG), group_buf(2 * NG),
            group_buf(NG),
            pltpu.VMEM((2, NG, GRP, D_WIDTH), jnp.bfloat16),
            pltpu.VMEM((2, NG, GRP, D_WIDTH), jnp.bfloat16),
        ],
        compiler_params=pltpu.CompilerParams(
            dimension_semantics=("arbitrary",) * 2, vmem_limit_bytes=VMEM_LIMIT),
        name="dilated_attention_fused",
    )(q, k, v, _class_permutations(), tabs)


def _dilated_kernel(q_ref, kh_ref, kc_ref, vh_ref, vc_ref, bias_ref, o_ref, lse_ref,
                    kbuf, vbuf, *, tl):
    j = pl.program_id(2)
    kbuf[0:D_SPAN, :] = kh_ref[...]
    kbuf[D_SPAN:, :] = kc_ref[...]
    vbuf[0:D_SPAN, :] = vh_ref[...]
    vbuf[D_SPAN:, :] = vc_ref[...]
    lane = lax.broadcasted_iota(jnp.int32, (1, LANES), 1)
    kk = lax.broadcasted_iota(jnp.int32, (1, SUB + D_SPAN), 1)
    halo_pen = jnp.where(kk < D_SPAN, jnp.where(j == 0, NEG, 0.0), 0.0)
    for qs in range(tl // SUB):
        rows = slice(qs * SUB, (qs + 1) * SUB)
        krows = slice(qs * SUB, (qs + 1) * SUB + D_SPAN)
        lse_tile = jnp.zeros((SUB, LANES), jnp.float32)
        for p in range(D_HEADS // 2):
            cols = slice(p * LANES, (p + 1) * LANES)
            q = q_ref[rows, cols]
            kp = kbuf[krows, cols]
            vp = vbuf[krows, cols]
            zero = jnp.zeros_like(q)
            halves = []
            for e in range(2):
                hd = 2 * p + e
                qm = jnp.where((lane < HEAD_DIM) if e == 0 else (lane >= HEAD_DIM), q, zero)
                s = _dot_nt(qm, kp) + bias_ref[hd]
                if qs == 0:
                    s = s + halo_pen
                m = jnp.max(s, axis=-1, keepdims=True)
                pe = jnp.exp2(s - m)
                l = jnp.sum(pe, axis=-1, keepdims=True)
                halves.append(_dot(pe.astype(jnp.bfloat16), vp) / l)
                lse_tile = jnp.where(lane == hd, m + jnp.log2(l), lse_tile)
            o_ref[rows, cols] = jnp.where(lane < HEAD_DIM, halves[0], halves[1]).astype(jnp.bfloat16)
        lse_ref[rows, :] = lse_tile


def _dilated_branch(q, k, v, *, bsz, seq, dil):
    tl = TL
    length = seq // dil
    nb = length // tl
    halo_per_blk = tl // D_SPAN
    view = lambda a: a.reshape(bsz, length, dil * D_WIDTH)
    cur = lambda b, r, j: (b, j, r)
    halo = lambda b, r, j: (b, jnp.maximum(j * halo_per_blk - 1, 0), r)
    o, lse = pl.pallas_call(
        functools.partial(_dilated_kernel, tl=tl),
        grid=(bsz, dil, nb),
        in_specs=[
            pl.BlockSpec((None, tl, D_WIDTH), cur),
            pl.BlockSpec((None, D_SPAN, D_WIDTH), halo),
            pl.BlockSpec((None, tl, D_WIDTH), cur),
            pl.BlockSpec((None, D_SPAN, D_WIDTH), halo),
            pl.BlockSpec((None, tl, D_WIDTH), cur),
            pl.BlockSpec((D_HEADS, SUB, SUB + D_SPAN), lambda b, r, j: (0, 0, 0)),
        ],
        out_specs=[
            pl.BlockSpec((None, tl, D_WIDTH), cur),
            pl.BlockSpec((None, tl, LANES), cur),
        ],
        out_shape=[
            jax.ShapeDtypeStruct((bsz, length, dil * D_WIDTH), jnp.bfloat16),
            jax.ShapeDtypeStruct((bsz, length, dil * LANES), jnp.float32),
        ],
        scratch_shapes=[pltpu.VMEM((tl + D_SPAN, D_WIDTH), jnp.bfloat16)] * 2,
        compiler_params=pltpu.CompilerParams(
            dimension_semantics=("arbitrary",) * 3, vmem_limit_bytes=VMEM_LIMIT),
        name=f"dilated_attention_d{dil}",
    )(view(q), view(k), view(k), view(v), view(v), _dilated_bias_table(dil))
    return o.reshape(bsz * seq, D_WIDTH), lse.reshape(bsz * seq, LANES)


def _branch_combine_kernel(o1_ref, o2_ref, o3_ref, l1_ref, l2_ref, l3_ref, y_ref):
    lses = [l1_ref[...], l2_ref[...], l3_ref[...]]
    mx = jnp.maximum(jnp.maximum(lses[0], lses[1]), lses[2])
    es = [jnp.exp2(l - mx) for l in lses]
    inv = 1.0 / (es[0] + es[1] + es[2])
    ws = [e * inv for e in es]
    lane = lax.broadcasted_iota(jnp.int32, (1, LANES), 1)
    for p in range(D_HEADS // 2):
        cols = slice(p * LANES, (p + 1) * LANES)
        yd = None
        for w, o in zip(ws, (o1_ref, o2_ref, o3_ref)):
            wf = jnp.where(lane < HEAD_DIM, w[:, 2 * p:2 * p + 1], w[:, 2 * p + 1:2 * p + 2])
            term = wf * o[:, cols].astype(jnp.float32)
            yd = term if yd is None else yd + term
        y_ref[:, cols] = yd.astype(jnp.bfloat16)


def _dilated_by_branch(q, k, v, *, bsz, seq):
    outs, lses = zip(*[_dilated_branch(q, k, v, bsz=bsz, seq=seq, dil=d) for d in D_DILATIONS])
    tm = TM_PROJ
    row = lambda i: (i, 0)
    return pl.pallas_call(
        _branch_combine_kernel,
        grid=(bsz * seq // tm,),
        in_specs=[pl.BlockSpec((tm, D_WIDTH), row)] * 3 + [pl.BlockSpec((tm, LANES), row)] * 3,
        out_specs=pl.BlockSpec((tm, D_WIDTH), row),
        out_shape=jax.ShapeDtypeStruct((bsz * seq, D_WIDTH), jnp.bfloat16),
        compiler_params=pltpu.CompilerParams(
            dimension_semantics=("arbitrary",), vmem_limit_bytes=VMEM_LIMIT),
        name="dilated_branch_combine",
    )(*outs, *lses)


def kernel(x, norm_mix, norm_ffn, ev_w_in, ev_w_out, ev_q_gain, ev_k_gain, ev_lambda_q1, ev_lambda_k1, ev_lambda_q2, ev_lambda_k2, ev_subln_gain, ev_pool_w, ev_pool_scale, od_w_in, od_w_out, od_conv_w, od_q_gain, od_k_gain, ffn_w_gate, ffn_w_up, ffn_w_down):
    bsz, seq, _ = x.shape
    bf = jnp.bfloat16
    f32 = jnp.float32
    q_scale = HEAD_DIM ** -0.5 * LOG2E
    x2d = x.reshape(bsz * seq, D_MODEL)
    grp = lax.broadcasted_iota(jnp.int32, (MXU_DEPTH, MXU_DEPTH), 0) // HEAD_DIM
    gsum = jnp.where(grp == grp.T, 1.0 / HEAD_DIM, 0.0).astype(bf)
    pw = ev_pool_w[0].astype(bf)
    zpw = jnp.zeros_like(pw[0])
    pool_w = jnp.stack([jnp.block([[pw[2 * i], zpw], [zpw, pw[2 * i + 1]]]) for i in range(2)])
    tile8 = lambda gain, s: (jnp.tile(gain.astype(f32), 8) * s).reshape(1, 512)

    w = ev_w_in[0]
    regroup = lambda c: c.reshape(D_MODEL, 2, A_HEADS, HEAD_DIM).transpose(0, 2, 1, 3).reshape(D_MODEL, 512)
    w_even = jnp.concatenate([regroup(w[:, 0:512]), regroup(w[:, 512:1024]), w[:, 1024:]], axis=1).astype(bf)
    q1, q2, k1, k2, v, yb = _even_in(
        x2d, norm_mix[0].reshape(1, D_MODEL).astype(f32), w_even, gsum,
        tile8(ev_q_gain[0], q_scale), tile8(ev_k_gain[0], 1.0),
        pool_w, ev_pool_scale[0].reshape(1, 512).astype(f32), seq=seq)
    lam_vecs = jnp.stack([ev_lambda_q1[0], ev_lambda_k1[0], ev_lambda_q2[0], ev_lambda_k2[0]]).astype(f32)
    attn = functools.partial(_diff_attn, q1, q2, k1, k2, v, lam_vecs, bsz=bsz, seq=seq)
    ya = lax.cond(_score_bound(ev_q_gain[0], ev_k_gain[0]) <= MAX_UNSHIFTED_SCORE,
                  lambda: attn(online=False), lambda: attn(online=True))
    ffn_w = (ffn_w_gate.astype(bf), ffn_w_up.astype(bf), ffn_w_down.astype(bf))
    x2d = _post(x2d, ya, yb, ev_w_out[0].astype(bf),
                norm_ffn[0].reshape(1, D_MODEL).astype(f32), *ffn_w, 0, "even_post",
                subnorm_gain=(ev_subln_gain[0].astype(f32) * (1.0 - LAMBDA_INIT_L0)).reshape(1, A_V))

    yc, q, k, v = _odd_in(
        x2d, norm_mix[1].reshape(1, D_MODEL).astype(f32), od_w_in[0].astype(bf), gsum,
        tile8(od_q_gain[0], q_scale), tile8(od_k_gain[0], 1.0), od_conv_w[0].astype(f32), seq=seq)
    yd = lax.cond(_score_bound(od_q_gain[0], od_k_gain[0]) <= MAX_UNSHIFTED_SCORE,
                  lambda: _dilated_fused(q, k, v, bsz=bsz, seq=seq),
                  lambda: _dilated_by_branch(q, k, v, bsz=bsz, seq=seq))
    x2d = _post(x2d, yc, yd, od_w_out[0].astype(bf),
                norm_ffn[1].reshape(1, D_MODEL).astype(f32), *ffn_w, 1, "odd_post")
    return x2d.reshape(bsz, seq, D_MODEL)
```

```python
import functools
import math

import numpy as np
import jax
import jax.numpy as jnp
from jax import lax
from jax.experimental import pallas as pl
from jax.experimental.pallas import tpu as pltpu

D_MODEL = 1024
HEAD_DIM = 64
EPS = 1e-6
A_HEADS = 4
A_V = 128
B_WINDOWS = (2, 4, 8, 16)
POOL_HALO = 16
CONV_HALO = 8
D_HEADS = 8
D_WIDTH = D_HEADS * HEAD_DIM
D_DILATIONS = (1, 4, 16)
D_SPAN = 128
D_FF = 2816
LAMBDA_INIT_L0 = 0.8 - 0.6 * math.exp(-0.3 * 0)
NEG = -1e30
LOG2E = math.log2(math.e)
LOG2E_HI = float(np.float32(LOG2E).astype(jnp.bfloat16))
LOG2E_LO = float(np.float32(LOG2E - LOG2E_HI).astype(jnp.bfloat16))
MAX_UNSHIFTED_SCORE = 60.0

LANES = 128
MXU_DEPTH = 256
VMEM_LIMIT = 56 * 1024 * 1024

TM_PROJ = 512
TM_IN = 1024
SM_IN = 512
TQ = 2048
TK = 512
TK_DIAG = 256
TL = 512
SUB = 128
DBLK = 2048
GRP = 256
NG = DBLK // GRP

_NT = (((1,), (1,)), ((), ()))


def _dot(a, b):
    return jnp.dot(a, b, preferred_element_type=jnp.float32)


def _dot_nt(a, b):
    return lax.dot_general(a, b, _NT, preferred_element_type=jnp.float32)


def _rms_scale(x):
    return lax.rsqrt(jnp.mean(x * x, axis=-1, keepdims=True) + EPS)


def _group_rmsnorm(t, gsum_ref, gain):
    sq = (t * t).astype(jnp.bfloat16)
    depth = gsum_ref.shape[0]
    msq = jnp.concatenate([_dot(sq[:, c:c + depth], gsum_ref[...])
                           for c in range(0, t.shape[1], depth)], axis=1)
    return t * lax.rsqrt(msq + EPS) * gain


def _alibi_slopes(n):
    return [2.0 ** (-8.0 * (i + 1) / n) for i in range(n)]


def _alibi_lanes(pos, lane, first):
    rel = lane - first
    hi = ((pos >> 8) << 8).astype(jnp.float32)
    lo = (pos & 255).astype(jnp.float32)
    pos_sel = jnp.where((rel & 1) == 0, hi, lo)
    l_sel = jnp.where((rel & 3) < 2, LOG2E_HI, LOG2E_LO)
    first4 = (rel >= 0) & (rel < 4)
    last4 = (rel >= 4) & (rel < 8)
    return (jnp.where(first4, l_sel, jnp.where(last4, -pos_sel, 0.0)),
            jnp.where(first4, pos_sel, jnp.where(last4, l_sel, 0.0)))


def _shift_rows(ext, d):
    return pltpu.roll(ext, d, axis=0)


def _with_ones(v):
    return jnp.concatenate([v, jnp.ones_like(v)], axis=1)


def _score_bound(q_gain, k_gain):
    return (HEAD_DIM ** 0.5) * jnp.max(jnp.abs(q_gain)) * jnp.max(jnp.abs(k_gain))


def _even_in_kernel(x_ref, g_ref, w_ref, gsum_ref, qg_ref, kg_ref, pw_ref, ps_ref,
                    q1_ref, q2_ref, k1_ref, k2_ref, v_ref, yb_ref, carry_ref,
                    *, tm, sm, blocks_per_seq):
    blk = pl.program_id(0) % blocks_per_seq

    @pl.when(blk == 0)
    def _():
        carry_ref[...] = jnp.zeros_like(carry_ref)

    for sb in range(tm // sm):
        rows = slice(sb * sm, (sb + 1) * sm)
        x = x_ref[rows, :]
        h = (x * _rms_scale(x) * g_ref[...]).astype(jnp.bfloat16)
        proj = lambda part: _dot(h, w_ref[:, part * 512:(part + 1) * 512])
        qn = _group_rmsnorm(proj(0), gsum_ref, qg_ref[...])
        kn = _group_rmsnorm(proj(1), gsum_ref, kg_ref[...])
        v_ref[rows, :] = proj(2).astype(jnp.bfloat16)

        pos = lax.broadcasted_iota(jnp.int32, (sm, 1), 0) + (blk * tm + sb * sm)
        lane = lax.broadcasted_iota(jnp.int32, (1, LANES), 1)
        low = lane < HEAD_DIM
        for mp, (q_out, k_out) in enumerate(((q1_ref, k1_ref), (q2_ref, k2_ref))):
            q_lanes, k_lanes = _alibi_lanes(pos, lane, HEAD_DIM if mp == 0 else 0)
            data = low if mp == 0 else jnp.logical_not(low)
            for hd, slope in enumerate(_alibi_slopes(A_HEADS)):
                cols = slice(hd * LANES, (hd + 1) * LANES)
                q_out[rows, cols] = jnp.where(data, qn[:, cols], slope * q_lanes).astype(jnp.bfloat16)
                k_out[rows, cols] = jnp.where(data, kn[:, cols], k_lanes).astype(jnp.bfloat16)

        u = proj(3)
        ext = jnp.concatenate([carry_ref[...], u], axis=0)
        pos1 = (pos + 1).astype(jnp.float32)
        pooled = []
        for g, w in enumerate(B_WINDOWS):
            cols = slice(g * LANES, (g + 1) * LANES)
            s = ext[:, cols]
            d = 1
            while d < w:
                s = s + _shift_rows(s, d)
                d *= 2
            pooled.append((s[POOL_HALO:, :] / jnp.minimum(pos1, float(w)) - u[:, cols]).astype(jnp.bfloat16))
        for pg in range(len(B_WINDOWS) // 2):
            cols = slice(2 * pg * LANES, (2 * pg + 2) * LANES)
            yb = _dot(jnp.concatenate(pooled[2 * pg:2 * pg + 2], axis=1), pw_ref[pg]) * ps_ref[:, cols]
            yb_ref[rows, cols] = yb.astype(jnp.bfloat16)
        carry_ref[...] = u[sm - POOL_HALO:, :]


def _even_in(x2d, g, w, gsum, qg, kg, pw, ps, *, seq):
    t = x2d.shape[0]
    tm = TM_IN
    const = lambda i: (0, 0)
    row = lambda i: (i, 0)
    out = jax.ShapeDtypeStruct((t, 512), jnp.bfloat16)
    return pl.pallas_call(
        functools.partial(_even_in_kernel, tm=tm, sm=SM_IN, blocks_per_seq=seq // tm),
        grid=(t // tm,),
        in_specs=[
            pl.BlockSpec((tm, D_MODEL), row),
            pl.BlockSpec((1, D_MODEL), const),
            pl.BlockSpec((D_MODEL, 2048), const),
            pl.BlockSpec((MXU_DEPTH, MXU_DEPTH), const),
            pl.BlockSpec((1, 512), const),
            pl.BlockSpec((1, 512), const),
            pl.BlockSpec((2, MXU_DEPTH, MXU_DEPTH), lambda i: (0, 0, 0)),
            pl.BlockSpec((1, 512), const),
        ],
        out_specs=[pl.BlockSpec((tm, 512), row)] * 6,
        out_shape=[out] * 6,
        scratch_shapes=[pltpu.VMEM((POOL_HALO, 512), jnp.float32)],
        compiler_params=pltpu.CompilerParams(
            dimension_semantics=("arbitrary",), vmem_limit_bytes=VMEM_LIMIT),
        name="even_in_proj",
    )(x2d, g, w, gsum, qg, kg, pw, ps)


def _diff_attn_kernel(q1_ref, q2_ref, k1_ref, k2_ref, v_ref, lam_ref, *refs,
                      tq, tk, td, n_cast, online):
    o_ref, acc_ref = refs[n_cast], refs[-1]
    for src, dst in zip(refs[:n_cast], refs[n_cast + 1:-1]):
        dst[...] = src[...].astype(jnp.bfloat16)

    n = tq // tk
    qi = pl.program_id(2)
    qs = (q1_ref[...], q2_ref[...])
    k_refs = (k1_ref, k2_ref)

    def key_rows(j, size):
        return pl.ds(pl.multiple_of(j * size, size), size)

    def scores(j, q_list, first_key=None, size=tk):
        out = []
        for q, k_ref in zip(q_list, k_refs):
            s = _dot_nt(q, k_ref[key_rows(j, size), :])
            if first_key is not None:
                ii = lax.broadcasted_iota(jnp.int32, s.shape, 0)
                jj = lax.broadcasted_iota(jnp.int32, s.shape, 1)
                s = jnp.where(jj + first_key > ii, NEG, s)
            out.append(s)
        return out

    def pv(j, ps, size=tk):
        v1 = _with_ones(v_ref[key_rows(j, size), :])
        return [_dot(p, v1) for p in ps]

    if online:
        def tile(j, state, first_key=None):
            out = []
            for mp, s in enumerate(scores(j, qs, first_key)):
                m_new = jnp.max(s, axis=-1, keepdims=True)
                if state is not None:
                    m_new = jnp.maximum(state[2 * mp], m_new)
                out += [m_new, jnp.exp2(s - m_new).astype(jnp.bfloat16)]
            accs = pv(j, out[1::2])
            if state is not None:
                accs = [jnp.exp2(state[2 * mp] - out[2 * mp]) * state[2 * mp + 1] + accs[mp]
                        for mp in range(2)]
            return (out[0], accs[0], out[2], accs[1])

        state = None
        for t in range(n):
            state = tile(n * qi + t, state, t * tk)
        _, acc1, _, acc2 = lax.fori_loop(0, n * qi, tile, state)
    else:
        def tile(j, q_list, first_key=None, size=tk):
            ps = [jnp.exp2(s).astype(jnp.bfloat16) for s in scores(j, q_list, first_key, size)]
            return pv(j, ps, size)

        for t in range(tq // td):
            for mp, part in enumerate(tile((tq // td) * qi + t, [q[t * td:] for q in qs], 0, td)):
                if t == 0:
                    acc_ref[mp] = part
                else:
                    acc_ref[mp, t * td:, :] += part

        def n_tiles(i, carry):
            total = tile(n * i, qs)
            for t in range(1, n):
                total = [a + b for a, b in zip(total, tile(n * i + t, qs))]
            for mp in range(2):
                acc_ref[mp] += total[mp]
            return carry

        lax.fori_loop(0, qi, n_tiles, 0)
        acc1, acc2 = acc_ref[0], acc_ref[1]

    lv = lam_ref[...]
    lam = (jnp.exp(jnp.sum(lv[0:1] * lv[1:2], axis=-1, keepdims=True))
           - jnp.exp(jnp.sum(lv[2:3] * lv[3:4], axis=-1, keepdims=True)) + LAMBDA_INIT_L0)
    o = acc1[:, :A_V] / acc1[:, A_V:] - lam * (acc2[:, :A_V] / acc2[:, A_V:])
    o_ref[...] = o.astype(jnp.bfloat16)


def _diff_attn(q1, q2, k1, k2, v, lam_vecs, weights, *, bsz, seq, online):
    tq = TQ
    nq = seq // tq
    steps = bsz * A_HEADS * nq
    q_spec = pl.BlockSpec((tq, LANES), lambda b, h, i: (b * nq + i, h))
    kv_spec = pl.BlockSpec((seq, LANES), lambda b, h, i: (b, h))

    def cast_spec(w):
        nblk = max(n for n in range(1, steps + 1)
                   if steps % n == 0 and w.shape[0] % (16 * n) == 0)
        return pl.BlockSpec((w.shape[0] // nblk, w.shape[1]),
                            lambda b, h, i: (((b * A_HEADS + h) * nq + i) // (steps // nblk), 0))

    cast_specs = [cast_spec(w) for w in weights]
    out = pl.pallas_call(
        functools.partial(_diff_attn_kernel, tq=tq, tk=TK, td=TK_DIAG, n_cast=len(weights),
                          online=online),
        grid=(bsz, A_HEADS, nq),
        in_specs=[
            q_spec, q_spec, kv_spec, kv_spec, kv_spec,
            pl.BlockSpec((4, HEAD_DIM), lambda b, h, i: (0, 0)),
        ] + cast_specs,
        out_specs=[q_spec] + cast_specs,
        out_shape=[jax.ShapeDtypeStruct((bsz * seq, A_HEADS * A_V), jnp.bfloat16)]
        + [jax.ShapeDtypeStruct(w.shape, jnp.bfloat16) for w in weights],
        scratch_shapes=[pltpu.VMEM((2, tq, 2 * A_V), jnp.float32)],
        compiler_params=pltpu.CompilerParams(
            dimension_semantics=("arbitrary",) * 3, vmem_limit_bytes=VMEM_LIMIT),
        name="diff_attention_online" if online else "diff_attention",
    )(q1, q2, k1, k2, v, lam_vecs, *weights)
    return out[0], out[1:]


def _post_kernel(x_ref, ya_ref, yb_ref, wo_ref, g_ref, wg_ref, wu_ref, wd_ref, *rest):
    o_ref = rest[-1]
    half = ya_ref.shape[1]
    ya = ya_ref[...]
    if len(rest) == 2:
        heads = []
        for hd in range(half // A_V):
            o = ya[:, hd * A_V:(hd + 1) * A_V].astype(jnp.float32)
            heads.append((o * _rms_scale(o) * rest[0][...]).astype(jnp.bfloat16))
        ya = jnp.concatenate(heads, axis=1)
    x1 = x_ref[...] + _dot(ya, wo_ref[0:half, :]) + _dot(yb_ref[...], wo_ref[half:, :])
    h = (x1 * _rms_scale(x1) * g_ref[...]).astype(jnp.bfloat16)
    gate = _dot(h, wg_ref[...])
    up = _dot(h, wu_ref[...])
    act = (gate * jax.nn.sigmoid(gate) * up).astype(jnp.bfloat16)
    o_ref[...] = x1 + _dot(act, wd_ref[...])


def _post(x2d, ya, yb, wo, g, wg, wu, wd, layer, name, subnorm_gain=None):
    t = x2d.shape[0]
    tm = TM_PROJ
    row = lambda i: (i, 0)
    const = lambda i: (0, 0)
    resident = functools.partial(pl.BlockSpec, index_map=const, pipeline_mode=pl.Buffered(1))
    ffn_w = lambda rows, cols: pl.BlockSpec((None, rows, cols), lambda i: (layer, 0, 0),
                                            pipeline_mode=pl.Buffered(1))
    extra = [] if subnorm_gain is None else [subnorm_gain]
    return pl.pallas_call(
        _post_kernel,
        grid=(t // tm,),
        in_specs=[
            pl.BlockSpec((tm, D_MODEL), row),
            pl.BlockSpec((tm, ya.shape[1]), row),
            pl.BlockSpec((tm, yb.shape[1]), row),
            resident((D_MODEL, D_MODEL)),
            pl.BlockSpec((1, D_MODEL), const),
            ffn_w(D_MODEL, D_FF),
            ffn_w(D_MODEL, D_FF),
            ffn_w(D_FF, D_MODEL),
        ] + [pl.BlockSpec((1, A_V), const)] * len(extra),
        out_specs=pl.BlockSpec((tm, D_MODEL), row),
        out_shape=jax.ShapeDtypeStruct((t, D_MODEL), jnp.float32),
        compiler_params=pltpu.CompilerParams(
            dimension_semantics=("arbitrary",), vmem_limit_bytes=VMEM_LIMIT),
        name=name,
    )(x2d, ya, yb, wo, g, wg, wu, wd, *extra)


def _odd_in_kernel(x_ref, g_ref, w_ref, gsum_ref, qg_ref, kg_ref, cw_ref,
                   yc_ref, q_ref, k_ref, v_ref, carry_ref, *, tm, sm, blocks_per_seq):
    blk = pl.program_id(0) % blocks_per_seq

    @pl.when(blk == 0)
    def _():
        carry_ref[...] = jnp.zeros_like(carry_ref)

    for sb in range(tm // sm):
        rows = slice(sb * sm, (sb + 1) * sm)
        x = x_ref[rows, :]
        h = (x * _rms_scale(x) * g_ref[...]).astype(jnp.bfloat16)
        proj = lambda part: _dot(h, w_ref[:, part * 512:(part + 1) * 512])
        z = proj(1) * proj(2)
        bg = proj(0)
        ext = jnp.concatenate([carry_ref[...], z], axis=0)
        z1 = _shift_rows(ext, 1)[CONV_HALO:, :]
        z2 = _shift_rows(ext, 2)[CONV_HALO:, :]
        cw = cw_ref[...]
        yc = bg * (cw[0:1] * z2 + cw[1:2] * z1 + cw[2:3] * z)
        yc_ref[rows, :] = yc.astype(jnp.bfloat16)
        carry_ref[...] = z[sm - CONV_HALO:, :]

        q_ref[rows, :] = _group_rmsnorm(proj(3), gsum_ref, qg_ref[...]).astype(jnp.bfloat16)
        k_ref[rows, :] = _group_rmsnorm(proj(4), gsum_ref, kg_ref[...]).astype(jnp.bfloat16)
        v_ref[rows, :] = proj(5).astype(jnp.bfloat16)


def _odd_in(x2d, g, w, gsum, qg, kg, cw, *, seq):
    t = x2d.shape[0]
    tm = TM_IN
    const = lambda i: (0, 0)
    row = lambda i: (i, 0)
    out = jax.ShapeDtypeStruct((t, 512), jnp.bfloat16)
    return pl.pallas_call(
        functools.partial(_odd_in_kernel, tm=tm, sm=SM_IN, blocks_per_seq=seq // tm),
        grid=(t // tm,),
        in_specs=[
            pl.BlockSpec((tm, D_MODEL), row),
            pl.BlockSpec((1, D_MODEL), const),
            pl.BlockSpec((D_MODEL, 3072), const),
            pl.BlockSpec((MXU_DEPTH, MXU_DEPTH), const),
            pl.BlockSpec((1, 512), const),
            pl.BlockSpec((1, 512), const),
            pl.BlockSpec((3, 512), const),
        ],
        out_specs=[pl.BlockSpec((tm, 512), row)] * 4,
        out_shape=[out] * 4,
        scratch_shapes=[pltpu.VMEM((CONV_HALO, 512), jnp.float32)],
        compiler_params=pltpu.CompilerParams(
            dimension_semantics=("arbitrary",), vmem_limit_bytes=VMEM_LIMIT),
        name="odd_in_proj",
    )(x2d, g, w, gsum, qg, kg, cw)


def _pair_tile(q, kp, vp, bias):
    lane = lax.broadcasted_iota(jnp.int32, (1, LANES), 1)
    low = lane < HEAD_DIM
    zero = jnp.zeros_like(q)
    q2 = jnp.concatenate([jnp.where(low, q, zero), jnp.where(low, zero, q)], axis=0)
    p = jnp.exp2(_dot_nt(q2, kp) + bias).astype(jnp.bfloat16)
    r = _dot(p, _with_ones(vp))
    num = jnp.where(low, r[:SUB, :LANES], r[SUB:, :LANES])
    den = jnp.where(low, r[:SUB, LANES:], r[SUB:, LANES:])
    return num, den


def _dilated_fused_kernel(q_ref, k_ref, v_ref, perm_ref, tab_ref, o_ref,
                          kn, vn, k4, v4, k16, v16, qp, res4, res16):
    blk = pl.program_id(1)
    first = blk == 0
    cur = (blk & 1) * NG
    prev = NG - cur
    bf = jnp.bfloat16

    @pl.when(first)
    def _():
        kn[0:D_SPAN, :] = jnp.zeros((D_SPAN, D_WIDTH), bf)
        vn[0:D_SPAN, :] = jnp.zeros((D_SPAN, D_WIDTH), bf)
        k4[0:2] = jnp.zeros((2, GRP, D_WIDTH), bf)
        v4[0:2] = jnp.zeros((2, GRP, D_WIDTH), bf)
        k16[pl.ds(prev, NG)] = jnp.zeros((NG, GRP, D_WIDTH), bf)
        v16[pl.ds(prev, NG)] = jnp.zeros((NG, GRP, D_WIDTH), bf)

    @pl.when(jnp.logical_not(first))
    def _():
        kn[0:D_SPAN, :] = kn[DBLK:DBLK + D_SPAN, :]
        vn[0:D_SPAN, :] = vn[DBLK:DBLK + D_SPAN, :]
        k4[0:2] = k4[NG:NG + 2]
        v4[0:2] = v4[NG:NG + 2]

    kn[D_SPAN:, :] = k_ref[...]
    vn[D_SPAN:, :] = v_ref[...]
    for g in range(NG):
        rows = slice(g * GRP, (g + 1) * GRP)
        for src, d4, d16 in ((k_ref, k4, k16), (v_ref, v4, v16)):
            x = src[rows, :]
            d4[2 + g] = _dot(perm_ref[0], x).astype(bf)
            d16[cur + g] = _dot(perm_ref[2], x).astype(bf)

    kk = lax.broadcasted_iota(jnp.int32, (1, 2 * SUB), 1)
    no_halo = jnp.where(kk < D_SPAN, jnp.where(first, NEG, 0.0), 0.0)

    def pair_cols(p):
        return slice(p * LANES, (p + 1) * LANES)

    for g in range(NG):
        qp[g] = _dot(perm_ref[2], q_ref[g * GRP:(g + 1) * GRP, :]).astype(bf)

    def classes16(i, carry):
        for t in range(8):
            r = pl.ds(pl.multiple_of(i * 128 + t * 16, 16), 16)
            for p in range(D_HEADS // 2):
                cols = pair_cols(p)
                q = jnp.concatenate([qp[g, r, cols] for g in range(NG)], axis=0)
                kp = jnp.concatenate([k16[prev + g, r, cols] for g in range(NG)]
                                     + [k16[cur + g, r, cols] for g in range(NG)], axis=0)
                vp = jnp.concatenate([v16[prev + g, r, cols] for g in range(NG)]
                                     + [v16[cur + g, r, cols] for g in range(NG)], axis=0)
                num, den = _pair_tile(q, kp, vp, tab_ref[2, p] + no_halo)
                for g in range(NG):
                    res16[0, g, r, cols] = num[16 * g:16 * g + 16].astype(bf)
                    res16[1, g, r, cols] = den[16 * g:16 * g + 16].astype(bf)
        return carry

    lax.fori_loop(0, 2, classes16, 0)

    for g in range(NG):
        qp[g] = _dot(perm_ref[0], q_ref[g * GRP:(g + 1) * GRP, :]).astype(bf)

    def classes4(i, carry):
        for t in range(2):
            r = pl.ds(pl.multiple_of(i * 128 + t * 64, 64), 64)
            for qs in range(NG // 2):
                for p in range(D_HEADS // 2):
                    cols = pair_cols(p)
                    q = jnp.concatenate([qp[2 * qs + j, r, cols] for j in range(2)], axis=0)
                    kp = jnp.concatenate([k4[2 * qs + j, r, cols] for j in range(4)], axis=0)
                    vp = jnp.concatenate([v4[2 * qs + j, r, cols] for j in range(4)], axis=0)
                    bias = tab_ref[1, p] + no_halo if qs == 0 else tab_ref[1, p]
                    num, den = _pair_tile(q, kp, vp, bias)
                    for j in range(2):
                        res4[0, 2 * qs + j, r, cols] = num[64 * j:64 * j + 64].astype(bf)
                        res4[1, 2 * qs + j, r, cols] = den[64 * j:64 * j + 64].astype(bf)
        return carry

    lax.fori_loop(0, 2, classes4, 0)

    def groups(i, carry):
        for u in range(2):
            g = 2 * i + u
            num = _dot(perm_ref[1], res4[0, g]) + _dot(perm_ref[3], res16[0, g])
            den = _dot(perm_ref[1], res4[1, g]) + _dot(perm_ref[3], res16[1, g])
            for t in range(GRP // SUB):
                q_rows = pl.ds(pl.multiple_of(g * GRP + t * SUB, SUB), SUB)
                k_rows = pl.ds(pl.multiple_of(g * GRP + t * SUB, SUB), 2 * SUB)
                sub = slice(t * SUB, (t + 1) * SUB)
                for p in range(D_HEADS // 2):
                    cols = pair_cols(p)
                    bias = tab_ref[0, p]
                    if u == 0 and t == 0:
                        bias = bias + jnp.where(i == 0, no_halo, 0.0)
                    n1, d1 = _pair_tile(q_ref[q_rows, cols], kn[k_rows, cols], vn[k_rows, cols], bias)
                    o_ref[q_rows, cols] = ((n1 + num[sub, cols]) / (d1 + den[sub, cols])).astype(bf)
        return carry

    lax.fori_loop(0, NG // 2, groups, 0)


def _class_permutations():
    mats = []
    out = np.arange(GRP)
    for d in (4, 16):
        per = GRP // d
        p = np.zeros((GRP, GRP), np.float32)
        p[out, d * (out % per) + out // per] = 1.0
        mats += [p, p.T]
    return jnp.asarray(np.stack(mats), jnp.bfloat16)


def _dilated_bias_table(dil):
    ii = lax.broadcasted_iota(jnp.int32, (SUB, SUB + D_SPAN), 0)
    kk = lax.broadcasted_iota(jnp.int32, (SUB, SUB + D_SPAN), 1)
    delta = D_SPAN + ii - kk
    slopes = jnp.asarray(_alibi_slopes(D_HEADS), jnp.float32)
    bias = (-LOG2E) * slopes[:, None, None] * (delta * dil).astype(jnp.float32)[None]
    return jnp.where(((delta >= 0) & (delta <= D_SPAN))[None], bias, NEG)


def _dilated_fused(q, k, v, *, bsz, seq):
    nblk = seq // DBLK
    tabs = jnp.stack([_dilated_bias_table(d).reshape(D_HEADS // 2, 2 * SUB, SUB + D_SPAN)
                      for d in D_DILATIONS])
    blk_spec = pl.BlockSpec((DBLK, D_WIDTH), lambda b, i: (b * nblk + i, 0))
    group_buf = lambda n: pltpu.VMEM((n, GRP, D_WIDTH), jnp.bfloat16)
    return pl.pallas_call(
        _dilated_fused_kernel,
        grid=(bsz, nblk),
        in_specs=[
            blk_spec, blk_spec, blk_spec,
            pl.BlockSpec((4, GRP, GRP), lambda b, i: (0, 0, 0), pipeline_mode=pl.Buffered(1)),
            pl.BlockSpec(tabs.shape, lambda b, i: (0, 0, 0, 0), pipeline_mode=pl.Buffered(1)),
        ],
        out_specs=blk_spec,
        out_shape=jax.ShapeDtypeStruct((bsz * seq, D_WIDTH), jnp.bfloat16),
        scratch_shapes=[
            pltpu.VMEM((D_SPAN + DBLK, D_WIDTH), jnp.bfloat16),
            pltpu.VMEM((D_SPAN + DBLK, D_WIDTH), jnp.bfloat16),
            group_buf(2 + NG), group_buf(2 + NG),
            group_buf(2 * NG), group_buf(2 * NG),
            group_buf(NG),
            pltpu.VMEM((2, NG, GRP, D_WIDTH), jnp.bfloat16),
            pltpu.VMEM((2, NG, GRP, D_WIDTH), jnp.bfloat16),
        ],
        compiler_params=pltpu.CompilerParams(
            dimension_semantics=("arbitrary",) * 2, vmem_limit_bytes=VMEM_LIMIT),
        name="dilated_attention_fused",
    )(q, k, v, _class_permutations(), tabs)


def _dilated_kernel(q_ref, kh_ref, kc_ref, vh_ref, vc_ref, bias_ref, o_ref, lse_ref,
                    kbuf, vbuf, *, tl):
    j = pl.program_id(2)
    kbuf[0:D_SPAN, :] = kh_ref[...]
    kbuf[D_SPAN:, :] = kc_ref[...]
    vbuf[0:D_SPAN, :] = vh_ref[...]
    vbuf[D_SPAN:, :] = vc_ref[...]
    lane = lax.broadcasted_iota(jnp.int32, (1, LANES), 1)
    kk = lax.broadcasted_iota(jnp.int32, (1, SUB + D_SPAN), 1)
    halo_pen = jnp.where(kk < D_SPAN, jnp.where(j == 0, NEG, 0.0), 0.0)
    for qs in range(tl // SUB):
        rows = slice(qs * SUB, (qs + 1) * SUB)
        krows = slice(qs * SUB, (qs + 1) * SUB + D_SPAN)
        lse_tile = jnp.zeros((SUB, LANES), jnp.float32)
        for p in range(D_HEADS // 2):
            cols = slice(p * LANES, (p + 1) * LANES)
            q = q_ref[rows, cols]
            kp = kbuf[krows, cols]
            vp = vbuf[krows, cols]
            zero = jnp.zeros_like(q)
            halves = []
            for e in range(2):
                hd = 2 * p + e
                qm = jnp.where((lane < HEAD_DIM) if e == 0 else (lane >= HEAD_DIM), q, zero)
                s = _dot_nt(qm, kp) + bias_ref[hd]
                if qs == 0:
                    s = s + halo_pen
                m = jnp.max(s, axis=-1, keepdims=True)
                pe = jnp.exp2(s - m)
                l = jnp.sum(pe, axis=-1, keepdims=True)
                halves.append(_dot(pe.astype(jnp.bfloat16), vp) / l)
                lse_tile = jnp.where(lane == hd, m + jnp.log2(l), lse_tile)
            o_ref[rows, cols] = jnp.where(lane < HEAD_DIM, halves[0], halves[1]).astype(jnp.bfloat16)
        lse_ref[rows, :] = lse_tile


def _dilated_branch(q, k, v, *, bsz, seq, dil):
    tl = TL
    length = seq // dil
    nb = length // tl
    halo_per_blk = tl // D_SPAN
    view = lambda a: a.reshape(bsz, length, dil * D_WIDTH)
    cur = lambda b, r, j: (b, j, r)
    halo = lambda b, r, j: (b, jnp.maximum(j * halo_per_blk - 1, 0), r)
    o, lse = pl.pallas_call(
        functools.partial(_dilated_kernel, tl=tl),
        grid=(bsz, dil, nb),
        in_specs=[
            pl.BlockSpec((None, tl, D_WIDTH), cur),
            pl.BlockSpec((None, D_SPAN, D_WIDTH), halo),
            pl.BlockSpec((None, tl, D_WIDTH), cur),
            pl.BlockSpec((None, D_SPAN, D_WIDTH), halo),
            pl.BlockSpec((None, tl, D_WIDTH), cur),
            pl.BlockSpec((D_HEADS, SUB, SUB + D_SPAN), lambda b, r, j: (0, 0, 0)),
        ],
        out_specs=[
            pl.BlockSpec((None, tl, D_WIDTH), cur),
            pl.BlockSpec((None, tl, LANES), cur),
        ],
        out_shape=[
            jax.ShapeDtypeStruct((bsz, length, dil * D_WIDTH), jnp.bfloat16),
            jax.ShapeDtypeStruct((bsz, length, dil * LANES), jnp.float32),
        ],
        scratch_shapes=[pltpu.VMEM((tl + D_SPAN, D_WIDTH), jnp.bfloat16)] * 2,
        compiler_params=pltpu.CompilerParams(
            dimension_semantics=("arbitrary",) * 3, vmem_limit_bytes=VMEM_LIMIT),
        name=f"dilated_attention_d{dil}",
    )(view(q), view(k), view(k), view(v), view(v), _dilated_bias_table(dil))
    return o.reshape(bsz * seq, D_WIDTH), lse.reshape(bsz * seq, LANES)


def _branch_combine_kernel(o1_ref, o2_ref, o3_ref, l1_ref, l2_ref, l3_ref, y_ref):
    lses = [l1_ref[...], l2_ref[...], l3_ref[...]]
    mx = jnp.maximum(jnp.maximum(lses[0], lses[1]), lses[2])
    es = [jnp.exp2(l - mx) for l in lses]
    inv = 1.0 / (es[0] + es[1] + es[2])
    ws = [e * inv for e in es]
    lane = lax.broadcasted_iota(jnp.int32, (1, LANES), 1)
    for p in range(D_HEADS // 2):
        cols = slice(p * LANES, (p + 1) * LANES)
        yd = None
        for w, o in zip(ws, (o1_ref, o2_ref, o3_ref)):
            wf = jnp.where(lane < HEAD_DIM, w[:, 2 * p:2 * p + 1], w[:, 2 * p + 1:2 * p + 2])
            term = wf * o[:, cols].astype(jnp.float32)
            yd = term if yd is None else yd + term
        y_ref[:, cols] = yd.astype(jnp.bfloat16)


def _dilated_by_branch(q, k, v, *, bsz, seq):
    outs, lses = zip(*[_dilated_branch(q, k, v, bsz=bsz, seq=seq, dil=d) for d in D_DILATIONS])
    tm = TM_PROJ
    row = lambda i: (i, 0)
    return pl.pallas_call(
        _branch_combine_kernel,
        grid=(bsz * seq // tm,),
        in_specs=[pl.BlockSpec((tm, D_WIDTH), row)] * 3 + [pl.BlockSpec((tm, LANES), row)] * 3,
        out_specs=pl.BlockSpec((tm, D_WIDTH), row),
        out_shape=jax.ShapeDtypeStruct((bsz * seq, D_WIDTH), jnp.bfloat16),
        compiler_params=pltpu.CompilerParams(
            dimension_semantics=("arbitrary",), vmem_limit_bytes=VMEM_LIMIT),
        name="dilated_branch_combine",
    )(*outs, *lses)


def kernel(x, norm_mix, norm_ffn, ev_w_in, ev_w_out, ev_q_gain, ev_k_gain, ev_lambda_q1, ev_lambda_k1, ev_lambda_q2, ev_lambda_k2, ev_subln_gain, ev_pool_w, ev_pool_scale, od_w_in, od_w_out, od_conv_w, od_q_gain, od_k_gain, ffn_w_gate, ffn_w_up, ffn_w_down):
    bsz, seq, _ = x.shape
    bf = jnp.bfloat16
    f32 = jnp.float32
    q_scale = HEAD_DIM ** -0.5 * LOG2E
    x2d = x.reshape(bsz * seq, D_MODEL)
    grp = lax.broadcasted_iota(jnp.int32, (MXU_DEPTH, MXU_DEPTH), 0) // HEAD_DIM
    gsum = jnp.where(grp == grp.T, 1.0 / HEAD_DIM, 0.0).astype(bf)
    pw = ev_pool_w[0].astype(bf)
    zpw = jnp.zeros_like(pw[0])
    pool_w = jnp.stack([jnp.block([[pw[2 * i], zpw], [zpw, pw[2 * i + 1]]]) for i in range(2)])
    tile8 = lambda gain, s: (jnp.tile(gain.astype(f32), 8) * s).reshape(1, 512)

    w = ev_w_in[0]
    regroup = lambda c: c.reshape(D_MODEL, 2, A_HEADS, HEAD_DIM).transpose(0, 2, 1, 3).reshape(D_MODEL, 512)
    w_even = jnp.concatenate([regroup(w[:, 0:512]), regroup(w[:, 512:1024]), w[:, 1024:]], axis=1).astype(bf)
    q1, q2, k1, k2, v, yb = _even_in(
        x2d, norm_mix[0].reshape(1, D_MODEL).astype(f32), w_even, gsum,
        tile8(ev_q_gain[0], q_scale), tile8(ev_k_gain[0], 1.0),
        pool_w, ev_pool_scale[0].reshape(1, 512).astype(f32), seq=seq)
    lam_vecs = jnp.stack([ev_lambda_q1[0], ev_lambda_k1[0], ev_lambda_q2[0], ev_lambda_k2[0]]).astype(f32)
    later = [ffn_w_gate.reshape(-1, D_FF), ffn_w_up.reshape(-1, D_FF), ffn_w_down.reshape(-1, D_MODEL),
             ev_w_out[0], od_w_out[0], od_w_in[0]]
    attn = functools.partial(_diff_attn, q1, q2, k1, k2, v, lam_vecs, later, bsz=bsz, seq=seq)
    ya, (w_gate, w_up, w_down, ev_wo, od_wo, od_wi) = lax.cond(
        _score_bound(ev_q_gain[0], ev_k_gain[0]) <= MAX_UNSHIFTED_SCORE,
        lambda: attn(online=False), lambda: attn(online=True))
    ffn_w = (w_gate.reshape(ffn_w_gate.shape), w_up.reshape(ffn_w_up.shape),
             w_down.reshape(ffn_w_down.shape))
    x2d = _post(x2d, ya, yb, ev_wo,
                norm_ffn[0].reshape(1, D_MODEL).astype(f32), *ffn_w, 0, "even_post",
                subnorm_gain=(ev_subln_gain[0].astype(f32) * (1.0 - LAMBDA_INIT_L0)).reshape(1, A_V))

    yc, q, k, v = _odd_in(
        x2d, norm_mix[1].reshape(1, D_MODEL).astype(f32), od_wi, gsum,
        tile8(od_q_gain[0], q_scale), tile8(od_k_gain[0], 1.0), od_conv_w[0].astype(f32), seq=seq)
    yd = lax.cond(_score_bound(od_q_gain[0], od_k_gain[0]) <= MAX_UNSHIFTED_SCORE,
                  lambda: _dilated_fused(q, k, v, bsz=bsz, seq=seq),
                  lambda: _dilated_by_branch(q, k, v, bsz=bsz, seq=seq))
    x2d = _post(x2d, yc, yd, od_wo,
                norm_ffn[1].reshape(1, D_MODEL).astype(f32), *ffn_w, 1, "odd_post")
    return x2d.reshape(bsz, seq, D_MODEL)
```

```python
import functools
import math

import numpy as np
import jax
import jax.numpy as jnp
from jax import lax
from jax.experimental import pallas as pl
from jax.experimental.pallas import tpu as pltpu

D_MODEL = 1024
HEAD_DIM = 64
EPS = 1e-6
A_HEADS = 4
A_V = 128
B_WINDOWS = (2, 4, 8, 16)
POOL_HALO = 16
CONV_HALO = 8
D_HEADS = 8
D_WIDTH = D_HEADS * HEAD_DIM
D_DILATIONS = (1, 4, 16)
D_SPAN = 128
D_FF = 2816
LAMBDA_INIT_L0 = 0.8 - 0.6 * math.exp(-0.3 * 0)
NEG = -1e30
LOG2E = math.log2(math.e)
LOG2E_HI = float(np.float32(LOG2E).astype(jnp.bfloat16))
LOG2E_LO = float(np.float32(LOG2E - LOG2E_HI).astype(jnp.bfloat16))
MAX_UNSHIFTED_SCORE = 60.0

LANES = 128
MXU_DEPTH = 256
VMEM_LIMIT = 56 * 1024 * 1024

TM_PROJ = 512
TM_IN = 1024
SM_IN = 512
TQ = 2048
TK = 512
TK_DIAG = 256
TL = 512
SUB = 128
DBLK = 2048
GRP = 256
NG = DBLK // GRP

_NT = (((1,), (1,)), ((), ()))


def _dot(a, b):
    return jnp.dot(a, b, preferred_element_type=jnp.float32)


def _dot_nt(a, b):
    return lax.dot_general(a, b, _NT, preferred_element_type=jnp.float32)


def _rms_scale(x):
    return lax.rsqrt(jnp.mean(x * x, axis=-1, keepdims=True) + EPS)


def _group_rmsnorm(t, gsum_ref, gain):
    sq = (t * t).astype(jnp.bfloat16)
    depth = gsum_ref.shape[0]
    msq = jnp.concatenate([_dot(sq[:, c:c + depth], gsum_ref[...])
                           for c in range(0, t.shape[1], depth)], axis=1)
    return t * lax.rsqrt(msq + EPS) * gain


def _alibi_slopes(n):
    return [2.0 ** (-8.0 * (i + 1) / n) for i in range(n)]


def _alibi_lanes(pos, lane, first):
    rel = lane - first
    hi = ((pos >> 8) << 8).astype(jnp.float32)
    lo = (pos & 255).astype(jnp.float32)
    pos_sel = jnp.where((rel & 1) == 0, hi, lo)
    l_sel = jnp.where((rel & 3) < 2, LOG2E_HI, LOG2E_LO)
    first4 = (rel >= 0) & (rel < 4)
    last4 = (rel >= 4) & (rel < 8)
    return (jnp.where(first4, l_sel, jnp.where(last4, -pos_sel, 0.0)),
            jnp.where(first4, pos_sel, jnp.where(last4, l_sel, 0.0)))


def _shift_rows(ext, d):
    return pltpu.roll(ext, d, axis=0)


def _with_ones(v):
    return jnp.concatenate([v, jnp.ones_like(v)], axis=1)


def _score_bound(q_gain, k_gain):
    return (HEAD_DIM ** 0.5) * jnp.max(jnp.abs(q_gain)) * jnp.max(jnp.abs(k_gain))


def _even_in_kernel(x_ref, g_ref, w_ref, gsum_ref, qg_ref, kg_ref, pw_ref, ps_ref,
                    q1_ref, q2_ref, k1_ref, k2_ref, v_ref, yb_ref, carry_ref,
                    *, tm, sm, blocks_per_seq):
    blk = pl.program_id(0) % blocks_per_seq

    @pl.when(blk == 0)
    def _():
        carry_ref[...] = jnp.zeros_like(carry_ref)

    for sb in range(tm // sm):
        rows = slice(sb * sm, (sb + 1) * sm)
        x = x_ref[rows, :]
        h = (x * _rms_scale(x) * g_ref[...]).astype(jnp.bfloat16)
        proj = lambda part: _dot(h, w_ref[:, part * 512:(part + 1) * 512])
        qn = _group_rmsnorm(proj(0), gsum_ref, qg_ref[...])
        kn = _group_rmsnorm(proj(1), gsum_ref, kg_ref[...])
        v_ref[rows, :] = proj(2).astype(jnp.bfloat16)

        pos = lax.broadcasted_iota(jnp.int32, (sm, 1), 0) + (blk * tm + sb * sm)
        lane = lax.broadcasted_iota(jnp.int32, (1, LANES), 1)
        low = lane < HEAD_DIM
        alibi = [_alibi_lanes(pos, lane, HEAD_DIM), _alibi_lanes(pos, lane, 0)]
        for mp, (q_out, k_out) in enumerate(((q1_ref, k1_ref), (q2_ref, k2_ref))):
            for hd, slope in enumerate(_alibi_slopes(A_HEADS)):
                src = slice((mp * A_HEADS + hd) // 2 * LANES, ((mp * A_HEADS + hd) // 2 + 1) * LANES)
                dst = slice(hd * LANES, (hd + 1) * LANES)
                data = low if hd % 2 == 0 else jnp.logical_not(low)
                q_lanes, k_lanes = alibi[hd % 2]
                q_out[rows, dst] = jnp.where(data, qn[:, src], slope * q_lanes).astype(jnp.bfloat16)
                k_out[rows, dst] = jnp.where(data, kn[:, src], k_lanes).astype(jnp.bfloat16)

        u = proj(3)
        ext = jnp.concatenate([carry_ref[...], u], axis=0)
        pos1 = (pos + 1).astype(jnp.float32)
        pooled = []
        for g, w in enumerate(B_WINDOWS):
            cols = slice(g * LANES, (g + 1) * LANES)
            s = ext[:, cols]
            d = 1
            while d < w:
                s = s + _shift_rows(s, d)
                d *= 2
            pooled.append((s[POOL_HALO:, :] / jnp.minimum(pos1, float(w)) - u[:, cols]).astype(jnp.bfloat16))
        for pg in range(len(B_WINDOWS) // 2):
            cols = slice(2 * pg * LANES, (2 * pg + 2) * LANES)
            yb = _dot(jnp.concatenate(pooled[2 * pg:2 * pg + 2], axis=1), pw_ref[pg]) * ps_ref[:, cols]
            yb_ref[rows, cols] = yb.astype(jnp.bfloat16)
        carry_ref[...] = u[sm - POOL_HALO:, :]


def _even_in(x2d, g, w, gsum, qg, kg, pw, ps, *, seq):
    t = x2d.shape[0]
    tm = TM_IN
    const = lambda i: (0, 0)
    row = lambda i: (i, 0)
    out = jax.ShapeDtypeStruct((t, 512), jnp.bfloat16)
    return pl.pallas_call(
        functools.partial(_even_in_kernel, tm=tm, sm=SM_IN, blocks_per_seq=seq // tm),
        grid=(t // tm,),
        in_specs=[
            pl.BlockSpec((tm, D_MODEL), row),
            pl.BlockSpec((1, D_MODEL), const),
            pl.BlockSpec((D_MODEL, 2048), const),
            pl.BlockSpec((MXU_DEPTH, MXU_DEPTH), const),
            pl.BlockSpec((1, 512), const),
            pl.BlockSpec((1, 512), const),
            pl.BlockSpec((2, MXU_DEPTH, MXU_DEPTH), lambda i: (0, 0, 0)),
            pl.BlockSpec((1, 512), const),
        ],
        out_specs=[pl.BlockSpec((tm, 512), row)] * 6,
        out_shape=[out] * 6,
        scratch_shapes=[pltpu.VMEM((POOL_HALO, 512), jnp.float32)],
        compiler_params=pltpu.CompilerParams(
            dimension_semantics=("arbitrary",), vmem_limit_bytes=VMEM_LIMIT),
        name="even_in_proj",
    )(x2d, g, w, gsum, qg, kg, pw, ps)


def _diff_attn_kernel(q1_ref, q2_ref, k1_ref, k2_ref, v_ref, lam_ref, *refs,
                      tq, tk, td, n_cast, online):
    o_ref, acc_ref = refs[n_cast], refs[-1]
    for src, dst in zip(refs[:n_cast], refs[n_cast + 1:-1]):
        dst[...] = src[...].astype(jnp.bfloat16)

    n = tq // tk
    qi = pl.program_id(2)
    qs = (q1_ref[...], q2_ref[...])
    k_refs = (k1_ref, k2_ref)

    def key_rows(j, size):
        return pl.ds(pl.multiple_of(j * size, size), size)

    def scores(j, q_list, first_key=None, size=tk):
        out = []
        for q, k_ref in zip(q_list, k_refs):
            s = _dot_nt(q, k_ref[key_rows(j, size), :])
            if first_key is not None:
                ii = lax.broadcasted_iota(jnp.int32, s.shape, 0)
                jj = lax.broadcasted_iota(jnp.int32, s.shape, 1)
                s = jnp.where(jj + first_key > ii, NEG, s)
            out.append(s)
        return out

    def pv(j, ps, size=tk):
        v1 = _with_ones(v_ref[key_rows(j, size), :])
        return [_dot(p, v1) for p in ps]

    if online:
        def tile(j, state, first_key=None):
            out = []
            for mp, s in enumerate(scores(j, qs, first_key)):
                m_new = jnp.max(s, axis=-1, keepdims=True)
                if state is not None:
                    m_new = jnp.maximum(state[2 * mp], m_new)
                out += [m_new, jnp.exp2(s - m_new).astype(jnp.bfloat16)]
            accs = pv(j, out[1::2])
            if state is not None:
                accs = [jnp.exp2(state[2 * mp] - out[2 * mp]) * state[2 * mp + 1] + accs[mp]
                        for mp in range(2)]
            return (out[0], accs[0], out[2], accs[1])

        state = None
        for t in range(n):
            state = tile(n * qi + t, state, t * tk)
        _, acc1, _, acc2 = lax.fori_loop(0, n * qi, tile, state)
    else:
        def tile(j, q_list, first_key=None, size=tk):
            ps = [jnp.exp2(s).astype(jnp.bfloat16) for s in scores(j, q_list, first_key, size)]
            return pv(j, ps, size)

        for t in range(tq // td):
            for mp, part in enumerate(tile((tq // td) * qi + t, [q[t * td:] for q in qs], 0, td)):
                if t == 0:
                    acc_ref[mp] = part
                else:
                    acc_ref[mp, t * td:, :] += part

        def n_tiles(i, carry):
            total = tile(n * i, qs)
            for t in range(1, n):
                total = [a + b for a, b in zip(total, tile(n * i + t, qs))]
            for mp in range(2):
                acc_ref[mp] += total[mp]
            return carry

        lax.fori_loop(0, qi, n_tiles, 0)
        acc1, acc2 = acc_ref[0], acc_ref[1]

    lv = lam_ref[...]
    lam = (jnp.exp(jnp.sum(lv[0:1] * lv[1:2], axis=-1, keepdims=True))
           - jnp.exp(jnp.sum(lv[2:3] * lv[3:4], axis=-1, keepdims=True)) + LAMBDA_INIT_L0)
    o = acc1[:, :A_V] / acc1[:, A_V:] - lam * (acc2[:, :A_V] / acc2[:, A_V:])
    o_ref[...] = o.astype(jnp.bfloat16)


def _diff_attn(q1, q2, k1, k2, v, lam_vecs, weights, *, bsz, seq, online):
    tq = TQ
    nq = seq // tq
    steps = bsz * A_HEADS * nq
    q_spec = pl.BlockSpec((tq, LANES), lambda b, h, i: (b * nq + i, h))
    kv_spec = pl.BlockSpec((seq, LANES), lambda b, h, i: (b, h))

    def cast_spec(w):
        nblk = max(n for n in range(1, steps + 1)
                   if steps % n == 0 and w.shape[0] % (16 * n) == 0)
        return pl.BlockSpec((w.shape[0] // nblk, w.shape[1]),
                            lambda b, h, i: (((b * A_HEADS + h) * nq + i) // (steps // nblk), 0))

    cast_specs = [cast_spec(w) for w in weights]
    out = pl.pallas_call(
        functools.partial(_diff_attn_kernel, tq=tq, tk=TK, td=TK_DIAG, n_cast=len(weights),
                          online=online),
        grid=(bsz, A_HEADS, nq),
        in_specs=[
            q_spec, q_spec, kv_spec, kv_spec, kv_spec,
            pl.BlockSpec((4, HEAD_DIM), lambda b, h, i: (0, 0)),
        ] + cast_specs,
        out_specs=[q_spec] + cast_specs,
        out_shape=[jax.ShapeDtypeStruct((bsz * seq, A_HEADS * A_V), jnp.bfloat16)]
        + [jax.ShapeDtypeStruct(w.shape, jnp.bfloat16) for w in weights],
        scratch_shapes=[pltpu.VMEM((2, tq, 2 * A_V), jnp.float32)],
        compiler_params=pltpu.CompilerParams(
            dimension_semantics=("arbitrary",) * 3, vmem_limit_bytes=VMEM_LIMIT),
        name="diff_attention_online" if online else "diff_attention",
    )(q1, q2, k1, k2, v, lam_vecs, *weights)
    return out[0], out[1:]


def _post_kernel(x_ref, ya_ref, yb_ref, wo_ref, g_ref, wg_ref, wu_ref, wd_ref, *rest):
    o_ref = rest[-1]
    half = ya_ref.shape[1]
    ya = ya_ref[...]
    if len(rest) == 2:
        heads = []
        for hd in range(half // A_V):
            o = ya[:, hd * A_V:(hd + 1) * A_V].astype(jnp.float32)
            heads.append((o * _rms_scale(o) * rest[0][...]).astype(jnp.bfloat16))
        ya = jnp.concatenate(heads, axis=1)
    x1 = x_ref[...] + _dot(ya, wo_ref[0:half, :]) + _dot(yb_ref[...], wo_ref[half:, :])
    h = (x1 * _rms_scale(x1) * g_ref[...]).astype(jnp.bfloat16)
    gate = _dot(h, wg_ref[...])
    up = _dot(h, wu_ref[...])
    act = (gate * jax.nn.sigmoid(gate) * up).astype(jnp.bfloat16)
    o_ref[...] = x1 + _dot(act, wd_ref[...])


def _post(x2d, ya, yb, wo, g, wg, wu, wd, layer, name, subnorm_gain=None):
    t = x2d.shape[0]
    tm = TM_PROJ
    row = lambda i: (i, 0)
    const = lambda i: (0, 0)
    resident = functools.partial(pl.BlockSpec, index_map=const, pipeline_mode=pl.Buffered(1))
    ffn_w = lambda rows, cols: pl.BlockSpec((None, rows, cols), lambda i: (layer, 0, 0),
                                            pipeline_mode=pl.Buffered(1))
    extra = [] if subnorm_gain is None else [subnorm_gain]
    return pl.pallas_call(
        _post_kernel,
        grid=(t // tm,),
        in_specs=[
            pl.BlockSpec((tm, D_MODEL), row),
            pl.BlockSpec((tm, ya.shape[1]), row),
            pl.BlockSpec((tm, yb.shape[1]), row),
            resident((D_MODEL, D_MODEL)),
            pl.BlockSpec((1, D_MODEL), const),
            ffn_w(D_MODEL, D_FF),
            ffn_w(D_MODEL, D_FF),
            ffn_w(D_FF, D_MODEL),
        ] + [pl.BlockSpec((1, A_V), const)] * len(extra),
        out_specs=pl.BlockSpec((tm, D_MODEL), row),
        out_shape=jax.ShapeDtypeStruct((t, D_MODEL), jnp.float32),
        compiler_params=pltpu.CompilerParams(
            dimension_semantics=("arbitrary",), vmem_limit_bytes=VMEM_LIMIT),
        name=name,
    )(x2d, ya, yb, wo, g, wg, wu, wd, *extra)


def _odd_in_kernel(x_ref, g_ref, w_ref, gsum_ref, qg_ref, kg_ref, cw_ref,
                   yc_ref, q_ref, k_ref, v_ref, carry_ref, *, tm, sm, blocks_per_seq):
    blk = pl.program_id(0) % blocks_per_seq

    @pl.when(blk == 0)
    def _():
        carry_ref[...] = jnp.zeros_like(carry_ref)

    for sb in range(tm // sm):
        rows = slice(sb * sm, (sb + 1) * sm)
        x = x_ref[rows, :]
        h = (x * _rms_scale(x) * g_ref[...]).astype(jnp.bfloat16)
        proj = lambda part: _dot(h, w_ref[:, part * 512:(part + 1) * 512])
        z = proj(1) * proj(2)
        bg = proj(0)
        ext = jnp.concatenate([carry_ref[...], z], axis=0)
        z1 = _shift_rows(ext, 1)[CONV_HALO:, :]
        z2 = _shift_rows(ext, 2)[CONV_HALO:, :]
        cw = cw_ref[...]
        yc = bg * (cw[0:1] * z2 + cw[1:2] * z1 + cw[2:3] * z)
        yc_ref[rows, :] = yc.astype(jnp.bfloat16)
        carry_ref[...] = z[sm - CONV_HALO:, :]

        q_ref[rows, :] = _group_rmsnorm(proj(3), gsum_ref, qg_ref[...]).astype(jnp.bfloat16)
        k_ref[rows, :] = _group_rmsnorm(proj(4), gsum_ref, kg_ref[...]).astype(jnp.bfloat16)
        v_ref[rows, :] = proj(5).astype(jnp.bfloat16)


def _odd_in(x2d, g, w, gsum, qg, kg, cw, *, seq):
    t = x2d.shape[0]
    tm = TM_IN
    const = lambda i: (0, 0)
    row = lambda i: (i, 0)
    out = jax.ShapeDtypeStruct((t, 512), jnp.bfloat16)
    return pl.pallas_call(
        functools.partial(_odd_in_kernel, tm=tm, sm=SM_IN, blocks_per_seq=seq // tm),
        grid=(t // tm,),
        in_specs=[
            pl.BlockSpec((tm, D_MODEL), row),
            pl.BlockSpec((1, D_MODEL), const),
            pl.BlockSpec((D_MODEL, 3072), const),
            pl.BlockSpec((MXU_DEPTH, MXU_DEPTH), const),
            pl.BlockSpec((1, 512), const),
            pl.BlockSpec((1, 512), const),
            pl.BlockSpec((3, 512), const),
        ],
        out_specs=[pl.BlockSpec((tm, 512), row)] * 4,
        out_shape=[out] * 4,
        scratch_shapes=[pltpu.VMEM((CONV_HALO, 512), jnp.float32)],
        compiler_params=pltpu.CompilerParams(
            dimension_semantics=("arbitrary",), vmem_limit_bytes=VMEM_LIMIT),
        name="odd_in_proj",
    )(x2d, g, w, gsum, qg, kg, cw)


def _pair_tile(q, kp, vp, bias):
    lane = lax.broadcasted_iota(jnp.int32, (1, LANES), 1)
    low = lane < HEAD_DIM
    zero = jnp.zeros_like(q)
    q2 = jnp.concatenate([jnp.where(low, q, zero), jnp.where(low, zero, q)], axis=0)
    p = jnp.exp2(_dot_nt(q2, kp) + bias).astype(jnp.bfloat16)
    r = _dot(p, _with_ones(vp))
    num = jnp.where(low, r[:SUB, :LANES], r[SUB:, :LANES])
    den = jnp.where(low, r[:SUB, LANES:], r[SUB:, LANES:])
    return num, den


def _dilated_fused_kernel(q_ref, k_ref, v_ref, perm_ref, tab_ref, o_ref,
                          kn, vn, k4, v4, k16, v16, qp, res4, res16):
    blk = pl.program_id(1)
    first = blk == 0
    cur = (blk & 1) * NG
    prev = NG - cur
    bf = jnp.bfloat16

    @pl.when(first)
    def _():
        kn[0:D_SPAN, :] = jnp.zeros((D_SPAN, D_WIDTH), bf)
        vn[0:D_SPAN, :] = jnp.zeros((D_SPAN, D_WIDTH), bf)
        k4[0:2] = jnp.zeros((2, GRP, D_WIDTH), bf)
        v4[0:2] = jnp.zeros((2, GRP, D_WIDTH), bf)
        k16[pl.ds(prev, NG)] = jnp.zeros((NG, GRP, D_WIDTH), bf)
        v16[pl.ds(prev, NG)] = jnp.zeros((NG, GRP, D_WIDTH), bf)

    @pl.when(jnp.logical_not(first))
    def _():
        kn[0:D_SPAN, :] = kn[DBLK:DBLK + D_SPAN, :]
        vn[0:D_SPAN, :] = vn[DBLK:DBLK + D_SPAN, :]
        k4[0:2] = k4[NG:NG + 2]
        v4[0:2] = v4[NG:NG + 2]

    kn[D_SPAN:, :] = k_ref[...]
    vn[D_SPAN:, :] = v_ref[...]
    for g in range(NG):
        rows = slice(g * GRP, (g + 1) * GRP)
        for src, d4, d16 in ((k_ref, k4, k16), (v_ref, v4, v16)):
            x = src[rows, :]
            d4[2 + g] = _dot(perm_ref[0], x).astype(bf)
            d16[cur + g] = _dot(perm_ref[2], x).astype(bf)

    kk = lax.broadcasted_iota(jnp.int32, (1, 2 * SUB), 1)
    no_halo = jnp.where(kk < D_SPAN, jnp.where(first, NEG, 0.0), 0.0)

    def pair_cols(p):
        return slice(p * LANES, (p + 1) * LANES)

    for g in range(NG):
        qp[g] = _dot(perm_ref[2], q_ref[g * GRP:(g + 1) * GRP, :]).astype(bf)

    def classes16(i, carry):
        for t in range(8):
            r = pl.ds(pl.multiple_of(i * 128 + t * 16, 16), 16)
            for p in range(D_HEADS // 2):
                cols = pair_cols(p)
                q = jnp.concatenate([qp[g, r, cols] for g in range(NG)], axis=0)
                kp = jnp.concatenate([k16[prev + g, r, cols] for g in range(NG)]
                                     + [k16[cur + g, r, cols] for g in range(NG)], axis=0)
                vp = jnp.concatenate([v16[prev + g, r, cols] for g in range(NG)]
                                     + [v16[cur + g, r, cols] for g in range(NG)], axis=0)
                num, den = _pair_tile(q, kp, vp, tab_ref[2, p] + no_halo)
                for g in range(NG):
                    res16[0, g, r, cols] = num[16 * g:16 * g + 16].astype(bf)
                    res16[1, g, r, cols] = den[16 * g:16 * g + 16].astype(bf)
        return carry

    lax.fori_loop(0, 2, classes16, 0)

    for g in range(NG):
        qp[g] = _dot(perm_ref[0], q_ref[g * GRP:(g + 1) * GRP, :]).astype(bf)

    def classes4(i, carry):
        for t in range(2):
            r = pl.ds(pl.multiple_of(i * 128 + t * 64, 64), 64)
            for qs in range(NG // 2):
                for p in range(D_HEADS // 2):
                    cols = pair_cols(p)
                    q = jnp.concatenate([qp[2 * qs + j, r, cols] for j in range(2)], axis=0)
                    kp = jnp.concatenate([k4[2 * qs + j, r, cols] for j in range(4)], axis=0)
                    vp = jnp.concatenate([v4[2 * qs + j, r, cols] for j in range(4)], axis=0)
                    bias = tab_ref[1, p] + no_halo if qs == 0 else tab_ref[1, p]
                    num, den = _pair_tile(q, kp, vp, bias)
                    for j in range(2):
                        res4[0, 2 * qs + j, r, cols] = num[64 * j:64 * j + 64].astype(bf)
                        res4[1, 2 * qs + j, r, cols] = den[64 * j:64 * j + 64].astype(bf)
        return carry

    lax.fori_loop(0, 2, classes4, 0)

    def groups(i, carry):
        for u in range(2):
            g = 2 * i + u
            num = _dot(perm_ref[1], res4[0, g]) + _dot(perm_ref[3], res16[0, g])
            den = _dot(perm_ref[1], res4[1, g]) + _dot(perm_ref[3], res16[1, g])
            for t in range(GRP // SUB):
                q_rows = pl.ds(pl.multiple_of(g * GRP + t * SUB, SUB), SUB)
                k_rows = pl.ds(pl.multiple_of(g * GRP + t * SUB, SUB), 2 * SUB)
                sub = slice(t * SUB, (t + 1) * SUB)
                for p in range(D_HEADS // 2):
                    cols = pair_cols(p)
                    bias = tab_ref[0, p]
                    if u == 0 and t == 0:
                        bias = bias + jnp.where(i == 0, no_halo, 0.0)
                    n1, d1 = _pair_tile(q_ref[q_rows, cols], kn[k_rows, cols], vn[k_rows, cols], bias)
                    o_ref[q_rows, cols] = ((n1 + num[sub, cols]) / (d1 + den[sub, cols])).astype(bf)
        return carry

    lax.fori_loop(0, NG // 2, groups, 0)


def _class_permutations():
    mats = []
    out = np.arange(GRP)
    for d in (4, 16):
        per = GRP // d
        p = np.zeros((GRP, GRP), np.float32)
        p[out, d * (out % per) + out // per] = 1.0
        mats += [p, p.T]
    return jnp.asarray(np.stack(mats), jnp.bfloat16)


def _dilated_bias_table(dil):
    ii, kk = np.indices((SUB, SUB + D_SPAN))
    delta = D_SPAN + ii - kk
    slopes = np.asarray(_alibi_slopes(D_HEADS), np.float32)
    bias = np.float32(-LOG2E) * slopes[:, None, None] * (delta * dil).astype(np.float32)[None]
    return np.where(((delta >= 0) & (delta <= D_SPAN))[None], bias, np.float32(NEG))


def _dilated_fused(q, k, v, *, bsz, seq):
    nblk = seq // DBLK
    tabs = jnp.asarray(np.stack([_dilated_bias_table(d).reshape(D_HEADS // 2, 2 * SUB, SUB + D_SPAN)
                                 for d in D_DILATIONS]))
    blk_spec = pl.BlockSpec((DBLK, D_WIDTH), lambda b, i: (b * nblk + i, 0))
    group_buf = lambda n: pltpu.VMEM((n, GRP, D_WIDTH), jnp.bfloat16)
    return pl.pallas_call(
        _dilated_fused_kernel,
        grid=(bsz, nblk),
        in_specs=[
            blk_spec, blk_spec, blk_spec,
            pl.BlockSpec((4, GRP, GRP), lambda b, i: (0, 0, 0), pipeline_mode=pl.Buffered(1)),
            pl.BlockSpec(tabs.shape, lambda b, i: (0, 0, 0, 0), pipeline_mode=pl.Buffered(1)),
        ],
        out_specs=blk_spec,
        out_shape=jax.ShapeDtypeStruct((bsz * seq, D_WIDTH), jnp.bfloat16),
        scratch_shapes=[
            pltpu.VMEM((D_SPAN + DBLK, D_WIDTH), jnp.bfloat16),
            pltpu.VMEM((D_SPAN + DBLK, D_WIDTH), jnp.bfloat16),
            group_buf(2 + NG), group_buf(2 + NG),
            group_buf(2 * NG), group_buf(2 * NG),
            group_buf(NG),
            pltpu.VMEM((2, NG, GRP, D_WIDTH), jnp.bfloat16),
            pltpu.VMEM((2, NG, GRP, D_WIDTH), jnp.bfloat16),
        ],
        compiler_params=pltpu.CompilerParams(
            dimension_semantics=("arbitrary",) * 2, vmem_limit_bytes=VMEM_LIMIT),
        name="dilated_attention_fused",
    )(q, k, v, _class_permutations(), tabs)


def _dilated_kernel(q_ref, kh_ref, kc_ref, vh_ref, vc_ref, bias_ref, o_ref, lse_ref,
                    kbuf, vbuf, *, tl):
    j = pl.program_id(2)
    kbuf[0:D_SPAN, :] = kh_ref[...]
    kbuf[D_SPAN:, :] = kc_ref[...]
    vbuf[0:D_SPAN, :] = vh_ref[...]
    vbuf[D_SPAN:, :] = vc_ref[...]
    lane = lax.broadcasted_iota(jnp.int32, (1, LANES), 1)
    kk = lax.broadcasted_iota(jnp.int32, (1, SUB + D_SPAN), 1)
    halo_pen = jnp.where(kk < D_SPAN, jnp.where(j == 0, NEG, 0.0), 0.0)
    for qs in range(tl // SUB):
        rows = slice(qs * SUB, (qs + 1) * SUB)
        krows = slice(qs * SUB, (qs + 1) * SUB + D_SPAN)
        lse_tile = jnp.zeros((SUB, LANES), jnp.float32)
        for p in range(D_HEADS // 2):
            cols = slice(p * LANES, (p + 1) * LANES)
            q = q_ref[rows, cols]
            kp = kbuf[krows, cols]
            vp = vbuf[krows, cols]
            zero = jnp.zeros_like(q)
            halves = []
            for e in range(2):
                hd = 2 * p + e
                qm = jnp.where((lane < HEAD_DIM) if e == 0 else (lane >= HEAD_DIM), q, zero)
                s = _dot_nt(qm, kp) + bias_ref[hd]
                if qs == 0:
                    s = s + halo_pen
                m = jnp.max(s, axis=-1, keepdims=True)
                pe = jnp.exp2(s - m)
                l = jnp.sum(pe, axis=-1, keepdims=True)
                halves.append(_dot(pe.astype(jnp.bfloat16), vp) / l)
                lse_tile = jnp.where(lane == hd, m + jnp.log2(l), lse_tile)
            o_ref[rows, cols] = jnp.where(lane < HEAD_DIM, halves[0], halves[1]).astype(jnp.bfloat16)
        lse_ref[rows, :] = lse_tile


def _dilated_branch(q, k, v, *, bsz, seq, dil):
    tl = TL
    length = seq // dil
    nb = length // tl
    halo_per_blk = tl // D_SPAN
    view = lambda a: a.reshape(bsz, length, dil * D_WIDTH)
    cur = lambda b, r, j: (b, j, r)
    halo = lambda b, r, j: (b, jnp.maximum(j * halo_per_blk - 1, 0), r)
    o, lse = pl.pallas_call(
        functools.partial(_dilated_kernel, tl=tl),
        grid=(bsz, dil, nb),
        in_specs=[
            pl.BlockSpec((None, tl, D_WIDTH), cur),
            pl.BlockSpec((None, D_SPAN, D_WIDTH), halo),
            pl.BlockSpec((None, tl, D_WIDTH), cur),
            pl.BlockSpec((None, D_SPAN, D_WIDTH), halo),
            pl.BlockSpec((None, tl, D_WIDTH), cur),
            pl.BlockSpec((D_HEADS, SUB, SUB + D_SPAN), lambda b, r, j: (0, 0, 0)),
        ],
        out_specs=[
            pl.BlockSpec((None, tl, D_WIDTH), cur),
            pl.BlockSpec((None, tl, LANES), cur),
        ],
        out_shape=[
            jax.ShapeDtypeStruct((bsz, length, dil * D_WIDTH), jnp.bfloat16),
            jax.ShapeDtypeStruct((bsz, length, dil * LANES), jnp.float32),
        ],
        scratch_shapes=[pltpu.VMEM((tl + D_SPAN, D_WIDTH), jnp.bfloat16)] * 2,
        compiler_params=pltpu.CompilerParams(
            dimension_semantics=("arbitrary",) * 3, vmem_limit_bytes=VMEM_LIMIT),
        name=f"dilated_attention_d{dil}",
    )(view(q), view(k), view(k), view(v), view(v), jnp.asarray(_dilated_bias_table(dil)))
    return o.reshape(bsz * seq, D_WIDTH), lse.reshape(bsz * seq, LANES)


def _branch_combine_kernel(o1_ref, o2_ref, o3_ref, l1_ref, l2_ref, l3_ref, y_ref):
    lses = [l1_ref[...], l2_ref[...], l3_ref[...]]
    mx = jnp.maximum(jnp.maximum(lses[0], lses[1]), lses[2])
    es = [jnp.exp2(l - mx) for l in lses]
    inv = 1.0 / (es[0] + es[1] + es[2])
    ws = [e * inv for e in es]
    lane = lax.broadcasted_iota(jnp.int32, (1, LANES), 1)
    for p in range(D_HEADS // 2):
        cols = slice(p * LANES, (p + 1) * LANES)
        yd = None
        for w, o in zip(ws, (o1_ref, o2_ref, o3_ref)):
            wf = jnp.where(lane < HEAD_DIM, w[:, 2 * p:2 * p + 1], w[:, 2 * p + 1:2 * p + 2])
            term = wf * o[:, cols].astype(jnp.float32)
            yd = term if yd is None else yd + term
        y_ref[:, cols] = yd.astype(jnp.bfloat16)


def _dilated_by_branch(q, k, v, *, bsz, seq):
    outs, lses = zip(*[_dilated_branch(q, k, v, bsz=bsz, seq=seq, dil=d) for d in D_DILATIONS])
    tm = TM_PROJ
    row = lambda i: (i, 0)
    return pl.pallas_call(
        _branch_combine_kernel,
        grid=(bsz * seq // tm,),
        in_specs=[pl.BlockSpec((tm, D_WIDTH), row)] * 3 + [pl.BlockSpec((tm, LANES), row)] * 3,
        out_specs=pl.BlockSpec((tm, D_WIDTH), row),
        out_shape=jax.ShapeDtypeStruct((bsz * seq, D_WIDTH), jnp.bfloat16),
        compiler_params=pltpu.CompilerParams(
            dimension_semantics=("arbitrary",), vmem_limit_bytes=VMEM_LIMIT),
        name="dilated_branch_combine",
    )(*outs, *lses)


def kernel(x, norm_mix, norm_ffn, ev_w_in, ev_w_out, ev_q_gain, ev_k_gain, ev_lambda_q1, ev_lambda_k1, ev_lambda_q2, ev_lambda_k2, ev_subln_gain, ev_pool_w, ev_pool_scale, od_w_in, od_w_out, od_conv_w, od_q_gain, od_k_gain, ffn_w_gate, ffn_w_up, ffn_w_down):
    bsz, seq, _ = x.shape
    bf = jnp.bfloat16
    f32 = jnp.float32
    q_scale = HEAD_DIM ** -0.5 * LOG2E
    x2d = x.reshape(bsz * seq, D_MODEL)
    grp = np.arange(MXU_DEPTH) // HEAD_DIM
    gsum = jnp.asarray(np.where(grp[:, None] == grp[None, :], 1.0 / HEAD_DIM, 0.0), bf)
    pw = ev_pool_w[0].astype(bf)
    zpw = jnp.zeros_like(pw[0])
    pool_w = jnp.stack([jnp.block([[pw[2 * i], zpw], [zpw, pw[2 * i + 1]]]) for i in range(2)])
    tile8 = lambda gain, s: (jnp.tile(gain.astype(f32), 8) * s).reshape(1, 512)

    q1, q2, k1, k2, v, yb = _even_in(
        x2d, norm_mix[0].reshape(1, D_MODEL).astype(f32), ev_w_in[0].astype(bf), gsum,
        tile8(ev_q_gain[0], q_scale), tile8(ev_k_gain[0], 1.0),
        pool_w, ev_pool_scale[0].reshape(1, 512).astype(f32), seq=seq)
    lam_vecs = jnp.stack([ev_lambda_q1[0], ev_lambda_k1[0], ev_lambda_q2[0], ev_lambda_k2[0]]).astype(f32)
    later = [ffn_w_gate.reshape(-1, D_FF), ffn_w_up.reshape(-1, D_FF), ffn_w_down.reshape(-1, D_MODEL),
             ev_w_out[0], od_w_out[0], od_w_in[0]]
    attn = functools.partial(_diff_attn, q1, q2, k1, k2, v, lam_vecs, later, bsz=bsz, seq=seq)
    ya, (w_gate, w_up, w_down, ev_wo, od_wo, od_wi) = lax.cond(
        _score_bound(ev_q_gain[0], ev_k_gain[0]) <= MAX_UNSHIFTED_SCORE,
        lambda: attn(online=False), lambda: attn(online=True))
    ffn_w = (w_gate.reshape(ffn_w_gate.shape), w_up.reshape(ffn_w_up.shape),
             w_down.reshape(ffn_w_down.shape))
    x2d = _post(x2d, ya, yb, ev_wo,
                norm_ffn[0].reshape(1, D_MODEL).astype(f32), *ffn_w, 0, "even_post",
                subnorm_gain=(ev_subln_gain[0].astype(f32) * (1.0 - LAMBDA_INIT_L0)).reshape(1, A_V))

    yc, q, k, v = _odd_in(
        x2d, norm_mix[1].reshape(1, D_MODEL).astype(f32), od_wi, gsum,
        tile8(od_q_gain[0], q_scale), tile8(od_k_gain[0], 1.0), od_conv_w[0].astype(f32), seq=seq)
    yd = lax.cond(_score_bound(od_q_gain[0], od_k_gain[0]) <= MAX_UNSHIFTED_SCORE,
                  lambda: _dilated_fused(q, k, v, bsz=bsz, seq=seq),
                  lambda: _dilated_by_branch(q, k, v, bsz=bsz, seq=seq))
    x2d = _post(x2d, yc, yd, od_wo,
                norm_ffn[1].reshape(1, D_MODEL).astype(f32), *ffn_w, 1, "odd_post")
    return x2d.reshape(bsz, seq, D_MODEL)
```

```python
import functools
import math

import numpy as np
import jax
import jax.numpy as jnp
from jax import lax
from jax.experimental import pallas as pl
from jax.experimental.pallas import tpu as pltpu

D_MODEL = 1024
HEAD_DIM = 64
EPS = 1e-6
A_HEADS = 4
A_V = 128
B_WINDOWS = (2, 4, 8, 16)
POOL_HALO = 16
CONV_HALO = 8
D_HEADS = 8
D_WIDTH = D_HEADS * HEAD_DIM
D_DILATIONS = (1, 4, 16)
D_SPAN = 128
D_FF = 2816
LAMBDA_INIT_L0 = 0.8 - 0.6 * math.exp(-0.3 * 0)
NEG = -1e30
LOG2E = math.log2(math.e)
LOG2E_HI = float(np.float32(LOG2E).astype(jnp.bfloat16))
LOG2E_LO = float(np.float32(LOG2E - LOG2E_HI).astype(jnp.bfloat16))
MAX_UNSHIFTED_SCORE = 60.0

LANES = 128
MXU_DEPTH = 256
VMEM_LIMIT = 56 * 1024 * 1024

TM_PROJ = 512
TM_IN = 1024
SM_IN = 512
TQ = 2048
TK = 512
TK_DIAG = 256
TL = 512
SUB = 128
DBLK = 2048
GRP = 256
NG = DBLK // GRP

_NT = (((1,), (1,)), ((), ()))


def _dot(a, b):
    return jnp.dot(a, b, preferred_element_type=jnp.float32)


def _dot_nt(a, b):
    return lax.dot_general(a, b, _NT, preferred_element_type=jnp.float32)


def _rms_scale(x):
    return lax.rsqrt(jnp.mean(x * x, axis=-1, keepdims=True) + EPS)


def _group_rmsnorm(t, gsum_ref, gain):
    sq = (t * t).astype(jnp.bfloat16)
    depth = gsum_ref.shape[0]
    msq = jnp.concatenate([_dot(sq[:, c:c + depth], gsum_ref[...])
                           for c in range(0, t.shape[1], depth)], axis=1)
    return t * lax.rsqrt(msq + EPS) * gain


def _alibi_slopes(n):
    return [2.0 ** (-8.0 * (i + 1) / n) for i in range(n)]


def _alibi_lanes(pos, lane, first):
    rel = lane - first
    hi = ((pos >> 8) << 8).astype(jnp.float32)
    lo = (pos & 255).astype(jnp.float32)
    pos_sel = jnp.where((rel & 1) == 0, hi, lo)
    l_sel = jnp.where((rel & 3) < 2, LOG2E_HI, LOG2E_LO)
    first4 = (rel >= 0) & (rel < 4)
    last4 = (rel >= 4) & (rel < 8)
    return (jnp.where(first4, l_sel, jnp.where(last4, -pos_sel, 0.0)),
            jnp.where(first4, pos_sel, jnp.where(last4, l_sel, 0.0)))


def _shift_rows(ext, d):
    return pltpu.roll(ext, d, axis=0)


def _with_ones(v):
    return jnp.concatenate([v, jnp.ones_like(v)], axis=1)


def _score_bound(q_gain, k_gain):
    return (HEAD_DIM ** 0.5) * jnp.max(jnp.abs(q_gain)) * jnp.max(jnp.abs(k_gain))


def _even_in_kernel(x_ref, g_ref, w_ref, gsum_ref, qg_ref, kg_ref, pw_ref, ps_ref,
                    q1_ref, q2_ref, k1_ref, k2_ref, v_ref, yb_ref, carry_ref,
                    *, tm, sm, blocks_per_seq):
    blk = pl.program_id(0) % blocks_per_seq

    @pl.when(blk == 0)
    def _():
        carry_ref[...] = jnp.zeros_like(carry_ref)

    for sb in range(tm // sm):
        rows = slice(sb * sm, (sb + 1) * sm)
        x = x_ref[rows, :]
        h = (x * _rms_scale(x) * g_ref[...]).astype(jnp.bfloat16)
        proj = lambda part: _dot(h, w_ref[:, part * 512:(part + 1) * 512])
        qn = _group_rmsnorm(proj(0), gsum_ref, qg_ref[...])
        kn = _group_rmsnorm(proj(1), gsum_ref, kg_ref[...])
        v_ref[rows, :] = proj(2).astype(jnp.bfloat16)

        pos = lax.broadcasted_iota(jnp.int32, (sm, 1), 0) + (blk * tm + sb * sm)
        lane = lax.broadcasted_iota(jnp.int32, (1, LANES), 1)
        low = lane < HEAD_DIM
        alibi = [_alibi_lanes(pos, lane, HEAD_DIM), _alibi_lanes(pos, lane, 0)]
        for mp, (q_out, k_out) in enumerate(((q1_ref, k1_ref), (q2_ref, k2_ref))):
            for hd, slope in enumerate(_alibi_slopes(A_HEADS)):
                src = slice((mp * A_HEADS + hd) // 2 * LANES, ((mp * A_HEADS + hd) // 2 + 1) * LANES)
                dst = slice(hd * LANES, (hd + 1) * LANES)
                data = low if hd % 2 == 0 else jnp.logical_not(low)
                q_lanes, k_lanes = alibi[hd % 2]
                q_out[rows, dst] = jnp.where(data, qn[:, src], slope * q_lanes).astype(jnp.bfloat16)
                k_out[rows, dst] = jnp.where(data, kn[:, src], k_lanes).astype(jnp.bfloat16)

        u = proj(3)
        ext = jnp.concatenate([carry_ref[...], u], axis=0)
        pos1 = (pos + 1).astype(jnp.float32)
        pooled = []
        for g, w in enumerate(B_WINDOWS):
            cols = slice(g * LANES, (g + 1) * LANES)
            s = ext[:, cols]
            d = 1
            while d < w:
                s = s + _shift_rows(s, d)
                d *= 2
            pooled.append((s[POOL_HALO:, :] / jnp.minimum(pos1, float(w)) - u[:, cols]).astype(jnp.bfloat16))
        for pg in range(len(B_WINDOWS) // 2):
            cols = slice(2 * pg * LANES, (2 * pg + 2) * LANES)
            yb = _dot(jnp.concatenate(pooled[2 * pg:2 * pg + 2], axis=1), pw_ref[pg]) * ps_ref[:, cols]
            yb_ref[rows, cols] = yb.astype(jnp.bfloat16)
        carry_ref[...] = u[sm - POOL_HALO:, :]


def _even_in(x2d, g, w, gsum, qg, kg, pw, ps, *, seq):
    t = x2d.shape[0]
    tm = TM_IN
    const = lambda i: (0, 0)
    row = lambda i: (i, 0)
    out = jax.ShapeDtypeStruct((t, 512), jnp.bfloat16)
    return pl.pallas_call(
        functools.partial(_even_in_kernel, tm=tm, sm=SM_IN, blocks_per_seq=seq // tm),
        grid=(t // tm,),
        in_specs=[
            pl.BlockSpec((tm, D_MODEL), row),
            pl.BlockSpec((1, D_MODEL), const),
            pl.BlockSpec((D_MODEL, 2048), const),
            pl.BlockSpec((MXU_DEPTH, MXU_DEPTH), const),
            pl.BlockSpec((1, 512), const),
            pl.BlockSpec((1, 512), const),
            pl.BlockSpec((2, MXU_DEPTH, MXU_DEPTH), lambda i: (0, 0, 0)),
            pl.BlockSpec((1, 512), const),
        ],
        out_specs=[pl.BlockSpec((tm, 512), row)] * 6,
        out_shape=[out] * 6,
        scratch_shapes=[pltpu.VMEM((POOL_HALO, 512), jnp.float32)],
        compiler_params=pltpu.CompilerParams(
            dimension_semantics=("arbitrary",), vmem_limit_bytes=VMEM_LIMIT),
        name="even_in_proj",
    )(x2d, g, w, gsum, qg, kg, pw, ps)


def _diff_attn_kernel(q1_ref, q2_ref, k1_ref, k2_ref, v_ref, lam_ref, *refs,
                      tq, tk, td, n_cast, online):
    o_ref, acc_ref = refs[n_cast], refs[-1]
    for src, dst in zip(refs[:n_cast], refs[n_cast + 1:-1]):
        dst[...] = src[...].astype(jnp.bfloat16)

    n = tq // tk
    qi = pl.program_id(2)
    qs = (q1_ref[...], q2_ref[...])
    k_refs = (k1_ref, k2_ref)

    def key_rows(j, size):
        return pl.ds(pl.multiple_of(j * size, size), size)

    def scores(j, q_list, first_key=None, size=tk):
        out = []
        for q, k_ref in zip(q_list, k_refs):
            s = _dot_nt(q, k_ref[key_rows(j, size), :])
            if first_key is not None:
                ii = lax.broadcasted_iota(jnp.int32, s.shape, 0)
                jj = lax.broadcasted_iota(jnp.int32, s.shape, 1)
                s = jnp.where(jj + first_key > ii, NEG, s)
            out.append(s)
        return out

    def pv(j, ps, size=tk):
        v1 = _with_ones(v_ref[key_rows(j, size), :])
        return [_dot(p, v1) for p in ps]

    if online:
        def tile(j, state, first_key=None):
            out = []
            for mp, s in enumerate(scores(j, qs, first_key)):
                m_new = jnp.max(s, axis=-1, keepdims=True)
                if state is not None:
                    m_new = jnp.maximum(state[2 * mp], m_new)
                out += [m_new, jnp.exp2(s - m_new).astype(jnp.bfloat16)]
            accs = pv(j, out[1::2])
            if state is not None:
                accs = [jnp.exp2(state[2 * mp] - out[2 * mp]) * state[2 * mp + 1] + accs[mp]
                        for mp in range(2)]
            return (out[0], accs[0], out[2], accs[1])

        state = None
        for t in range(n):
            state = tile(n * qi + t, state, t * tk)
        _, acc1, _, acc2 = lax.fori_loop(0, n * qi, tile, state)
    else:
        def tile(j, q_list, first_key=None, size=tk):
            ps = [jnp.exp2(s).astype(jnp.bfloat16) for s in scores(j, q_list, first_key, size)]
            return pv(j, ps, size)

        for t in range(tq // td):
            for mp, part in enumerate(tile((tq // td) * qi + t, [q[t * td:] for q in qs], 0, td)):
                if t == 0:
                    acc_ref[mp] = part
                else:
                    acc_ref[mp, t * td:, :] += part

        def n_tiles(i, carry):
            total = tile(n * i, qs)
            for t in range(1, n):
                total = [a + b for a, b in zip(total, tile(n * i + t, qs))]
            for mp in range(2):
                acc_ref[mp] += total[mp]
            return carry

        lax.fori_loop(0, qi, n_tiles, 0)
        acc1, acc2 = acc_ref[0], acc_ref[1]

    lv = lam_ref[...]
    lam = (jnp.exp(jnp.sum(lv[0:1] * lv[1:2], axis=-1, keepdims=True))
           - jnp.exp(jnp.sum(lv[2:3] * lv[3:4], axis=-1, keepdims=True)) + LAMBDA_INIT_L0)
    o = acc1[:, :A_V] / acc1[:, A_V:] - lam * (acc2[:, :A_V] / acc2[:, A_V:])
    o_ref[...] = o.astype(jnp.bfloat16)


def _diff_attn(q1, q2, k1, k2, v, lam_vecs, weights, *, bsz, seq, online):
    tq = TQ
    nq = seq // tq
    steps = bsz * A_HEADS * nq
    q_spec = pl.BlockSpec((tq, LANES), lambda b, h, i: (b * nq + i, h))
    kv_spec = pl.BlockSpec((seq, LANES), lambda b, h, i: (b, h))

    def cast_spec(w):
        nblk = max(n for n in range(1, steps + 1)
                   if steps % n == 0 and w.shape[0] % (16 * n) == 0)
        return pl.BlockSpec((w.shape[0] // nblk, w.shape[1]),
                            lambda b, h, i: (((b * A_HEADS + h) * nq + i) // (steps // nblk), 0))

    cast_specs = [cast_spec(w) for w in weights]
    out = pl.pallas_call(
        functools.partial(_diff_attn_kernel, tq=tq, tk=TK, td=TK_DIAG, n_cast=len(weights),
                          online=online),
        grid=(bsz, A_HEADS, nq),
        in_specs=[
            q_spec, q_spec, kv_spec, kv_spec, kv_spec,
            pl.BlockSpec((4, HEAD_DIM), lambda b, h, i: (0, 0)),
        ] + cast_specs,
        out_specs=[q_spec] + cast_specs,
        out_shape=[jax.ShapeDtypeStruct((bsz * seq, A_HEADS * A_V), jnp.bfloat16)]
        + [jax.ShapeDtypeStruct(w.shape, jnp.bfloat16) for w in weights],
        scratch_shapes=[pltpu.VMEM((2, tq, 2 * A_V), jnp.float32)],
        compiler_params=pltpu.CompilerParams(
            dimension_semantics=("arbitrary",) * 3, vmem_limit_bytes=VMEM_LIMIT),
        name="diff_attention_online" if online else "diff_attention",
    )(q1, q2, k1, k2, v, lam_vecs, *weights)
    return out[0], out[1:]


def _post_kernel(x_ref, ya_ref, yb_ref, wo_ref, g_ref, wg_ref, wu_ref, wd_ref, *rest):
    o_ref = rest[-1]
    half = ya_ref.shape[1]
    ya = ya_ref[...]
    if len(rest) == 2:
        heads = []
        for hd in range(half // A_V):
            o = ya[:, hd * A_V:(hd + 1) * A_V].astype(jnp.float32)
            heads.append((o * _rms_scale(o) * rest[0][...]).astype(jnp.bfloat16))
        ya = jnp.concatenate(heads, axis=1)
    x1 = x_ref[...] + _dot(ya, wo_ref[0:half, :]) + _dot(yb_ref[...], wo_ref[half:, :])
    h = (x1 * _rms_scale(x1) * g_ref[...]).astype(jnp.bfloat16)
    gate = _dot(h, wg_ref[...])
    up = _dot(h, wu_ref[...])
    act = (gate * jax.nn.sigmoid(gate) * up).astype(jnp.bfloat16)
    o_ref[...] = x1 + _dot(act, wd_ref[...])


def _post(x2d, ya, yb, wo, g, wg, wu, wd, layer, name, subnorm_gain=None):
    t = x2d.shape[0]
    tm = TM_PROJ
    row = lambda i: (i, 0)
    const = lambda i: (0, 0)
    resident = functools.partial(pl.BlockSpec, index_map=const, pipeline_mode=pl.Buffered(1))
    ffn_w = lambda rows, cols: pl.BlockSpec((None, rows, cols), lambda i: (layer, 0, 0),
                                            pipeline_mode=pl.Buffered(1))
    extra = [] if subnorm_gain is None else [subnorm_gain]
    return pl.pallas_call(
        _post_kernel,
        grid=(t // tm,),
        in_specs=[
            pl.BlockSpec((tm, D_MODEL), row),
            pl.BlockSpec((tm, ya.shape[1]), row),
            pl.BlockSpec((tm, yb.shape[1]), row),
            resident((D_MODEL, D_MODEL)),
            pl.BlockSpec((1, D_MODEL), const),
            ffn_w(D_MODEL, D_FF),
            ffn_w(D_MODEL, D_FF),
            ffn_w(D_FF, D_MODEL),
        ] + [pl.BlockSpec((1, A_V), const)] * len(extra),
        out_specs=pl.BlockSpec((tm, D_MODEL), row),
        out_shape=jax.ShapeDtypeStruct((t, D_MODEL), jnp.float32),
        compiler_params=pltpu.CompilerParams(
            dimension_semantics=("arbitrary",), vmem_limit_bytes=VMEM_LIMIT),
        name=name,
    )(x2d, ya, yb, wo, g, wg, wu, wd, *extra)


def _odd_in_kernel(x_ref, g_ref, w_ref, gsum_ref, qg_ref, kg_ref, cw_ref,
                   yc_ref, q_ref, k_ref, v_ref, carry_ref, *, tm, sm, blocks_per_seq):
    blk = pl.program_id(0) % blocks_per_seq

    @pl.when(blk == 0)
    def _():
        carry_ref[...] = jnp.zeros_like(carry_ref)

    for sb in range(tm // sm):
        rows = slice(sb * sm, (sb + 1) * sm)
        x = x_ref[rows, :]
        h = (x * _rms_scale(x) * g_ref[...]).astype(jnp.bfloat16)
        proj = lambda part: _dot(h, w_ref[:, part * 512:(part + 1) * 512])
        z = proj(1) * proj(2)
        bg = proj(0)
        ext = jnp.concatenate([carry_ref[...], z], axis=0)
        z1 = _shift_rows(ext, 1)[CONV_HALO:, :]
        z2 = _shift_rows(ext, 2)[CONV_HALO:, :]
        cw = cw_ref[...]
        yc = bg * (cw[0:1] * z2 + cw[1:2] * z1 + cw[2:3] * z)
        yc_ref[rows, :] = yc.astype(jnp.bfloat16)
        carry_ref[...] = z[sm - CONV_HALO:, :]

        q_ref[rows, :] = _group_rmsnorm(proj(3), gsum_ref, qg_ref[...]).astype(jnp.bfloat16)
        k_ref[rows, :] = _group_rmsnorm(proj(4), gsum_ref, kg_ref[...]).astype(jnp.bfloat16)
        v_ref[rows, :] = proj(5).astype(jnp.bfloat16)


def _odd_in(x2d, g, w, gsum, qg, kg, cw, *, seq):
    t = x2d.shape[0]
    tm = TM_IN
    const = lambda i: (0, 0)
    row = lambda i: (i, 0)
    out = jax.ShapeDtypeStruct((t, 512), jnp.bfloat16)
    return pl.pallas_call(
        functools.partial(_odd_in_kernel, tm=tm, sm=SM_IN, blocks_per_seq=seq // tm),
        grid=(t // tm,),
        in_specs=[
            pl.BlockSpec((tm, D_MODEL), row),
            pl.BlockSpec((1, D_MODEL), const),
            pl.BlockSpec((D_MODEL, 3072), const),
            pl.BlockSpec((MXU_DEPTH, MXU_DEPTH), const),
            pl.BlockSpec((1, 512), const),
            pl.BlockSpec((1, 512), const),
            pl.BlockSpec((3, 512), const),
        ],
        out_specs=[pl.BlockSpec((tm, 512), row)] * 4,
        out_shape=[out] * 4,
        scratch_shapes=[pltpu.VMEM((CONV_HALO, 512), jnp.float32)],
        compiler_params=pltpu.CompilerParams(
            dimension_semantics=("arbitrary",), vmem_limit_bytes=VMEM_LIMIT),
        name="odd_in_proj",
    )(x2d, g, w, gsum, qg, kg, cw)


def _pair_tile(q, kp, vp, bias):
    lane = lax.broadcasted_iota(jnp.int32, (1, LANES), 1)
    low = lane < HEAD_DIM
    zero = jnp.zeros_like(q)
    q2 = jnp.concatenate([jnp.where(low, q, zero), jnp.where(low, zero, q)], axis=0)
    p = jnp.exp2(_dot_nt(q2, kp) + bias).astype(jnp.bfloat16)
    r = _dot(p, _with_ones(vp))
    num = jnp.where(low, r[:SUB, :LANES], r[SUB:, :LANES])
    den = jnp.where(low, r[:SUB, LANES:], r[SUB:, LANES:])
    return num, den


def _dilated_fused_kernel(q_ref, k_ref, v_ref, perm_ref, tab_ref, o_ref,
                          kn, vn, k4, v4, k16, v16, qp, res4, res16):
    blk = pl.program_id(1)
    first = blk == 0
    cur = (blk & 1) * NG
    prev = NG - cur
    bf = jnp.bfloat16

    @pl.when(first)
    def _():
        kn[0:D_SPAN, :] = jnp.zeros((D_SPAN, D_WIDTH), bf)
        vn[0:D_SPAN, :] = jnp.zeros((D_SPAN, D_WIDTH), bf)
        k4[0:2] = jnp.zeros((2, GRP, D_WIDTH), bf)
        v4[0:2] = jnp.zeros((2, GRP, D_WIDTH), bf)
        k16[pl.ds(prev, NG)] = jnp.zeros((NG, GRP, D_WIDTH), bf)
        v16[pl.ds(prev, NG)] = jnp.zeros((NG, GRP, D_WIDTH), bf)

    @pl.when(jnp.logical_not(first))
    def _():
        kn[0:D_SPAN, :] = kn[DBLK:DBLK + D_SPAN, :]
        vn[0:D_SPAN, :] = vn[DBLK:DBLK + D_SPAN, :]
        k4[0:2] = k4[NG:NG + 2]
        v4[0:2] = v4[NG:NG + 2]

    kn[D_SPAN:, :] = k_ref[...]
    vn[D_SPAN:, :] = v_ref[...]
    for g in range(NG):
        rows = slice(g * GRP, (g + 1) * GRP)
        for src, d4, d16 in ((k_ref, k4, k16), (v_ref, v4, v16)):
            x = src[rows, :]
            d4[2 + g] = _dot(perm_ref[0], x).astype(bf)
            d16[cur + g] = _dot(perm_ref[2], x).astype(bf)

    kk = lax.broadcasted_iota(jnp.int32, (1, 2 * SUB), 1)
    no_halo = jnp.where(kk < D_SPAN, jnp.where(first, NEG, 0.0), 0.0)

    def pair_cols(p):
        return slice(p * LANES, (p + 1) * LANES)

    for g in range(NG):
        qp[g] = _dot(perm_ref[2], q_ref[g * GRP:(g + 1) * GRP, :]).astype(bf)

    def classes16(i, carry):
        for t in range(16):
            r = pl.ds(pl.multiple_of(i * 256 + t * 16, 16), 16)
            for p in range(D_HEADS // 2):
                cols = pair_cols(p)
                q = jnp.concatenate([qp[g, r, cols] for g in range(NG)], axis=0)
                kp = jnp.concatenate([k16[prev + g, r, cols] for g in range(NG)]
                                     + [k16[cur + g, r, cols] for g in range(NG)], axis=0)
                vp = jnp.concatenate([v16[prev + g, r, cols] for g in range(NG)]
                                     + [v16[cur + g, r, cols] for g in range(NG)], axis=0)
                num, den = _pair_tile(q, kp, vp, tab_ref[2, p] + no_halo)
                for g in range(NG):
                    res16[0, g, r, cols] = num[16 * g:16 * g + 16].astype(bf)
                    res16[1, g, r, cols] = den[16 * g:16 * g + 16].astype(bf)
        return carry

    lax.fori_loop(0, 1, classes16, 0)

    for g in range(NG):
        qp[g] = _dot(perm_ref[0], q_ref[g * GRP:(g + 1) * GRP, :]).astype(bf)

    def classes4(i, carry):
        for t in range(4):
            r = pl.ds(pl.multiple_of(i * 256 + t * 64, 64), 64)
            for qs in range(NG // 2):
                for p in range(D_HEADS // 2):
                    cols = pair_cols(p)
                    q = jnp.concatenate([qp[2 * qs + j, r, cols] for j in range(2)], axis=0)
                    kp = jnp.concatenate([k4[2 * qs + j, r, cols] for j in range(4)], axis=0)
                    vp = jnp.concatenate([v4[2 * qs + j, r, cols] for j in range(4)], axis=0)
                    bias = tab_ref[1, p] + no_halo if qs == 0 else tab_ref[1, p]
                    num, den = _pair_tile(q, kp, vp, bias)
                    for j in range(2):
                        res4[0, 2 * qs + j, r, cols] = num[64 * j:64 * j + 64].astype(bf)
                        res4[1, 2 * qs + j, r, cols] = den[64 * j:64 * j + 64].astype(bf)
        return carry

    lax.fori_loop(0, 1, classes4, 0)

    def groups(i, carry):
        for u in range(4):
            g = 4 * i + u
            num = _dot(perm_ref[1], res4[0, g]) + _dot(perm_ref[3], res16[0, g])
            den = _dot(perm_ref[1], res4[1, g]) + _dot(perm_ref[3], res16[1, g])
            for t in range(GRP // SUB):
                q_rows = pl.ds(pl.multiple_of(g * GRP + t * SUB, SUB), SUB)
                k_rows = pl.ds(pl.multiple_of(g * GRP + t * SUB, SUB), 2 * SUB)
                sub = slice(t * SUB, (t + 1) * SUB)
                for p in range(D_HEADS // 2):
                    cols = pair_cols(p)
                    bias = tab_ref[0, p]
                    if u == 0 and t == 0:
                        bias = bias + jnp.where(i == 0, no_halo, 0.0)
                    n1, d1 = _pair_tile(q_ref[q_rows, cols], kn[k_rows, cols], vn[k_rows, cols], bias)
                    o_ref[q_rows, cols] = ((n1 + num[sub, cols]) / (d1 + den[sub, cols])).astype(bf)
        return carry

    lax.fori_loop(0, NG // 4, groups, 0)


def _class_permutations():
    mats = []
    out = np.arange(GRP)
    for d in (4, 16):
        per = GRP // d
        p = np.zeros((GRP, GRP), np.float32)
        p[out, d * (out % per) + out // per] = 1.0
        mats += [p, p.T]
    return jnp.asarray(np.stack(mats), jnp.bfloat16)


def _dilated_bias_table(dil):
    ii, kk = np.indices((SUB, SUB + D_SPAN))
    delta = D_SPAN + ii - kk
    slopes = np.asarray(_alibi_slopes(D_HEADS), np.float32)
    bias = np.float32(-LOG2E) * slopes[:, None, None] * (delta * dil).astype(np.float32)[None]
    return np.where(((delta >= 0) & (delta <= D_SPAN))[None], bias, np.float32(NEG))


def _dilated_fused(q, k, v, *, bsz, seq):
    nblk = seq // DBLK
    tabs = jnp.asarray(np.stack([_dilated_bias_table(d).reshape(D_HEADS // 2, 2 * SUB, SUB + D_SPAN)
                                 for d in D_DILATIONS]))
    blk_spec = pl.BlockSpec((DBLK, D_WIDTH), lambda b, i: (b * nblk + i, 0))
    group_buf = lambda n: pltpu.VMEM((n, GRP, D_WIDTH), jnp.bfloat16)
    return pl.pallas_call(
        _dilated_fused_kernel,
        grid=(bsz, nblk),
        in_specs=[
            blk_spec, blk_spec, blk_spec,
            pl.BlockSpec((4, GRP, GRP), lambda b, i: (0, 0, 0), pipeline_mode=pl.Buffered(1)),
            pl.BlockSpec(tabs.shape, lambda b, i: (0, 0, 0, 0), pipeline_mode=pl.Buffered(1)),
        ],
        out_specs=blk_spec,
        out_shape=jax.ShapeDtypeStruct((bsz * seq, D_WIDTH), jnp.bfloat16),
        scratch_shapes=[
            pltpu.VMEM((D_SPAN + DBLK, D_WIDTH), jnp.bfloat16),
            pltpu.VMEM((D_SPAN + DBLK, D_WIDTH), jnp.bfloat16),
            group_buf(2 + NG), group_buf(2 + NG),
            group_buf(2 * NG), group_buf(2 * NG),
            group_buf(NG),
            pltpu.VMEM((2, NG, GRP, D_WIDTH), jnp.bfloat16),
            pltpu.VMEM((2, NG, GRP, D_WIDTH), jnp.bfloat16),
        ],
        compiler_params=pltpu.CompilerParams(
            dimension_semantics=("arbitrary",) * 2, vmem_limit_bytes=VMEM_LIMIT),
        name="dilated_attention_fused",
    )(q, k, v, _class_permutations(), tabs)


def _dilated_kernel(q_ref, kh_ref, kc_ref, vh_ref, vc_ref, bias_ref, o_ref, lse_ref,
                    kbuf, vbuf, *, tl):
    j = pl.program_id(2)
    kbuf[0:D_SPAN, :] = kh_ref[...]
    kbuf[D_SPAN:, :] = kc_ref[...]
    vbuf[0:D_SPAN, :] = vh_ref[...]
    vbuf[D_SPAN:, :] = vc_ref[...]
    lane = lax.broadcasted_iota(jnp.int32, (1, LANES), 1)
    kk = lax.broadcasted_iota(jnp.int32, (1, SUB + D_SPAN), 1)
    halo_pen = jnp.where(kk < D_SPAN, jnp.where(j == 0, NEG, 0.0), 0.0)
    for qs in range(tl // SUB):
        rows = slice(qs * SUB, (qs + 1) * SUB)
        krows = slice(qs * SUB, (qs + 1) * SUB + D_SPAN)
        lse_tile = jnp.zeros((SUB, LANES), jnp.float32)
        for p in range(D_HEADS // 2):
            cols = slice(p * LANES, (p + 1) * LANES)
            q = q_ref[rows, cols]
            kp = kbuf[krows, cols]
            vp = vbuf[krows, cols]
            zero = jnp.zeros_like(q)
            halves = []
            for e in range(2):
                hd = 2 * p + e
                qm = jnp.where((lane < HEAD_DIM) if e == 0 else (lane >= HEAD_DIM), q, zero)
                s = _dot_nt(qm, kp) + bias_ref[hd]
                if qs == 0:
                    s = s + halo_pen
                m = jnp.max(s, axis=-1, keepdims=True)
                pe = jnp.exp2(s - m)
                l = jnp.sum(pe, axis=-1, keepdims=True)
                halves.append(_dot(pe.astype(jnp.bfloat16), vp) / l)
                lse_tile = jnp.where(lane == hd, m + jnp.log2(l), lse_tile)
            o_ref[rows, cols] = jnp.where(lane < HEAD_DIM, halves[0], halves[1]).astype(jnp.bfloat16)
        lse_ref[rows, :] = lse_tile


def _dilated_branch(q, k, v, *, bsz, seq, dil):
    tl = TL
    length = seq // dil
    nb = length // tl
    halo_per_blk = tl // D_SPAN
    view = lambda a: a.reshape(bsz, length, dil * D_WIDTH)
    cur = lambda b, r, j: (b, j, r)
    halo = lambda b, r, j: (b, jnp.maximum(j * halo_per_blk - 1, 0), r)
    o, lse = pl.pallas_call(
        functools.partial(_dilated_kernel, tl=tl),
        grid=(bsz, dil, nb),
        in_specs=[
            pl.BlockSpec((None, tl, D_WIDTH), cur),
            pl.BlockSpec((None, D_SPAN, D_WIDTH), halo),
            pl.BlockSpec((None, tl, D_WIDTH), cur),
            pl.BlockSpec((None, D_SPAN, D_WIDTH), halo),
            pl.BlockSpec((None, tl, D_WIDTH), cur),
            pl.BlockSpec((D_HEADS, SUB, SUB + D_SPAN), lambda b, r, j: (0, 0, 0)),
        ],
        out_specs=[
            pl.BlockSpec((None, tl, D_WIDTH), cur),
            pl.BlockSpec((None, tl, LANES), cur),
        ],
        out_shape=[
            jax.ShapeDtypeStruct((bsz, length, dil * D_WIDTH), jnp.bfloat16),
            jax.ShapeDtypeStruct((bsz, length, dil * LANES), jnp.float32),
        ],
        scratch_shapes=[pltpu.VMEM((tl + D_SPAN, D_WIDTH), jnp.bfloat16)] * 2,
        compiler_params=pltpu.CompilerParams(
            dimension_semantics=("arbitrary",) * 3, vmem_limit_bytes=VMEM_LIMIT),
        name=f"dilated_attention_d{dil}",
    )(view(q), view(k), view(k), view(v), view(v), jnp.asarray(_dilated_bias_table(dil)))
    return o.reshape(bsz * seq, D_WIDTH), lse.reshape(bsz * seq, LANES)


def _branch_combine_kernel(o1_ref, o2_ref, o3_ref, l1_ref, l2_ref, l3_ref, y_ref):
    lses = [l1_ref[...], l2_ref[...], l3_ref[...]]
    mx = jnp.maximum(jnp.maximum(lses[0], lses[1]), lses[2])
    es = [jnp.exp2(l - mx) for l in lses]
    inv = 1.0 / (es[0] + es[1] + es[2])
    ws = [e * inv for e in es]
    lane = lax.broadcasted_iota(jnp.int32, (1, LANES), 1)
    for p in range(D_HEADS // 2):
        cols = slice(p * LANES, (p + 1) * LANES)
        yd = None
        for w, o in zip(ws, (o1_ref, o2_ref, o3_ref)):
            wf = jnp.where(lane < HEAD_DIM, w[:, 2 * p:2 * p + 1], w[:, 2 * p + 1:2 * p + 2])
            term = wf * o[:, cols].astype(jnp.float32)
            yd = term if yd is None else yd + term
        y_ref[:, cols] = yd.astype(jnp.bfloat16)


def _dilated_by_branch(q, k, v, *, bsz, seq):
    outs, lses = zip(*[_dilated_branch(q, k, v, bsz=bsz, seq=seq, dil=d) for d in D_DILATIONS])
    tm = TM_PROJ
    row = lambda i: (i, 0)
    return pl.pallas_call(
        _branch_combine_kernel,
        grid=(bsz * seq // tm,),
        in_specs=[pl.BlockSpec((tm, D_WIDTH), row)] * 3 + [pl.BlockSpec((tm, LANES), row)] * 3,
        out_specs=pl.BlockSpec((tm, D_WIDTH), row),
        out_shape=jax.ShapeDtypeStruct((bsz * seq, D_WIDTH), jnp.bfloat16),
        compiler_params=pltpu.CompilerParams(
            dimension_semantics=("arbitrary",), vmem_limit_bytes=VMEM_LIMIT),
        name="dilated_branch_combine",
    )(*outs, *lses)


def kernel(x, norm_mix, norm_ffn, ev_w_in, ev_w_out, ev_q_gain, ev_k_gain, ev_lambda_q1, ev_lambda_k1, ev_lambda_q2, ev_lambda_k2, ev_subln_gain, ev_pool_w, ev_pool_scale, od_w_in, od_w_out, od_conv_w, od_q_gain, od_k_gain, ffn_w_gate, ffn_w_up, ffn_w_down):
    bsz, seq, _ = x.shape
    bf = jnp.bfloat16
    f32 = jnp.float32
    q_scale = HEAD_DIM ** -0.5 * LOG2E
    x2d = x.reshape(bsz * seq, D_MODEL)
    grp = np.arange(MXU_DEPTH) // HEAD_DIM
    gsum = jnp.asarray(np.where(grp[:, None] == grp[None, :], 1.0 / HEAD_DIM, 0.0), bf)
    pw = ev_pool_w[0].astype(bf)
    zpw = jnp.zeros_like(pw[0])
    pool_w = jnp.stack([jnp.block([[pw[2 * i], zpw], [zpw, pw[2 * i + 1]]]) for i in range(2)])
    tile8 = lambda gain, s: (jnp.tile(gain.astype(f32), 8) * s).reshape(1, 512)

    q1, q2, k1, k2, v, yb = _even_in(
        x2d, norm_mix[0].reshape(1, D_MODEL).astype(f32), ev_w_in[0].astype(bf), gsum,
        tile8(ev_q_gain[0], q_scale), tile8(ev_k_gain[0], 1.0),
        pool_w, ev_pool_scale[0].reshape(1, 512).astype(f32), seq=seq)
    lam_vecs = jnp.stack([ev_lambda_q1[0], ev_lambda_k1[0], ev_lambda_q2[0], ev_lambda_k2[0]]).astype(f32)
    later = [ffn_w_gate.reshape(-1, D_FF), ffn_w_up.reshape(-1, D_FF), ffn_w_down.reshape(-1, D_MODEL),
             ev_w_out[0], od_w_out[0], od_w_in[0]]
    attn = functools.partial(_diff_attn, q1, q2, k1, k2, v, lam_vecs, later, bsz=bsz, seq=seq)
    ya, (w_gate, w_up, w_down, ev_wo, od_wo, od_wi) = lax.cond(
        _score_bound(ev_q_gain[0], ev_k_gain[0]) <= MAX_UNSHIFTED_SCORE,
        lambda: attn(online=False), lambda: attn(online=True))
    ffn_w = (w_gate.reshape(ffn_w_gate.shape), w_up.reshape(ffn_w_up.shape),
             w_down.reshape(ffn_w_down.shape))
    x2d = _post(x2d, ya, yb, ev_wo,
                norm_ffn[0].reshape(1, D_MODEL).astype(f32), *ffn_w, 0, "even_post",
                subnorm_gain=(ev_subln_gain[0].astype(f32) * (1.0 - LAMBDA_INIT_L0)).reshape(1, A_V))

    yc, q, k, v = _odd_in(
        x2d, norm_mix[1].reshape(1, D_MODEL).astype(f32), od_wi, gsum,
        tile8(od_q_gain[0], q_scale), tile8(od_k_gain[0], 1.0), od_conv_w[0].astype(f32), seq=seq)
    yd = lax.cond(_score_bound(od_q_gain[0], od_k_gain[0]) <= MAX_UNSHIFTED_SCORE,
                  lambda: _dilated_fused(q, k, v, bsz=bsz, seq=seq),
                  lambda: _dilated_by_branch(q, k, v, bsz=bsz, seq=seq))
    x2d = _post(x2d, yc, yd, od_wo,
                norm_ffn[1].reshape(1, D_MODEL).astype(f32), *ffn_w, 1, "odd_post")
    return x2d.reshape(bsz, seq, D_MODEL)
```

```python
import functools
import math

import numpy as np
import jax
import jax.numpy as jnp
from jax import lax
from jax.experimental import pallas as pl
from jax.experimental.pallas import tpu as pltpu

D_MODEL = 1024
HEAD_DIM = 64
EPS = 1e-6
A_HEADS = 4
A_V = 128
B_WINDOWS = (2, 4, 8, 16)
POOL_HALO = 16
CONV_HALO = 8
D_HEADS = 8
D_WIDTH = D_HEADS * HEAD_DIM
D_DILATIONS = (1, 4, 16)
D_SPAN = 128
D_FF = 2816
LAMBDA_INIT_L0 = 0.8 - 0.6 * math.exp(-0.3 * 0)
NEG = -1e30
LOG2E = math.log2(math.e)
LOG2E_HI = float(np.float32(LOG2E).astype(jnp.bfloat16))
LOG2E_LO = float(np.float32(LOG2E - LOG2E_HI).astype(jnp.bfloat16))
MAX_UNSHIFTED_SCORE = 60.0

LANES = 128
MXU_DEPTH = 256
VMEM_LIMIT = 56 * 1024 * 1024

TM_PROJ = 512
TM_IN = 1024
SM_IN = 512
TQ = 2048
TK = 512
TK_DIAG = 256
TL = 512
SUB = 128
DBLK = 2048
GRP = 256
NG = DBLK // GRP

_NT = (((1,), (1,)), ((), ()))


def _dot(a, b):
    return jnp.dot(a, b, preferred_element_type=jnp.float32)


def _dot_nt(a, b):
    return lax.dot_general(a, b, _NT, preferred_element_type=jnp.float32)


def _rms_scale(x):
    return lax.rsqrt(jnp.mean(x * x, axis=-1, keepdims=True) + EPS)


def _group_rmsnorm(t, gsum_ref, gain):
    sq = (t * t).astype(jnp.bfloat16)
    depth = gsum_ref.shape[0]
    msq = jnp.concatenate([_dot(sq[:, c:c + depth], gsum_ref[...])
                           for c in range(0, t.shape[1], depth)], axis=1)
    return t * lax.rsqrt(msq + EPS) * gain


def _alibi_slopes(n):
    return [2.0 ** (-8.0 * (i + 1) / n) for i in range(n)]


def _alibi_lanes(pos, lane, first):
    rel = lane - first
    hi = ((pos >> 8) << 8).astype(jnp.float32)
    lo = (pos & 255).astype(jnp.float32)
    pos_sel = jnp.where((rel & 1) == 0, hi, lo)
    l_sel = jnp.where((rel & 3) < 2, LOG2E_HI, LOG2E_LO)
    first4 = (rel >= 0) & (rel < 4)
    last4 = (rel >= 4) & (rel < 8)
    return (jnp.where(first4, l_sel, jnp.where(last4, -pos_sel, 0.0)),
            jnp.where(first4, pos_sel, jnp.where(last4, l_sel, 0.0)))


def _shift_rows(ext, d):
    return pltpu.roll(ext, d, axis=0)


def _with_ones(v):
    return jnp.concatenate([v, jnp.ones_like(v)], axis=1)


def _score_bound(q_gain, k_gain):
    return (HEAD_DIM ** 0.5) * jnp.max(jnp.abs(q_gain)) * jnp.max(jnp.abs(k_gain))


def _even_in_kernel(x_ref, g_ref, w_ref, gsum_ref, qg_ref, kg_ref, pw_ref, ps_ref,
                    q1_ref, q2_ref, k1_ref, k2_ref, v_ref, yb_ref, carry_ref,
                    *, tm, sm, blocks_per_seq):
    blk = pl.program_id(0) % blocks_per_seq

    @pl.when(blk == 0)
    def _():
        carry_ref[...] = jnp.zeros_like(carry_ref)

    for sb in range(tm // sm):
        rows = slice(sb * sm, (sb + 1) * sm)
        x = x_ref[rows, :]
        h = (x * _rms_scale(x) * g_ref[...]).astype(jnp.bfloat16)
        proj = lambda part: _dot(h, w_ref[:, part * 512:(part + 1) * 512])
        qn = _group_rmsnorm(proj(0), gsum_ref, qg_ref[...])
        kn = _group_rmsnorm(proj(1), gsum_ref, kg_ref[...])
        v_ref[rows, :] = proj(2).astype(jnp.bfloat16)

        pos = lax.broadcasted_iota(jnp.int32, (sm, 1), 0) + (blk * tm + sb * sm)
        lane = lax.broadcasted_iota(jnp.int32, (1, LANES), 1)
        low = lane < HEAD_DIM
        alibi = [_alibi_lanes(pos, lane, HEAD_DIM), _alibi_lanes(pos, lane, 0)]
        for mp, (q_out, k_out) in enumerate(((q1_ref, k1_ref), (q2_ref, k2_ref))):
            for hd, slope in enumerate(_alibi_slopes(A_HEADS)):
                src = slice((mp * A_HEADS + hd) // 2 * LANES, ((mp * A_HEADS + hd) // 2 + 1) * LANES)
                dst = slice(hd * LANES, (hd + 1) * LANES)
                data = low if hd % 2 == 0 else jnp.logical_not(low)
                q_lanes, k_lanes = alibi[hd % 2]
                q_out[rows, dst] = jnp.where(data, qn[:, src], slope * q_lanes).astype(jnp.bfloat16)
                k_out[rows, dst] = jnp.where(data, kn[:, src], k_lanes).astype(jnp.bfloat16)

        u = proj(3)
        ext = jnp.concatenate([carry_ref[...], u], axis=0)
        pos1 = (pos + 1).astype(jnp.float32)
        pooled = []
        for g, w in enumerate(B_WINDOWS):
            cols = slice(g * LANES, (g + 1) * LANES)
            s = ext[:, cols]
            d = 1
            while d < w:
                s = s + _shift_rows(s, d)
                d *= 2
            pooled.append((s[POOL_HALO:, :] / jnp.minimum(pos1, float(w)) - u[:, cols]).astype(jnp.bfloat16))
        for pg in range(len(B_WINDOWS) // 2):
            cols = slice(2 * pg * LANES, (2 * pg + 2) * LANES)
            yb = _dot(jnp.concatenate(pooled[2 * pg:2 * pg + 2], axis=1), pw_ref[pg]) * ps_ref[:, cols]
            yb_ref[rows, cols] = yb.astype(jnp.bfloat16)
        carry_ref[...] = u[sm - POOL_HALO:, :]


def _even_in(x2d, g, w, gsum, qg, kg, pw, ps, *, seq):
    t = x2d.shape[0]
    tm = TM_IN
    const = lambda i: (0, 0)
    row = lambda i: (i, 0)
    out = jax.ShapeDtypeStruct((t, 512), jnp.bfloat16)
    return pl.pallas_call(
        functools.partial(_even_in_kernel, tm=tm, sm=SM_IN, blocks_per_seq=seq // tm),
        grid=(t // tm,),
        in_specs=[
            pl.BlockSpec((tm, D_MODEL), row),
            pl.BlockSpec((1, D_MODEL), const),
            pl.BlockSpec((D_MODEL, 2048), const),
            pl.BlockSpec((MXU_DEPTH, MXU_DEPTH), const),
            pl.BlockSpec((1, 512), const),
            pl.BlockSpec((1, 512), const),
            pl.BlockSpec((2, MXU_DEPTH, MXU_DEPTH), lambda i: (0, 0, 0)),
            pl.BlockSpec((1, 512), const),
        ],
        out_specs=[pl.BlockSpec((tm, 512), row)] * 6,
        out_shape=[out] * 6,
        scratch_shapes=[pltpu.VMEM((POOL_HALO, 512), jnp.float32)],
        compiler_params=pltpu.CompilerParams(
            dimension_semantics=("arbitrary",), vmem_limit_bytes=VMEM_LIMIT),
        name="even_in_proj",
    )(x2d, g, w, gsum, qg, kg, pw, ps)


def _diff_attn_kernel(q1_ref, q2_ref, k1_ref, k2_ref, v_ref, lam_ref, *refs,
                      tq, tk, td, n_cast, online):
    o_ref, acc_ref = refs[n_cast], refs[-1]
    for src, dst in zip(refs[:n_cast], refs[n_cast + 1:-1]):
        dst[...] = src[...].astype(jnp.bfloat16)

    n = tq // tk
    qi = pl.program_id(2)
    qs = (q1_ref[...], q2_ref[...])
    k_refs = (k1_ref, k2_ref)

    def key_rows(j, size):
        return pl.ds(pl.multiple_of(j * size, size), size)

    def scores(j, q_list, first_key=None, size=tk):
        out = []
        for q, k_ref in zip(q_list, k_refs):
            s = _dot_nt(q, k_ref[key_rows(j, size), :])
            if first_key is not None:
                ii = lax.broadcasted_iota(jnp.int32, s.shape, 0)
                jj = lax.broadcasted_iota(jnp.int32, s.shape, 1)
                s = jnp.where(jj + first_key > ii, NEG, s)
            out.append(s)
        return out

    def pv(j, ps, size=tk):
        v1 = _with_ones(v_ref[key_rows(j, size), :])
        return [_dot(p, v1) for p in ps]

    if online:
        def tile(j, state, first_key=None):
            out = []
            for mp, s in enumerate(scores(j, qs, first_key)):
                m_new = jnp.max(s, axis=-1, keepdims=True)
                if state is not None:
                    m_new = jnp.maximum(state[2 * mp], m_new)
                out += [m_new, jnp.exp2(s - m_new).astype(jnp.bfloat16)]
            accs = pv(j, out[1::2])
            if state is not None:
                accs = [jnp.exp2(state[2 * mp] - out[2 * mp]) * state[2 * mp + 1] + accs[mp]
                        for mp in range(2)]
            return (out[0], accs[0], out[2], accs[1])

        state = None
        for t in range(n):
            state = tile(n * qi + t, state, t * tk)
        _, acc1, _, acc2 = lax.fori_loop(0, n * qi, tile, state)
    else:
        def tile(j, q_list, first_key=None, size=tk):
            ps = [jnp.exp2(s).astype(jnp.bfloat16) for s in scores(j, q_list, first_key, size)]
            return pv(j, ps, size)

        for t in range(tq // td):
            for mp, part in enumerate(tile((tq // td) * qi + t, [q[t * td:] for q in qs], 0, td)):
                if t == 0:
                    acc_ref[mp] = part
                else:
                    acc_ref[mp, t * td:, :] += part

        def n_tiles(i, carry):
            total = tile(n * i, qs)
            for t in range(1, n):
                total = [a + b for a, b in zip(total, tile(n * i + t, qs))]
            for mp in range(2):
                acc_ref[mp] += total[mp]
            return carry

        lax.fori_loop(0, qi, n_tiles, 0)
        acc1, acc2 = acc_ref[0], acc_ref[1]

    lv = lam_ref[...]
    lam = (jnp.exp(jnp.sum(lv[0:1] * lv[1:2], axis=-1, keepdims=True))
           - jnp.exp(jnp.sum(lv[2:3] * lv[3:4], axis=-1, keepdims=True)) + LAMBDA_INIT_L0)
    o = acc1[:, :A_V] / acc1[:, A_V:] - lam * (acc2[:, :A_V] / acc2[:, A_V:])
    o_ref[...] = o.astype(jnp.bfloat16)


def _diff_attn(q1, q2, k1, k2, v, lam_vecs, weights, *, bsz, seq, online):
    tq = TQ
    nq = seq // tq
    steps = bsz * A_HEADS * nq
    q_spec = pl.BlockSpec((tq, LANES), lambda b, h, i: (b * nq + i, h))
    kv_spec = pl.BlockSpec((seq, LANES), lambda b, h, i: (b, h))

    def cast_spec(w):
        nblk = max(n for n in range(1, steps + 1)
                   if steps % n == 0 and w.shape[0] % (16 * n) == 0)
        return pl.BlockSpec((w.shape[0] // nblk, w.shape[1]),
                            lambda b, h, i: (((b * A_HEADS + h) * nq + i) // (steps // nblk), 0))

    cast_specs = [cast_spec(w) for w in weights]
    out = pl.pallas_call(
        functools.partial(_diff_attn_kernel, tq=tq, tk=TK, td=TK_DIAG, n_cast=len(weights),
                          online=online),
        grid=(bsz, A_HEADS, nq),
        in_specs=[
            q_spec, q_spec, kv_spec, kv_spec, kv_spec,
            pl.BlockSpec((4, HEAD_DIM), lambda b, h, i: (0, 0)),
        ] + cast_specs,
        out_specs=[q_spec] + cast_specs,
        out_shape=[jax.ShapeDtypeStruct((bsz * seq, A_HEADS * A_V), jnp.bfloat16)]
        + [jax.ShapeDtypeStruct(w.shape, jnp.bfloat16) for w in weights],
        scratch_shapes=[pltpu.VMEM((2, tq, 2 * A_V), jnp.float32)],
        compiler_params=pltpu.CompilerParams(
            dimension_semantics=("arbitrary",) * 3, vmem_limit_bytes=VMEM_LIMIT),
        name="diff_attention_online" if online else "diff_attention",
    )(q1, q2, k1, k2, v, lam_vecs, *weights)
    return out[0], out[1:]


def _post_kernel(x_ref, ya_ref, yb_ref, wo_ref, g_ref, wg_ref, wu_ref, wd_ref, *rest):
    o_ref = rest[-1]
    half = ya_ref.shape[1]
    ya = ya_ref[...]
    if len(rest) == 2:
        heads = []
        for hd in range(half // A_V):
            o = ya[:, hd * A_V:(hd + 1) * A_V].astype(jnp.float32)
            heads.append((o * _rms_scale(o) * rest[0][...]).astype(jnp.bfloat16))
        ya = jnp.concatenate(heads, axis=1)
    x1 = x_ref[...] + _dot(ya, wo_ref[0:half, :]) + _dot(yb_ref[...], wo_ref[half:, :])
    h = (x1 * _rms_scale(x1) * g_ref[...]).astype(jnp.bfloat16)
    gate = _dot(h, wg_ref[...])
    up = _dot(h, wu_ref[...])
    act = (gate * jax.nn.sigmoid(gate) * up).astype(jnp.bfloat16)
    o_ref[...] = x1 + _dot(act, wd_ref[...])


def _post(x2d, ya, yb, wo, g, wg, wu, wd, layer, name, subnorm_gain=None):
    t = x2d.shape[0]
    tm = TM_PROJ
    row = lambda i: (i, 0)
    const = lambda i: (0, 0)
    resident = functools.partial(pl.BlockSpec, index_map=const, pipeline_mode=pl.Buffered(1))
    ffn_w = lambda rows, cols: pl.BlockSpec((None, rows, cols), lambda i: (layer, 0, 0),
                                            pipeline_mode=pl.Buffered(1))
    extra = [] if subnorm_gain is None else [subnorm_gain]
    return pl.pallas_call(
        _post_kernel,
        grid=(t // tm,),
        in_specs=[
            pl.BlockSpec((tm, D_MODEL), row),
            pl.BlockSpec((tm, ya.shape[1]), row),
            pl.BlockSpec((tm, yb.shape[1]), row),
            resident((D_MODEL, D_MODEL)),
            pl.BlockSpec((1, D_MODEL), const),
            ffn_w(D_MODEL, D_FF),
            ffn_w(D_MODEL, D_FF),
            ffn_w(D_FF, D_MODEL),
        ] + [pl.BlockSpec((1, A_V), const)] * len(extra),
        out_specs=pl.BlockSpec((tm, D_MODEL), row),
        out_shape=jax.ShapeDtypeStruct((t, D_MODEL), jnp.float32),
        compiler_params=pltpu.CompilerParams(
            dimension_semantics=("arbitrary",), vmem_limit_bytes=VMEM_LIMIT),
        name=name,
    )(x2d, ya, yb, wo, g, wg, wu, wd, *extra)


def _odd_in_kernel(x_ref, g_ref, w_ref, gsum_ref, qg_ref, kg_ref, cw_ref,
                   yc_ref, q_ref, k_ref, v_ref, carry_ref, *, tm, sm, blocks_per_seq):
    blk = pl.program_id(0) % blocks_per_seq

    @pl.when(blk == 0)
    def _():
        carry_ref[...] = jnp.zeros_like(carry_ref)

    for sb in range(tm // sm):
        rows = slice(sb * sm, (sb + 1) * sm)
        x = x_ref[rows, :]
        h = (x * _rms_scale(x) * g_ref[...]).astype(jnp.bfloat16)
        proj = lambda part: _dot(h, w_ref[:, part * 512:(part + 1) * 512])
        z = proj(1) * proj(2)
        bg = proj(0)
        ext = jnp.concatenate([carry_ref[...], z], axis=0)
        z1 = _shift_rows(ext, 1)[CONV_HALO:, :]
        z2 = _shift_rows(ext, 2)[CONV_HALO:, :]
        cw = cw_ref[...]
        yc = bg * (cw[0:1] * z2 + cw[1:2] * z1 + cw[2:3] * z)
        yc_ref[rows, :] = yc.astype(jnp.bfloat16)
        carry_ref[...] = z[sm - CONV_HALO:, :]

        q_ref[rows, :] = _group_rmsnorm(proj(3), gsum_ref, qg_ref[...]).astype(jnp.bfloat16)
        k_ref[rows, :] = _group_rmsnorm(proj(4), gsum_ref, kg_ref[...]).astype(jnp.bfloat16)
        v_ref[rows, :] = proj(5).astype(jnp.bfloat16)


def _odd_in(x2d, g, w, gsum, qg, kg, cw, *, seq):
    t = x2d.shape[0]
    tm = TM_IN
    const = lambda i: (0, 0)
    row = lambda i: (i, 0)
    out = jax.ShapeDtypeStruct((t, 512), jnp.bfloat16)
    return pl.pallas_call(
        functools.partial(_odd_in_kernel, tm=tm, sm=SM_IN, blocks_per_seq=seq // tm),
        grid=(t // tm,),
        in_specs=[
            pl.BlockSpec((tm, D_MODEL), row),
            pl.BlockSpec((1, D_MODEL), const),
            pl.BlockSpec((D_MODEL, 3072), const),
            pl.BlockSpec((MXU_DEPTH, MXU_DEPTH), const),
            pl.BlockSpec((1, 512), const),
            pl.BlockSpec((1, 512), const),
            pl.BlockSpec((3, 512), const),
        ],
        out_specs=[pl.BlockSpec((tm, 512), row)] * 4,
        out_shape=[out] * 4,
        scratch_shapes=[pltpu.VMEM((CONV_HALO, 512), jnp.float32)],
        compiler_params=pltpu.CompilerParams(
            dimension_semantics=("arbitrary",), vmem_limit_bytes=VMEM_LIMIT),
        name="odd_in_proj",
    )(x2d, g, w, gsum, qg, kg, cw)


def _pair_tile(q, kp, vp, bias):
    lane = lax.broadcasted_iota(jnp.int32, (1, LANES), 1)
    low = lane < HEAD_DIM
    zero = jnp.zeros_like(q)
    q2 = jnp.concatenate([jnp.where(low, q, zero), jnp.where(low, zero, q)], axis=0)
    p = jnp.exp2(_dot_nt(q2, kp) + bias).astype(jnp.bfloat16)
    r = _dot(p, _with_ones(vp))
    num = jnp.where(low, r[:SUB, :LANES], r[SUB:, :LANES])
    den = jnp.where(low, r[:SUB, LANES:], r[SUB:, LANES:])
    return num, den


def _dilated_fused_kernel(q_ref, k_ref, v_ref, perm_ref, tab_ref, o_ref,
                          kn, vn, k4, v4, k16, v16, qp, res4, res16):
    blk = pl.program_id(1)
    first = blk == 0
    cur = (blk & 1) * NG
    prev = NG - cur
    bf = jnp.bfloat16

    @pl.when(first)
    def _():
        kn[0:D_SPAN, :] = jnp.zeros((D_SPAN, D_WIDTH), bf)
        vn[0:D_SPAN, :] = jnp.zeros((D_SPAN, D_WIDTH), bf)
        k4[0:2] = jnp.zeros((2, GRP, D_WIDTH), bf)
        v4[0:2] = jnp.zeros((2, GRP, D_WIDTH), bf)
        k16[pl.ds(prev, NG)] = jnp.zeros((NG, GRP, D_WIDTH), bf)
        v16[pl.ds(prev, NG)] = jnp.zeros((NG, GRP, D_WIDTH), bf)

    @pl.when(jnp.logical_not(first))
    def _():
        kn[0:D_SPAN, :] = kn[DBLK:DBLK + D_SPAN, :]
        vn[0:D_SPAN, :] = vn[DBLK:DBLK + D_SPAN, :]
        k4[0:2] = k4[NG:NG + 2]
        v4[0:2] = v4[NG:NG + 2]

    kn[D_SPAN:, :] = k_ref[...]
    vn[D_SPAN:, :] = v_ref[...]
    for g in range(NG):
        rows = slice(g * GRP, (g + 1) * GRP)
        for src, d4, d16 in ((k_ref, k4, k16), (v_ref, v4, v16)):
            x = src[rows, :]
            d4[2 + g] = _dot(perm_ref[0], x).astype(bf)
            d16[cur + g] = _dot(perm_ref[2], x).astype(bf)

    kk = lax.broadcasted_iota(jnp.int32, (1, 2 * SUB), 1)
    no_halo = jnp.where(kk < D_SPAN, jnp.where(first, NEG, 0.0), 0.0)

    def pair_cols(p):
        return slice(p * LANES, (p + 1) * LANES)

    for g in range(NG):
        qp[g] = _dot(perm_ref[2], q_ref[g * GRP:(g + 1) * GRP, :]).astype(bf)

    def classes16(i, carry):
        for t in range(16):
            r = pl.ds(pl.multiple_of(i * 256 + t * 16, 16), 16)
            for p in range(D_HEADS // 2):
                cols = pair_cols(p)
                q = jnp.concatenate([qp[g, r, cols] for g in range(NG)], axis=0)
                kp = jnp.concatenate([k16[prev + g, r, cols] for g in range(NG)]
                                     + [k16[cur + g, r, cols] for g in range(NG)], axis=0)
                vp = jnp.concatenate([v16[prev + g, r, cols] for g in range(NG)]
                                     + [v16[cur + g, r, cols] for g in range(NG)], axis=0)
                num, den = _pair_tile(q, kp, vp, tab_ref[2, p] + no_halo)
                for g in range(NG):
                    res16[0, g, r, cols] = num[16 * g:16 * g + 16].astype(bf)
                    res16[1, g, r, cols] = den[16 * g:16 * g + 16].astype(bf)
        return carry

    lax.fori_loop(0, 1, classes16, 0)

    for g in range(NG):
        qp[g] = _dot(perm_ref[0], q_ref[g * GRP:(g + 1) * GRP, :]).astype(bf)

    def classes4(i, carry):
        for t in range(4):
            r = pl.ds(pl.multiple_of(i * 256 + t * 64, 64), 64)
            for qs in range(NG // 2):
                for p in range(D_HEADS // 2):
                    cols = pair_cols(p)
                    q = jnp.concatenate([qp[2 * qs + j, r, cols] for j in range(2)], axis=0)
                    kp = jnp.concatenate([k4[2 * qs + j, r, cols] for j in range(4)], axis=0)
                    vp = jnp.concatenate([v4[2 * qs + j, r, cols] for j in range(4)], axis=0)
                    bias = tab_ref[1, p] + no_halo if qs == 0 else tab_ref[1, p]
                    num, den = _pair_tile(q, kp, vp, bias)
                    for j in range(2):
                        res4[0, 2 * qs + j, r, cols] = num[64 * j:64 * j + 64].astype(bf)
                        res4[1, 2 * qs + j, r, cols] = den[64 * j:64 * j + 64].astype(bf)
        return carry

    lax.fori_loop(0, 1, classes4, 0)

    def groups(i, carry):
        for u in range(NG):
            g = NG * i + u
            num = _dot(perm_ref[1], res4[0, g]) + _dot(perm_ref[3], res16[0, g])
            den = _dot(perm_ref[1], res4[1, g]) + _dot(perm_ref[3], res16[1, g])
            for t in range(GRP // SUB):
                q_rows = pl.ds(pl.multiple_of(g * GRP + t * SUB, SUB), SUB)
                k_rows = pl.ds(pl.multiple_of(g * GRP + t * SUB, SUB), 2 * SUB)
                sub = slice(t * SUB, (t + 1) * SUB)
                for p in range(D_HEADS // 2):
                    cols = pair_cols(p)
                    bias = tab_ref[0, p]
                    if u == 0 and t == 0:
                        bias = bias + jnp.where(i == 0, no_halo, 0.0)
                    n1, d1 = _pair_tile(q_ref[q_rows, cols], kn[k_rows, cols], vn[k_rows, cols], bias)
                    o_ref[q_rows, cols] = ((n1 + num[sub, cols]) / (d1 + den[sub, cols])).astype(bf)
        return carry

    lax.fori_loop(0, 1, groups, 0)


def _class_permutations():
    mats = []
    out = np.arange(GRP)
    for d in (4, 16):
        per = GRP // d
        p = np.zeros((GRP, GRP), np.float32)
        p[out, d * (out % per) + out // per] = 1.0
        mats += [p, p.T]
    return jnp.asarray(np.stack(mats), jnp.bfloat16)


def _dilated_bias_table(dil):
    ii, kk = np.indices((SUB, SUB + D_SPAN))
    delta = D_SPAN + ii - kk
    slopes = np.asarray(_alibi_slopes(D_HEADS), np.float32)
    bias = np.float32(-LOG2E) * slopes[:, None, None] * (delta * dil).astype(np.float32)[None]
    return np.where(((delta >= 0) & (delta <= D_SPAN))[None], bias, np.float32(NEG))


def _dilated_fused(q, k, v, *, bsz, seq):
    nblk = seq // DBLK
    tabs = jnp.asarray(np.stack([_dilated_bias_table(d).reshape(D_HEADS // 2, 2 * SUB, SUB + D_SPAN)
                                 for d in D_DILATIONS]))
    blk_spec = pl.BlockSpec((DBLK, D_WIDTH), lambda b, i: (b * nblk + i, 0))
    group_buf = lambda n: pltpu.VMEM((n, GRP, D_WIDTH), jnp.bfloat16)
    return pl.pallas_call(
        _dilated_fused_kernel,
        grid=(bsz, nblk),
        in_specs=[
            blk_spec, blk_spec, blk_spec,
            pl.BlockSpec((4, GRP, GRP), lambda b, i: (0, 0, 0), pipeline_mode=pl.Buffered(1)),
            pl.BlockSpec(tabs.shape, lambda b, i: (0, 0, 0, 0), pipeline_mode=pl.Buffered(1)),
        ],
        out_specs=blk_spec,
        out_shape=jax.ShapeDtypeStruct((bsz * seq, D_WIDTH), jnp.bfloat16),
        scratch_shapes=[
            pltpu.VMEM((D_SPAN + DBLK, D_WIDTH), jnp.bfloat16),
            pltpu.VMEM((D_SPAN + DBLK, D_WIDTH), jnp.bfloat16),
            group_buf(2 + NG), group_buf(2 + NG),
            group_buf(2 * NG), group_buf(2 * NG),
            group_buf(NG),
            pltpu.VMEM((2, NG, GRP, D_WIDTH), jnp.bfloat16),
            pltpu.VMEM((2, NG, GRP, D_WIDTH), jnp.bfloat16),
        ],
        compiler_params=pltpu.CompilerParams(
            dimension_semantics=("arbitrary",) * 2, vmem_limit_bytes=VMEM_LIMIT),
        name="dilated_attention_fused",
    )(q, k, v, _class_permutations(), tabs)


def _dilated_kernel(q_ref, kh_ref, kc_ref, vh_ref, vc_ref, bias_ref, o_ref, lse_ref,
                    kbuf, vbuf, *, tl):
    j = pl.program_id(2)
    kbuf[0:D_SPAN, :] = kh_ref[...]
    kbuf[D_SPAN:, :] = kc_ref[...]
    vbuf[0:D_SPAN, :] = vh_ref[...]
    vbuf[D_SPAN:, :] = vc_ref[...]
    lane = lax.broadcasted_iota(jnp.int32, (1, LANES), 1)
    kk = lax.broadcasted_iota(jnp.int32, (1, SUB + D_SPAN), 1)
    halo_pen = jnp.where(kk < D_SPAN, jnp.where(j == 0, NEG, 0.0), 0.0)
    for qs in range(tl // SUB):
        rows = slice(qs * SUB, (qs + 1) * SUB)
        krows = slice(qs * SUB, (qs + 1) * SUB + D_SPAN)
        lse_tile = jnp.zeros((SUB, LANES), jnp.float32)
        for p in range(D_HEADS // 2):
            cols = slice(p * LANES, (p + 1) * LANES)
            q = q_ref[rows, cols]
            kp = kbuf[krows, cols]
            vp = vbuf[krows, cols]
            zero = jnp.zeros_like(q)
            halves = []
            for e in range(2):
                hd = 2 * p + e
                qm = jnp.where((lane < HEAD_DIM) if e == 0 else (lane >= HEAD_DIM), q, zero)
                s = _dot_nt(qm, kp) + bias_ref[hd]
                if qs == 0:
                    s = s + halo_pen
                m = jnp.max(s, axis=-1, keepdims=True)
                pe = jnp.exp2(s - m)
                l = jnp.sum(pe, axis=-1, keepdims=True)
                halves.append(_dot(pe.astype(jnp.bfloat16), vp) / l)
                lse_tile = jnp.where(lane == hd, m + jnp.log2(l), lse_tile)
            o_ref[rows, cols] = jnp.where(lane < HEAD_DIM, halves[0], halves[1]).astype(jnp.bfloat16)
        lse_ref[rows, :] = lse_tile


def _dilated_branch(q, k, v, *, bsz, seq, dil):
    tl = TL
    length = seq // dil
    nb = length // tl
    halo_per_blk = tl // D_SPAN
    view = lambda a: a.reshape(bsz, length, dil * D_WIDTH)
    cur = lambda b, r, j: (b, j, r)
    halo = lambda b, r, j: (b, jnp.maximum(j * halo_per_blk - 1, 0), r)
    o, lse = pl.pallas_call(
        functools.partial(_dilated_kernel, tl=tl),
        grid=(bsz, dil, nb),
        in_specs=[
            pl.BlockSpec((None, tl, D_WIDTH), cur),
            pl.BlockSpec((None, D_SPAN, D_WIDTH), halo),
            pl.BlockSpec((None, tl, D_WIDTH), cur),
            pl.BlockSpec((None, D_SPAN, D_WIDTH), halo),
            pl.BlockSpec((None, tl, D_WIDTH), cur),
            pl.BlockSpec((D_HEADS, SUB, SUB + D_SPAN), lambda b, r, j: (0, 0, 0)),
        ],
        out_specs=[
            pl.BlockSpec((None, tl, D_WIDTH), cur),
            pl.BlockSpec((None, tl, LANES), cur),
        ],
        out_shape=[
            jax.ShapeDtypeStruct((bsz, length, dil * D_WIDTH), jnp.bfloat16),
            jax.ShapeDtypeStruct((bsz, length, dil * LANES), jnp.float32),
        ],
        scratch_shapes=[pltpu.VMEM((tl + D_SPAN, D_WIDTH), jnp.bfloat16)] * 2,
        compiler_params=pltpu.CompilerParams(
            dimension_semantics=("arbitrary",) * 3, vmem_limit_bytes=VMEM_LIMIT),
        name=f"dilated_attention_d{dil}",
    )(view(q), view(k), view(k), view(v), view(v), jnp.asarray(_dilated_bias_table(dil)))
    return o.reshape(bsz * seq, D_WIDTH), lse.reshape(bsz * seq, LANES)


def _branch_combine_kernel(o1_ref, o2_ref, o3_ref, l1_ref, l2_ref, l3_ref, y_ref):
    lses = [l1_ref[...], l2_ref[...], l3_ref[...]]
    mx = jnp.maximum(jnp.maximum(lses[0], lses[1]), lses[2])
    es = [jnp.exp2(l - mx) for l in lses]
    inv = 1.0 / (es[0] + es[1] + es[2])
    ws = [e * inv for e in es]
    lane = lax.broadcasted_iota(jnp.int32, (1, LANES), 1)
    for p in range(D_HEADS // 2):
        cols = slice(p * LANES, (p + 1) * LANES)
        yd = None
        for w, o in zip(ws, (o1_ref, o2_ref, o3_ref)):
            wf = jnp.where(lane < HEAD_DIM, w[:, 2 * p:2 * p + 1], w[:, 2 * p + 1:2 * p + 2])
            term = wf * o[:, cols].astype(jnp.float32)
            yd = term if yd is None else yd + term
        y_ref[:, cols] = yd.astype(jnp.bfloat16)


def _dilated_by_branch(q, k, v, *, bsz, seq):
    outs, lses = zip(*[_dilated_branch(q, k, v, bsz=bsz, seq=seq, dil=d) for d in D_DILATIONS])
    tm = TM_PROJ
    row = lambda i: (i, 0)
    return pl.pallas_call(
        _branch_combine_kernel,
        grid=(bsz * seq // tm,),
        in_specs=[pl.BlockSpec((tm, D_WIDTH), row)] * 3 + [pl.BlockSpec((tm, LANES), row)] * 3,
        out_specs=pl.BlockSpec((tm, D_WIDTH), row),
        out_shape=jax.ShapeDtypeStruct((bsz * seq, D_WIDTH), jnp.bfloat16),
        compiler_params=pltpu.CompilerParams(
            dimension_semantics=("arbitrary",), vmem_limit_bytes=VMEM_LIMIT),
        name="dilated_branch_combine",
    )(*outs, *lses)


def kernel(x, norm_mix, norm_ffn, ev_w_in, ev_w_out, ev_q_gain, ev_k_gain, ev_lambda_q1, ev_lambda_k1, ev_lambda_q2, ev_lambda_k2, ev_subln_gain, ev_pool_w, ev_pool_scale, od_w_in, od_w_out, od_conv_w, od_q_gain, od_k_gain, ffn_w_gate, ffn_w_up, ffn_w_down):
    bsz, seq, _ = x.shape
    bf = jnp.bfloat16
    f32 = jnp.float32
    q_scale = HEAD_DIM ** -0.5 * LOG2E
    x2d = x.reshape(bsz * seq, D_MODEL)
    grp = np.arange(MXU_DEPTH) // HEAD_DIM
    gsum = jnp.asarray(np.where(grp[:, None] == grp[None, :], 1.0 / HEAD_DIM, 0.0), bf)
    pw = ev_pool_w[0].astype(bf)
    zpw = jnp.zeros_like(pw[0])
    pool_w = jnp.stack([jnp.block([[pw[2 * i], zpw], [zpw, pw[2 * i + 1]]]) for i in range(2)])
    tile8 = lambda gain, s: (jnp.tile(gain.astype(f32), 8) * s).reshape(1, 512)

    q1, q2, k1, k2, v, yb = _even_in(
        x2d, norm_mix[0].reshape(1, D_MODEL).astype(f32), ev_w_in[0].astype(bf), gsum,
        tile8(ev_q_gain[0], q_scale), tile8(ev_k_gain[0], 1.0),
        pool_w, ev_pool_scale[0].reshape(1, 512).astype(f32), seq=seq)
    lam_vecs = jnp.stack([ev_lambda_q1[0], ev_lambda_k1[0], ev_lambda_q2[0], ev_lambda_k2[0]]).astype(f32)
    later = [ffn_w_gate.reshape(-1, D_FF), ffn_w_up.reshape(-1, D_FF), ffn_w_down.reshape(-1, D_MODEL),
             ev_w_out[0], od_w_out[0], od_w_in[0]]
    attn = functools.partial(_diff_attn, q1, q2, k1, k2, v, lam_vecs, later, bsz=bsz, seq=seq)
    ya, (w_gate, w_up, w_down, ev_wo, od_wo, od_wi) = lax.cond(
        _score_bound(ev_q_gain[0], ev_k_gain[0]) <= MAX_UNSHIFTED_SCORE,
        lambda: attn(online=False), lambda: attn(online=True))
    ffn_w = (w_gate.reshape(ffn_w_gate.shape), w_up.reshape(ffn_w_up.shape),
             w_down.reshape(ffn_w_down.shape))
    x2d = _post(x2d, ya, yb, ev_wo,
                norm_ffn[0].reshape(1, D_MODEL).astype(f32), *ffn_w, 0, "even_post",
                subnorm_gain=(ev_subln_gain[0].astype(f32) * (1.0 - LAMBDA_INIT_L0)).reshape(1, A_V))

    yc, q, k, v = _odd_in(
        x2d, norm_mix[1].reshape(1, D_MODEL).astype(f32), od_wi, gsum,
        tile8(od_q_gain[0], q_scale), tile8(od_k_gain[0], 1.0), od_conv_w[0].astype(f32), seq=seq)
    yd = lax.cond(_score_bound(od_q_gain[0], od_k_gain[0]) <= MAX_UNSHIFTED_SCORE,
                  lambda: _dilated_fused(q, k, v, bsz=bsz, seq=seq),
                  lambda: _dilated_by_branch(q, k, v, bsz=bsz, seq=seq))
    x2d = _post(x2d, yc, yd, od_wo,
                norm_ffn[1].reshape(1, D_MODEL).astype(f32), *ffn_w, 1, "odd_post")
    return x2d.reshape(bsz, seq, D_MODEL)
```
